```python
import jax
import jax.numpy as jnp
from jax import lax
import numpy as np

D_MODEL = 4096
BATCH = 4
SEQ = 2048
DEPTH = 1
DEC_BATCH = 128
DEC_SEQ = 8
PAST_LEN = 16384
PAGE_SIZE = 128

GLA_WIDTH = D_MODEL // 2
GLA_HEADS = 4
GLA_DK = GLA_WIDTH // 2 // GLA_HEADS
GLA_DV = GLA_WIDTH // GLA_HEADS
GLA_QK = GLA_HEADS * GLA_DK
GLA_GATE_RANK = 16
GLA_TAU = 16.0
GLA_CHUNK = 64
RWKV_WIDTH = D_MODEL // 2
RWKV_HEAD = 64
RWKV_HEADS = RWKV_WIDTH // RWKV_HEAD
RWKV_W_LORA = 96
RWKV_A_LORA = 96
RWKV_G_LORA = 256
RWKV_SIZES = (RWKV_WIDTH, RWKV_W_LORA, RWKV_WIDTH, RWKV_WIDTH, RWKV_A_LORA, RWKV_G_LORA)
RWKV_COLS = sum(RWKV_SIZES)
RWKV_SPLITS = tuple(int(s) for s in np.cumsum(RWKV_SIZES)[:-1])
GN_EPS = 64e-5
IN_SIZES = (GLA_QK, GLA_QK, GLA_WIDTH, GLA_GATE_RANK, GLA_WIDTH, RWKV_COLS, D_MODEL, D_MODEL)
IN_TOTAL = sum(IN_SIZES)
IN_SPLITS = tuple(int(s) for s in np.cumsum(IN_SIZES)[:-1])
D_FF = 11008
CONV_W = 3
P_DIM = 256
EPS = 1e-6

kernel_name = 'hybrid_gla_rwkv7_decoder_step'


def rmsnorm(x, gain):
    xf = x.astype(jnp.float32)
    y = xf * lax.rsqrt(jnp.mean(xf * xf, axis=-1, keepdims=True) + EPS)
    return (y * gain.astype(jnp.float32)).astype(x.dtype)


def gla_chunked(q, k, v, log_a, s0):
    B, L, H, _ = q.shape
    c = min(GLA_CHUNK, L)
    n = -(-L // c)
    pad = n * c - L

    def blocks(t):
        t = jnp.pad(t, ((0, 0), (0, pad), (0, 0), (0, 0)))
        return t.reshape(B, n, c, H, t.shape[-1]).transpose(1, 0, 3, 2, 4)

    causal = jnp.tril(jnp.ones((c, c), dtype=bool))

    def step(S, blk):
        qb, kb, vb, ab = blk
        b = jnp.cumsum(ab, axis=2)
        b_end = b[:, :, -1:, :]
        qd = qb * jnp.exp(b)
        kd = kb * jnp.exp(-b)
        att = jnp.where(causal, jnp.einsum('bhtd,bhsd->bhts', qd, kd), 0.0)
        o = jnp.einsum('bhts,bhsv->bhtv', att, vb) + jnp.einsum('bhtd,bhdv->bhtv', qd, S)
        S = S * jnp.exp(b_end[:, :, 0, :, None]) + jnp.einsum('bhsd,bhsv->bhdv', kb * jnp.exp(b_end - b), vb)
        return S, o

    S, o = lax.scan(step, s0.astype(jnp.float32), (blocks(q), blocks(k), blocks(v), blocks(log_a)))
    o = o.transpose(1, 0, 3, 2, 4).reshape(B, n * c, H, -1)[:, :L]
    return o, S


def rwkv7_scan(r, w, k, v, kk, a, s0):
    def step(S, inp):
        r_t, w_t, k_t, v_t, kk_t, a_t = inp
        sa = jnp.einsum('bhvk,bhk->bhv', S, -kk_t)
        S = (S * w_t[:, :, None, :] + sa[..., None] * (kk_t * a_t)[:, :, None, :]
             + v_t[..., None] * k_t[:, :, None, :])
        return S, jnp.einsum('bhvk,bhk->bhv', S, r_t)

    xs = tuple(jnp.swapaxes(t, 0, 1) for t in (r, w, k, v, kk, a))
    S, o = lax.scan(step, s0.astype(jnp.float32), xs)
    return jnp.swapaxes(o, 0, 1), S


def layer(x, p, s_gla, s_rwkv, s_shift, s_conv,
          w_in, w_alpha2, b_alpha, gla_norm, w_branch_a,
          mu_shift, w0, w_decay2, a0, w_iclr2, w_gate2, k_k, k_a, r_k, ln_x_w, ln_x_b, w_branch_b,
          w_out, g_pre_mix, g_post_mix, g_pre_ffn, g_post_ffn,
          w_up, conv_w, conv_b, w_down, g_pe, w_pe_gate, w_pe):
    B, L, _ = x.shape
    f32 = jnp.float32
    h = rmsnorm(x, g_pre_mix)
    proj = h @ w_in
    q, k, v, za, zg, zr, gate_a, gate_b = jnp.split(proj, IN_SPLITS, axis=-1)

    q = q.reshape(B, L, GLA_HEADS, GLA_DK).astype(f32) * (GLA_DK ** -0.5)
    k = k.reshape(B, L, GLA_HEADS, GLA_DK).astype(f32)
    v = v.reshape(B, L, GLA_HEADS, GLA_DV).astype(f32)
    log_a = (jax.nn.log_sigmoid((za @ w_alpha2 + b_alpha).astype(f32)) / GLA_TAU).reshape(B, L, GLA_HEADS, GLA_DK)
    o_a, s_gla_new = gla_chunked(q, k, v, log_a, s_gla)
    o_a = o_a * lax.rsqrt(jnp.mean(o_a * o_a, axis=-1, keepdims=True) + EPS) * gla_norm.astype(f32)
    o_a = o_a.reshape(B, L, GLA_WIDTH).astype(x.dtype) * jax.nn.silu(zg)
    y_a = o_a @ w_branch_a

    prev = jnp.concatenate([s_shift[:, None, :].astype(zr.dtype), zr[:, :-1]], axis=1)
    zs = zr + (prev - zr) * mu_shift
    new_shift = zr[:, -1]
    r, xw, kr, vr, xa, xg = jnp.split(zs, RWKV_SPLITS, axis=-1)
    w_log = -jax.nn.softplus(-(w0 + jnp.tanh(xw) @ w_decay2).astype(f32)) - 0.5
    decay = jnp.exp(-jnp.exp(w_log))
    a = jax.nn.sigmoid((a0 + xa @ w_iclr2).astype(f32))
    g = jax.nn.sigmoid(xg) @ w_gate2
    kr = kr.astype(f32)

    def heads(t):
        return t.reshape(B, L, RWKV_HEADS, RWKV_HEAD)

    kk = heads(kr * k_k.astype(f32))
    kk = kk / jnp.maximum(jnp.sqrt(jnp.sum(kk * kk, axis=-1, keepdims=True)), 1e-12)
    kr = kr * (1.0 + (a - 1.0) * k_a.astype(f32))
    rh, kh, vh = heads(r.astype(f32)), heads(kr), heads(vr.astype(f32))
    o_b, s_rwkv_new = rwkv7_scan(rh, heads(decay), kh, vh, kk, heads(a), s_rwkv)
    mu = jnp.mean(o_b, axis=-1, keepdims=True)
    var = jnp.mean(jnp.square(o_b - mu), axis=-1, keepdims=True)
    o_b = ((o_b - mu) * lax.rsqrt(var + GN_EPS)).reshape(B, L, RWKV_WIDTH) * ln_x_w + ln_x_b
    bonus = jnp.sum(rh * kh * r_k.astype(f32), axis=-1, keepdims=True) * vh
    o_b = o_b + bonus.reshape(B, L, RWKV_WIDTH)
    y_b = (o_b.astype(x.dtype) * g) @ w_branch_b

    mixed = jax.nn.sigmoid(gate_a) * y_a + jax.nn.sigmoid(gate_b) * y_b
    x = x + rmsnorm(mixed @ w_out, g_post_mix)

    hf = rmsnorm(x, g_pre_ffn)
    up_g, up_v = jnp.split(hf @ w_up, 2, axis=-1)
    gpad = jnp.concatenate([s_conv.astype(up_g.dtype), up_g], axis=1)
    conv = conv_b + gpad[:, 0:L] * conv_w[0]
    for j in range(1, CONV_W):
        conv = conv + gpad[:, j:j + L] * conv_w[j]
    new_conv = gpad[:, L:]
    f = (jax.nn.gelu(conv) * up_v) @ w_down
    x = x + rmsnorm(f, g_post_ffn)

    x = x + jax.nn.sigmoid(rmsnorm(x, g_pe) @ w_pe_gate) * (p.astype(x.dtype) @ w_pe)
    return x, s_gla_new, s_rwkv_new, new_shift, new_conv


def setup_inputs(seed: int = 0) -> dict:
    key = jax.random.key(seed)
    ks = iter(jax.random.split(key, 48))
    f32 = jnp.float32

    def nrm(shape, scale):
        return jax.random.normal(next(ks), shape, f32) * scale

    def unif(shape, lo, hi):
        return jax.random.uniform(next(ks), shape, f32, lo, hi)

    def gain(n):
        return 1.0 + nrm((DEPTH, n), 0.02)

    return {
        'x_prompt': nrm((BATCH, SEQ, D_MODEL), 1.0),
        'x_sample': nrm((DEC_BATCH, DEC_SEQ, D_MODEL), 1.0),
        'state_gla': nrm((DEPTH, DEC_BATCH, GLA_HEADS, GLA_DK, GLA_DV), 0.5),
        'state_rwkv': nrm((DEPTH, DEC_BATCH, RWKV_HEADS, RWKV_HEAD, RWKV_HEAD), 0.3),
        'state_shift': nrm((DEPTH, DEC_BATCH, RWKV_COLS), 1.0),
        'state_ffn_conv': nrm((DEPTH, DEC_BATCH, CONV_W - 1, D_FF), 1.0),
        'p_prompt': nrm((DEPTH, BATCH, SEQ, P_DIM), 1.0),
        'p_sample': nrm((DEPTH, DEC_BATCH, DEC_SEQ, P_DIM), 1.0),
        'w_in': nrm((DEPTH, D_MODEL, IN_TOTAL), D_MODEL ** -0.5),
        'w_alpha2': nrm((DEPTH, GLA_GATE_RANK, GLA_QK), GLA_GATE_RANK ** -0.5),
        'b_alpha': unif((DEPTH, GLA_QK), -1.0, 4.0),
        'gla_norm': gain(GLA_DV),
        'w_branch_a': nrm((DEPTH, GLA_WIDTH, D_MODEL), GLA_WIDTH ** -0.5),
        'mu_shift': unif((DEPTH, RWKV_COLS), 0.0, 1.0),
        'w0': unif((DEPTH, RWKV_WIDTH), -4.0, 1.0),
        'w_decay2': nrm((DEPTH, RWKV_W_LORA, RWKV_WIDTH), 0.1),
        'a0': nrm((DEPTH, RWKV_WIDTH), 0.1),
        'w_iclr2': nrm((DEPTH, RWKV_A_LORA, RWKV_WIDTH), RWKV_A_LORA ** -0.5),
        'w_gate2': nrm((DEPTH, RWKV_G_LORA, RWKV_WIDTH), RWKV_G_LORA ** -0.5),
        'k_k': 0.85 + nrm((DEPTH, RWKV_WIDTH), 0.02),
        'k_a': gain(RWKV_WIDTH),
        'r_k': nrm((DEPTH, RWKV_HEADS, RWKV_HEAD), 0.1),
        'ln_x_w': gain(RWKV_WIDTH),
        'ln_x_b': nrm((DEPTH, RWKV_WIDTH), 0.02),
        'w_branch_b': nrm((DEPTH, RWKV_WIDTH, D_MODEL), RWKV_WIDTH ** -0.5),
        'w_out': nrm((DEPTH, D_MODEL, D_MODEL), D_MODEL ** -0.5),
        'g_pre_mix': gain(D_MODEL),
        'g_post_mix': gain(D_MODEL),
        'g_pre_ffn': gain(D_MODEL),
        'g_post_ffn': gain(D_MODEL),
        'w_up': nrm((DEPTH, D_MODEL, 2 * D_FF), D_MODEL ** -0.5),
        'conv_w': nrm((DEPTH, CONV_W, D_FF), CONV_W ** -0.5),
        'conv_b': nrm((DEPTH, D_FF), 0.02),
        'w_down': nrm((DEPTH, D_FF, D_MODEL), D_FF ** -0.5),
        'g_pe': gain(D_MODEL),
        'w_pe_gate': nrm((DEPTH, D_MODEL, D_MODEL), D_MODEL ** -0.5),
        'w_pe': nrm((DEPTH, P_DIM, D_MODEL), P_DIM ** -0.5),
    }


def reference(x_prompt, x_sample, state_gla, state_rwkv, state_shift, state_ffn_conv, p_prompt, p_sample,
              w_in, w_alpha2, b_alpha, gla_norm, w_branch_a,
              mu_shift, w0, w_decay2, a0, w_iclr2, w_gate2, k_k, k_a, r_k, ln_x_w, ln_x_b, w_branch_b,
              w_out, g_pre_mix, g_post_mix, g_pre_ffn, g_post_ffn,
              w_up, conv_w, conv_b, w_down, g_pe, w_pe_gate, w_pe):
    params = (w_in, w_alpha2, b_alpha, gla_norm, w_branch_a,
              mu_shift, w0, w_decay2, a0, w_iclr2, w_gate2, k_k, k_a, r_k, ln_x_w, ln_x_b, w_branch_b,
              w_out, g_pre_mix, g_post_mix, g_pre_ffn, g_post_ffn,
              w_up, conv_w, conv_b, w_down, g_pe, w_pe_gate, w_pe)
    yp, ys = x_prompt, x_sample
    gla_p, rwkv_p, shift_p, conv_p = [], [], [], []
    gla_s, rwkv_s, shift_s, conv_s = [], [], [], []
    for i in range(DEPTH):
        lp = tuple(t[i] for t in params)
        z_gla = jnp.zeros((BATCH, GLA_HEADS, GLA_DK, GLA_DV), jnp.float32)
        z_rwkv = jnp.zeros((BATCH, RWKV_HEADS, RWKV_HEAD, RWKV_HEAD), jnp.float32)
        z_shift = jnp.zeros((BATCH, RWKV_COLS), x_prompt.dtype)
        z_conv = jnp.zeros((BATCH, CONV_W - 1, D_FF), x_prompt.dtype)
        yp, sg, sr, ss, sc = layer(yp, p_prompt[i], z_gla, z_rwkv, z_shift, z_conv, *lp)
        gla_p.append(sg); rwkv_p.append(sr); shift_p.append(ss); conv_p.append(sc)
        ys, sg, sr, ss, sc = layer(ys, p_sample[i], state_gla[i], state_rwkv[i], state_shift[i],
                                   state_ffn_conv[i], *lp)
        gla_s.append(sg); rwkv_s.append(sr); shift_s.append(ss); conv_s.append(sc)
    return (yp, ys,
            jnp.stack(gla_p), jnp.stack(rwkv_p), jnp.stack(shift_p), jnp.stack(conv_p),
            jnp.stack(gla_s), jnp.stack(rwkv_s), jnp.stack(shift_s), jnp.stack(conv_s))
```

```python
import functools
import math

import jax
import jax.numpy as jnp
from jax import lax
from jax.experimental import pallas as pl
from jax.experimental.pallas import tpu as pltpu

F32 = jnp.float32
BF16 = jnp.bfloat16

LANE = 128
SUBLANE = 8
VMEM_LIMIT_BYTES = 56 * 1024 * 1024

EPS = 1e-6
GN_EPS = 64e-5
GLA_TAU = 16.0
GLA_CHUNK = 64
RWKV_CHUNK = 64
KK_EPS = 1e-12
DECAY_SCALE = math.exp(-0.5)
GELU_C = math.sqrt(2.0 / math.pi)


def _cparams(sem):
    return pltpu.CompilerParams(dimension_semantics=sem, vmem_limit_bytes=VMEM_LIMIT_BYTES)


def _sigmoid(x):
    return 1.0 / (1.0 + jnp.exp(-x))


def _log_sigmoid(x):
    return jnp.minimum(x, 0.0) - jnp.log(1.0 + jnp.exp(-jnp.abs(x)))


def _gelu_tanh(x):
    return 0.5 * x * (1.0 + jnp.tanh(GELU_C * (x + 0.044715 * (x * x * x))))


def _split2(x):
    hi = x.astype(BF16)
    lo = (x - hi.astype(F32)).astype(BF16)
    return hi, lo


def _dot(a, b):
    return jnp.dot(a, b, preferred_element_type=F32)


def _dot_nt(a, b):
    return lax.dot_general(a, b, (((1,), (1,)), ((), ())), preferred_element_type=F32)


def _dot_tn(a, b):
    return lax.dot_general(a, b, (((0,), (0,)), ((), ())), preferred_element_type=F32)


def _ones_dot(m01, x):
    hi, lo = _split2(x)
    return _dot(m01, hi) + _dot(m01, lo)


def _dot_ones(x, m01):
    hi, lo = _split2(x)
    return _dot(hi, m01) + _dot(lo, m01)


def _tri_incl(n):
    r = lax.broadcasted_iota(jnp.int32, (n, n), 0)
    c = lax.broadcasted_iota(jnp.int32, (n, n), 1)
    return r >= c


def _pick_tile(n, target, mult):
    if n <= target:
        return n
    best = None
    t = mult
    while t <= target:
        if n % t == 0:
            best = t
        t += mult
    assert best is not None, (n, target, mult)
    return best


def _rms_rows(x_ref, g_ref, h_ref, rows):
    tm = x_ref.shape[0]

    def body(i, _):
        sl = pl.ds(pl.multiple_of(i * rows, rows), rows)
        x = x_ref[sl, :]
        ms = jnp.mean(x * x, axis=-1, keepdims=True)
        h_ref[sl, :] = (x * lax.rsqrt(ms + EPS) * g_ref[...]).astype(BF16)
        return 0

    lax.fori_loop(0, tm // rows, body, 0)


def _norm_mm_kernel(x_ref, g_ref, w_ref, o_ref, h_ref, *, rows):
    @pl.when(pl.program_id(1) == 0)
    def _():
        _rms_rows(x_ref, g_ref, h_ref, rows)

    o_ref[...] = _dot(h_ref[...], w_ref[...])


def _norm_matmul(x, gain, w, *, tm=512, tn=512):
    M, K = x.shape
    N = w.shape[1]
    tm = _pick_tile(M, tm, SUBLANE)
    tn = _pick_tile(N, tn, LANE)
    rows = _pick_tile(tm, 64, SUBLANE)
    return pl.pallas_call(
        functools.partial(_norm_mm_kernel, rows=rows),
        grid=(M // tm, N // tn),
        in_specs=[pl.BlockSpec((tm, K), lambda i, j: (i, 0)),
                  pl.BlockSpec((1, K), lambda i, j: (0, 0)),
                  pl.BlockSpec((K, tn), lambda i, j: (0, j))],
        out_specs=pl.BlockSpec((tm, tn), lambda i, j: (i, j)),
        out_shape=jax.ShapeDtypeStruct((M, N), F32),
        scratch_shapes=[pltpu.VMEM((tm, K), BF16)],
        compiler_params=_cparams(("parallel", "arbitrary")),
        name="norm_matmul",
    )(x, gain.reshape(1, K), w)


def _pe_kernel(x_ref, g_ref, w_ref, p_ref, wp_ref, o_ref, h_ref, *, rows, tn):
    j = pl.program_id(1)

    @pl.when(j == 0)
    def _():
        _rms_rows(x_ref, g_ref, h_ref, rows)

    gate = _sigmoid(_dot(h_ref[...], w_ref[...]))
    pe = _dot(p_ref[...].astype(BF16), wp_ref[...])
    xs = x_ref[:, pl.ds(pl.multiple_of(j * tn, tn), tn)]
    o_ref[...] = xs + gate * pe


def _pe_layer(x, gain, w_gate, p, w_pe, *, tm=512, tn=512):
    M, K = x.shape
    N = w_gate.shape[1]
    P = p.shape[1]
    tm = _pick_tile(M, tm, SUBLANE)
    tn = _pick_tile(N, tn, LANE)
    rows = _pick_tile(tm, 64, SUBLANE)
    return pl.pallas_call(
        functools.partial(_pe_kernel, rows=rows, tn=tn),
        grid=(M // tm, N // tn),
        in_specs=[pl.BlockSpec((tm, K), lambda i, j: (i, 0)),
                  pl.BlockSpec((1, K), lambda i, j: (0, 0)),
                  pl.BlockSpec((K, tn), lambda i, j: (0, j)),
                  pl.BlockSpec((tm, P), lambda i, j: (i, 0)),
                  pl.BlockSpec((P, tn), lambda i, j: (0, j))],
        out_specs=pl.BlockSpec((tm, tn), lambda i, j: (i, j)),
        out_shape=jax.ShapeDtypeStruct((M, N), F32),
        scratch_shapes=[pltpu.VMEM((tm, K), BF16)],
        compiler_params=_cparams(("parallel", "arbitrary")),
        name="pe_layer",
    )(x, gain.reshape(1, K), w_gate, p, w_pe)


def _mm_norm_res_kernel(a_ref, w_ref, x_ref, g_ref, o_ref, *, tn, rows):
    j = pl.program_id(1)
    o_ref[:, pl.ds(pl.multiple_of(j * tn, tn), tn)] = _dot(a_ref[...], w_ref[...])

    @pl.when(j == pl.num_programs(1) - 1)
    def _():
        tm = o_ref.shape[0]

        def body(i, _):
            sl = pl.ds(pl.multiple_of(i * rows, rows), rows)
            y = o_ref[sl, :]
            ms = jnp.mean(y * y, axis=-1, keepdims=True)
            o_ref[sl, :] = x_ref[sl, :] + y * lax.rsqrt(ms + EPS) * g_ref[...]
            return 0

        lax.fori_loop(0, tm // rows, body, 0)


def _matmul_norm_residual(a, w, x, gain, *, tm, tn):
    M, K = a.shape
    N = w.shape[1]
    tm = _pick_tile(M, tm, SUBLANE)
    tn = _pick_tile(N, tn, LANE)
    rows = _pick_tile(tm, 64, SUBLANE)
    return pl.pallas_call(
        functools.partial(_mm_norm_res_kernel, tn=tn, rows=rows),
        grid=(M // tm, N // tn),
        in_specs=[pl.BlockSpec((tm, K), lambda i, j: (i, 0)),
                  pl.BlockSpec((K, tn), lambda i, j: (0, j)),
                  pl.BlockSpec((tm, N), lambda i, j: (i, 0)),
                  pl.BlockSpec((1, N), lambda i, j: (0, 0))],
        out_specs=pl.BlockSpec((tm, N), lambda i, j: (i, 0)),
        out_shape=jax.ShapeDtypeStruct((M, N), F32),
        compiler_params=_cparams(("parallel", "arbitrary")),
        name="matmul_norm_residual",
    )(a, w, x, gain.reshape(1, N))


def _merge_kernel(oa_ref, ob_ref, wa_ref, wb_ref, ga_ref, gb_ref, o_ref):
    ya = _dot(oa_ref[...], wa_ref[...])
    yb = _dot(ob_ref[...], wb_ref[...])
    o_ref[...] = (_sigmoid(ga_ref[...]) * ya + _sigmoid(gb_ref[...]) * yb).astype(BF16)


def _merge(oa, ob, wa, wb, proj, ga_off, gb_off, *, tm=512, tn=512):
    M, KA = oa.shape
    KB = ob.shape[1]
    N = wa.shape[1]
    tm = _pick_tile(M, tm, SUBLANE)
    tn = _pick_tile(N, tn, LANE)
    assert ga_off % tn == 0 and gb_off % tn == 0
    ja, jb = ga_off // tn, gb_off // tn
    return pl.pallas_call(
        _merge_kernel,
        grid=(M // tm, N // tn),
        in_specs=[pl.BlockSpec((tm, KA), lambda i, j: (i, 0)),
                  pl.BlockSpec((tm, KB), lambda i, j: (i, 0)),
                  pl.BlockSpec((KA, tn), lambda i, j: (0, j)),
                  pl.BlockSpec((KB, tn), lambda i, j: (0, j)),
                  pl.BlockSpec((tm, tn), lambda i, j: (i, ja + j)),
                  pl.BlockSpec((tm, tn), lambda i, j: (i, jb + j))],
        out_specs=pl.BlockSpec((tm, tn), lambda i, j: (i, j)),
        out_shape=jax.ShapeDtypeStruct((M, N), BF16),
        compiler_params=_cparams(("parallel", "arbitrary")),
        name="merge",
    )(oa, ob, wa, wb, proj, proj)


def _gla_kernel(q_ref, k_ref, v_ref, za_ref, zg_ref, wa_ref, ba_ref, gn_ref, s0_ref,
                o_ref, sn_ref, s_ref, *, scale, c):
    l = pl.program_id(2)

    @pl.when(l == 0)
    def _():
        s_ref[...] = s0_ref[0, 0]

    dv = v_ref.shape[2]
    q = q_ref[0] * scale
    k = k_ref[0]
    v = v_ref[0].astype(BF16)
    x = _dot(za_ref[0].astype(BF16), wa_ref[...]) + ba_ref[...]
    la = _log_sigmoid(x) * (1.0 / GLA_TAU)
    tri = _tri_incl(c)
    b = _ones_dot(jnp.where(tri, 1.0, 0.0).astype(BF16), la)
    b_end = b[c - 1:c, :]
    qd = (q * jnp.exp(b)).astype(BF16)
    kd = (k * jnp.exp(-b)).astype(BF16)
    att = jnp.where(tri, _dot_nt(qd, kd), 0.0).astype(BF16)
    s = s_ref[...]
    o = _dot(att, v) + _dot(qd, s.astype(BF16))
    k_end = (k * jnp.exp(b_end - b)).astype(BF16)
    la_hi, la_lo = _split2(la)
    ones = jnp.ones((c, LANE), BF16)
    dec = jnp.exp(_dot_tn(la_hi, ones) + _dot_tn(la_lo, ones))
    dec = jnp.concatenate([dec] * (dv // LANE), axis=1)
    s_new = s * dec + _dot_tn(k_end, v)
    s_ref[...] = s_new

    ms = jnp.mean(o * o, axis=-1, keepdims=True)
    on = o * lax.rsqrt(ms + EPS) * gn_ref[...]
    zg = zg_ref[0]
    o_ref[0] = (on * (zg * _sigmoid(zg))).astype(BF16)

    @pl.when(l == pl.num_programs(2) - 1)
    def _():
        sn_ref[0, 0] = s_new


def _gla(proj3, offs, w_alpha2p, b_alpha, gla_norm, s0, *, dk, dv, heads):
    B, L, _ = proj3.shape
    c = min(GLA_CHUNK, L)
    assert L % c == 0 and dv % LANE == 0
    for name, wdt in (("q", dk), ("k", dk), ("v", dv), ("zg", dv), ("za", LANE)):
        assert offs[name] % wdt == 0
    jq, jk, jv, jg, jz = (offs["q"] // dk, offs["k"] // dk, offs["v"] // dv, offs["zg"] // dv,
                          offs["za"] // LANE)
    return pl.pallas_call(
        functools.partial(_gla_kernel, scale=dk ** -0.5, c=c),
        grid=(B, heads, L // c),
        in_specs=[pl.BlockSpec((1, c, dk), lambda b, h, l: (b, l, jq + h)),
                  pl.BlockSpec((1, c, dk), lambda b, h, l: (b, l, jk + h)),
                  pl.BlockSpec((1, c, dv), lambda b, h, l: (b, l, jv + h)),
                  pl.BlockSpec((1, c, LANE), lambda b, h, l: (b, l, jz)),
                  pl.BlockSpec((1, c, dv), lambda b, h, l: (b, l, jg + h)),
                  pl.BlockSpec((LANE, dk), lambda b, h, l: (0, h)),
                  pl.BlockSpec((1, dk), lambda b, h, l: (0, h)),
                  pl.BlockSpec((1, dv), lambda b, h, l: (0, 0)),
                  pl.BlockSpec((1, 1, dk, dv), lambda b, h, l: (b, h, 0, 0))],
        out_specs=[pl.BlockSpec((1, c, dv), lambda b, h, l: (b, l, h)),
                   pl.BlockSpec((1, 1, dk, dv), lambda b, h, l: (b, h, 0, 0))],
        out_shape=[jax.ShapeDtypeStruct((B, L, heads * dv), BF16),
                   jax.ShapeDtypeStruct((B, heads, dk, dv), F32)],
        scratch_shapes=[pltpu.VMEM((dk, dv), F32)],
        compiler_params=_cparams(("parallel", "parallel", "arbitrary")),
        name="gla",
    )(proj3, proj3, proj3, proj3, proj3, w_alpha2p, b_alpha.reshape(1, -1), gla_norm.reshape(1, dv), s0)


def _shifted(cur, first_row):
    rolled = pltpu.roll(cur, 1, 0)
    row = lax.broadcasted_iota(jnp.int32, cur.shape, 0)
    return jnp.where(row == 0, first_row, rolled)


def _rwkv_prep_kernel(r_ref, xw_ref, kr_ref, vr_ref, xa_ref, xg_ref,
                      sr_ref, sxw_ref, skr_ref, svr_ref, sxa_ref, sxg_ref,
                      mr_ref, mxw_ref, mkr_ref, mvr_ref, mxa_ref, mxg_ref,
                      w0_ref, wd_ref, a0_ref, wi_ref, wg_ref,
                      ro_ref, lw_ref, ko_ref, vo_ref, ao_ref, go_ref,
                      cr_ref, cxw_ref, ckr_ref, cvr_ref, cxa_ref, cxg_ref):
    l = pl.program_id(1)
    tl = r_ref.shape[1]

    def shift_mix(z_ref, s_ref, m_ref, c_ref):
        cur = z_ref[0]
        first = jnp.where(l == 0, s_ref[0], c_ref[...])
        prev = _shifted(cur, first)
        c_ref[...] = cur[tl - 1:tl, :]
        return cur + (prev - cur) * m_ref[...]

    ro_ref[0] = shift_mix(r_ref, sr_ref, mr_ref, cr_ref)
    ko_ref[0] = shift_mix(kr_ref, skr_ref, mkr_ref, ckr_ref)
    vo_ref[0] = shift_mix(vr_ref, svr_ref, mvr_ref, cvr_ref)
    xw = shift_mix(xw_ref, sxw_ref, mxw_ref, cxw_ref)
    xa = shift_mix(xa_ref, sxa_ref, mxa_ref, cxa_ref)
    xg = shift_mix(xg_ref, sxg_ref, mxg_ref, cxg_ref)
    z = w0_ref[...] + _dot(jnp.tanh(xw).astype(BF16), wd_ref[...])
    lw_ref[0] = -DECAY_SCALE * _sigmoid(z)
    ao_ref[0] = _sigmoid(a0_ref[...] + _dot(xa.astype(BF16), wi_ref[...]))
    go_ref[0] = _dot(_sigmoid(xg).astype(BF16), wg_ref[...])


def _rwkv_prep(proj3, offs, shift_parts, mu_parts, w0, w_decay2p, a0, w_iclr2p, w_gate2, *, rw, gl):
    B, L, _ = proj3.shape
    tl = _pick_tile(L, 256, SUBLANE)
    names = ("r", "xw", "kr", "vr", "xa", "xg")
    widths = {"r": rw, "xw": LANE, "kr": rw, "vr": rw, "xa": LANE, "xg": gl}
    in_specs, args = [], []
    for n in names:
        w = widths[n]
        assert offs[n] % w == 0
        in_specs.append(pl.BlockSpec((1, tl, w), functools.partial(lambda b, l, j: (b, l, j), j=offs[n] // w)))
        args.append(proj3)
    for n in names:
        in_specs.append(pl.BlockSpec((1, 1, widths[n]), lambda b, l: (b, 0, 0)))
        args.append(shift_parts[n])
    for n in names:
        in_specs.append(pl.BlockSpec((1, widths[n]), lambda b, l: (0, 0)))
        args.append(mu_parts[n])
    for arr in (w0.reshape(1, rw), w_decay2p, a0.reshape(1, rw), w_iclr2p, w_gate2):
        in_specs.append(pl.BlockSpec(arr.shape, lambda b, l: (0, 0)))
        args.append(arr)
    out_spec = pl.BlockSpec((1, tl, rw), lambda b, l: (b, l, 0))
    out_sd = jax.ShapeDtypeStruct((B, L, rw), F32)
    return pl.pallas_call(
        _rwkv_prep_kernel,
        grid=(B, L // tl),
        in_specs=in_specs,
        out_specs=[out_spec] * 6,
        out_shape=[out_sd] * 6,
        scratch_shapes=[pltpu.VMEM((1, widths[n]), F32) for n in names],
        compiler_params=_cparams(("parallel", "arbitrary")),
        name="rwkv_prep",
    )(*args)


def _rwkv_chunk_kernel(r_ref, lw_ref, kr_ref, v_ref, a_ref, g_ref,
                       kk_w_ref, ka_w_ref, rk_w_ref, lnw_ref, lnb_ref, s0_ref,
                       o_ref, sn_ref, w_ref, *, c, hd):
    l = pl.program_id(2)
    tl = r_ref.shape[1]
    n2 = 2 * c
    lane_c = lax.broadcasted_iota(jnp.int32, (c, LANE), 1)
    head0_c = lane_c < hd
    row2 = lax.broadcasted_iota(jnp.int32, (n2, n2), 0)
    col2 = lax.broadcasted_iota(jnp.int32, (n2, n2), 1)
    same = (2 * row2 + 1 - n2) * (2 * col2 + 1 - n2) > 0
    strict = jnp.logical_and(same, row2 > col2)
    incl = jnp.logical_and(same, row2 >= col2)
    eye = jnp.where(row2 == col2, 1.0, 0.0)
    tri = jnp.where(_tri_incl(c), 1.0, 0.0).astype(BF16)
    lr = lax.broadcasted_iota(jnp.int32, (LANE, LANE), 0)
    lc = lax.broadcasted_iota(jnp.int32, (LANE, LANE), 1)
    seg = jnp.where((2 * lr + 1 - LANE) * (2 * lc + 1 - LANE) > 0, 1.0, 0.0).astype(BF16)

    @pl.when(l == 0)
    def _():
        s0 = s0_ref[0, 0]
        lane_s = lax.broadcasted_iota(jnp.int32, (hd, LANE), 1)
        w_ref[...] = jnp.concatenate([jnp.where(lane_s < hd, s0, 0.0),
                                      jnp.where(lane_s < hd, 0.0, s0)], axis=0)

    def stack(x):
        return jnp.concatenate([jnp.where(head0_c, x, 0.0), jnp.where(head0_c, 0.0, x)], axis=0)

    def unstack(x):
        return x[0:c] + x[c:n2]

    def chunk(ci, _):
        sl = pl.ds(pl.multiple_of(ci * c, c), c)
        r = r_ref[0, sl, :]
        lw = lw_ref[0, sl, :]
        kr = kr_ref[0, sl, :]
        v = v_ref[0, sl, :]
        a = a_ref[0, sl, :]
        kk = kr * kk_w_ref[...]
        kk = kk / jnp.maximum(jnp.sqrt(_dot_ones(kk * kk, seg)), KK_EPS)
        k = kr * (1.0 + (a - 1.0) * ka_w_ref[...])
        cum = _ones_dot(tri, lw)
        cum_end = cum[c - 1:c, :]
        e_neg = jnp.exp(-cum)
        e_rem = jnp.exp(cum_end - cum)
        al = kk * a
        al_t = stack(al * e_neg).astype(BF16)
        k_t = stack(k * e_neg).astype(BF16)
        be_t = stack(-kk * jnp.exp(cum - lw)).astype(BF16)
        r_t = stack(r * jnp.exp(cum)).astype(BF16)
        v_s = stack(v).astype(BF16)
        lhs = jnp.concatenate([be_t, r_t], axis=0)
        sc_a = _dot_nt(lhs, al_t)
        sc_k = _dot_nt(lhs, k_t)
        l_a = jnp.where(strict, sc_a[0:n2], 0.0)
        l_k = jnp.where(strict, sc_k[0:n2], 0.0).astype(BF16)
        m_a = jnp.where(incl, sc_a[n2:2 * n2], 0.0).astype(BF16)
        m_k = jnp.where(incl, sc_k[n2:2 * n2], 0.0).astype(BF16)
        t_inv = eye + l_a
        lp = l_a
        span = 1
        while 2 * span < c:
            lpb = lp.astype(BF16)
            lp = _dot(lpb, lpb)
            t_inv = t_inv + _dot(t_inv.astype(BF16), lp.astype(BF16))
            span *= 2
        w = w_ref[...]
        rd = _dot_nt(lhs, w.astype(BF16))
        rhs_u = rd[0:n2] + _dot(l_k, v_s)
        u_s = _dot(t_inv.astype(BF16), rhs_u.astype(BF16))
        u_b = u_s.astype(BF16)
        o_s = rd[n2:2 * n2] + _dot(m_a, u_b) + _dot(m_k, v_s)
        o = unstack(o_s)
        uv = jnp.concatenate([u_b, v_s], axis=0)
        ak = jnp.concatenate([stack(al * e_rem), stack(k * e_rem)], axis=0).astype(BF16)
        w_ref[...] = w * jnp.exp(cum_end) + _dot_tn(uv, ak)

        mu = _dot_ones(o, seg) * (1.0 / hd)
        d = o - mu
        var = _dot_ones(d * d, seg) * (1.0 / hd)
        on = d * lax.rsqrt(var + GN_EPS) * lnw_ref[...] + lnb_ref[...]
        bonus = _dot_ones(r * k * rk_w_ref[...], seg) * v
        o_ref[0, sl, :] = ((on + bonus) * g_ref[0, sl, :]).astype(BF16)
        return 0

    lax.fori_loop(0, tl // c, chunk, 0)

    @pl.when(l == pl.num_programs(2) - 1)
    def _():
        wf = w_ref[...]
        sn_ref[0, 0] = wf[0:hd] + wf[hd:2 * hd]


def _rwkv_chunk(r, lw, kr, vr, a, g, k_k, k_a, r_k, ln_w, ln_b, s0p, *, hd):
    B, L, RW = r.shape
    assert 2 * hd == LANE and RW % LANE == 0
    c = min(RWKV_CHUNK, L)
    tl = _pick_tile(L, 512, c)
    assert L % c == 0 and tl % c == 0 and c % SUBLANE == 0
    npair = RW // LANE
    seq = pl.BlockSpec((1, tl, LANE), lambda b, p, l: (b, l, p))
    par = pl.BlockSpec((1, LANE), lambda b, p, l: (0, p))
    st = pl.BlockSpec((1, 1, hd, LANE), lambda b, p, l: (b, p, 0, 0))
    return pl.pallas_call(
        functools.partial(_rwkv_chunk_kernel, c=c, hd=hd),
        grid=(B, npair, L // tl),
        in_specs=[seq] * 6 + [par] * 5 + [st],
        out_specs=[seq, st],
        out_shape=[jax.ShapeDtypeStruct((B, L, RW), BF16),
                   jax.ShapeDtypeStruct((B, npair, hd, LANE), F32)],
        scratch_shapes=[pltpu.VMEM((LANE, LANE), F32)],
        compiler_params=_cparams(("parallel", "parallel", "arbitrary")),
        name="rwkv_chunk",
    )(r, lw, kr, vr, a, g, k_k.reshape(1, RW), k_a.reshape(1, RW), r_k.reshape(1, RW),
      ln_w.reshape(1, RW), ln_b.reshape(1, RW), s0p)


def _conv_act_kernel(g_ref, v_ref, s_ref, cw_ref, cb_ref, o_ref, c_ref):
    l = pl.program_id(2)
    tl = g_ref.shape[1]
    g = g_ref[0]
    prev = jnp.where(l == 0, s_ref[0], c_ref[...])
    row = lax.broadcasted_iota(jnp.int32, g.shape, 0)
    g1 = jnp.where(row == 0, prev[1:2], pltpu.roll(g, 1, 0))
    g2 = jnp.where(row == 0, prev[0:1], jnp.where(row == 1, prev[1:2], pltpu.roll(g, 2, 0)))
    c_ref[...] = g[tl - 2:tl, :]
    conv = cb_ref[...] + g2 * cw_ref[0:1, :] + g1 * cw_ref[1:2, :] + g * cw_ref[2:3, :]
    o_ref[0] = (_gelu_tanh(conv) * v_ref[0]).astype(BF16)


def _conv_act(up3, s_conv, conv_w, conv_b, *, dff):
    B, L, _ = up3.shape
    assert conv_w.shape[0] == 3 and s_conv.shape[1] == 2 and L >= 2
    tl = _pick_tile(L, 2048, SUBLANE)
    tn = _pick_tile(dff, max(LANE, (1 << 20) // tl), LANE)
    nj = dff // tn
    return pl.pallas_call(
        _conv_act_kernel,
        grid=(B, nj, L // tl),
        in_specs=[pl.BlockSpec((1, tl, tn), lambda b, j, l: (b, l, j)),
                  pl.BlockSpec((1, tl, tn), lambda b, j, l: (b, l, nj + j)),
                  pl.BlockSpec((1, 2, tn), lambda b, j, l: (b, 0, j)),
                  pl.BlockSpec((3, tn), lambda b, j, l: (0, j)),
                  pl.BlockSpec((1, tn), lambda b, j, l: (0, j))],
        out_specs=pl.BlockSpec((1, tl, tn), lambda b, j, l: (b, l, j)),
        out_shape=jax.ShapeDtypeStruct((B, L, dff), BF16),
        scratch_shapes=[pltpu.VMEM((2, tn), F32)],
        compiler_params=_cparams(("parallel", "parallel", "arbitrary")),
        name="conv_act",
    )(up3, up3, s_conv, conv_w, conv_b.reshape(1, dff))


def _pad_cols(a, width):
    return a if a.shape[-1] == width else jnp.pad(a, [(0, 0)] * (a.ndim - 1) + [(0, width - a.shape[-1])])


def _pad_rows(a, height):
    return a if a.shape[0] == height else jnp.pad(a, [(0, height - a.shape[0])] + [(0, 0)] * (a.ndim - 1))


def _layout(D, QK, GW, R, RW, WL, AL, GL):
    src, o = {}, 0
    for name, w in (("q", QK), ("k", QK), ("v", GW), ("za", R), ("zg", GW), ("r", RW), ("xw", WL),
                    ("kr", RW), ("vr", RW), ("xa", AL), ("xg", GL), ("ga", D), ("gb", D)):
        src[name] = (o, w)
        o += w
    pad = lambda w: -(-w // LANE) * LANE
    order = sorted(src, key=lambda n: -pad(src[n][1]))
    dst, o = {}, 0
    for name in order:
        dst[name] = o
        o += pad(src[name][1])
    return src, dst, order, o


def _prepare_params(lp):
    (w_in, w_alpha2, b_alpha, gla_norm, w_branch_a,
     mu_shift, w0, w_decay2, a0, w_iclr2, w_gate2, k_k, k_a, r_k, ln_x_w, ln_x_b, w_branch_b,
     w_out, g_pre_mix, g_post_mix, g_pre_ffn, g_post_ffn,
     w_up, conv_w, conv_b, w_down, g_pe, w_pe_gate, w_pe) = lp
    D = w_in.shape[0]
    R, QK = w_alpha2.shape
    GW = w_branch_a.shape[0]
    RW = w_branch_b.shape[0]
    WL, AL, GL = w_decay2.shape[0], w_iclr2.shape[0], w_gate2.shape[0]
    src, dst, order, total = _layout(D, QK, GW, R, RW, WL, AL, GL)
    n_pad = -(-total // 512) * 512
    pad = lambda w: -(-w // LANE) * LANE
    cols = [_pad_cols(w_in[:, src[n][0]:src[n][0] + src[n][1]], pad(src[n][1])) for n in order]
    if n_pad > total:
        cols.append(jnp.zeros((D, n_pad - total), w_in.dtype))
    prm = dict(
        dims=dict(D=D, R=R, QK=QK, GW=GW, RW=RW, WL=WL, AL=AL, GL=GL, DV=gla_norm.shape[0],
                  HD=r_k.shape[1], DFF=conv_b.shape[0]),
        src=src, dst=dst,
        w_in=jnp.concatenate(cols, axis=1).astype(BF16),
        w_alpha2=_pad_rows(w_alpha2, LANE).astype(BF16), b_alpha=b_alpha, gla_norm=gla_norm,
        w_branch_a=w_branch_a.astype(BF16), w_branch_b=w_branch_b.astype(BF16),
        w0=w0, w_decay2=_pad_rows(w_decay2, LANE).astype(BF16), a0=a0,
        w_iclr2=_pad_rows(w_iclr2, LANE).astype(BF16), w_gate2=w_gate2.astype(BF16),
        k_k=k_k, k_a=k_a, r_k=r_k.reshape(-1), ln_x_w=ln_x_w, ln_x_b=ln_x_b,
        w_out=w_out.astype(BF16), g_pre_mix=g_pre_mix, g_post_mix=g_post_mix,
        g_pre_ffn=g_pre_ffn, g_post_ffn=g_post_ffn,
        w_up=w_up.astype(BF16), conv_w=conv_w, conv_b=conv_b, w_down=w_down.astype(BF16),
        g_pe=g_pe, w_pe_gate=w_pe_gate.astype(BF16), w_pe=w_pe.astype(BF16),
    )
    rsrc = src["r"][0]
    rnames = ("r", "xw", "kr", "vr", "xa", "xg")
    prm["rnames"] = rnames
    prm["rsl"] = {n: (src[n][0] - rsrc, src[n][1]) for n in rnames}
    prm["mu"] = {n: _pad_cols(mu_shift[None, prm["rsl"][n][0]:prm["rsl"][n][0] + prm["rsl"][n][1]],
                              pad(prm["rsl"][n][1])) for n in rnames}
    return prm


def _pack_state(s, hd):
    B, H = s.shape[:2]
    return s.reshape(B, H // 2, 2, hd, hd).transpose(0, 1, 3, 2, 4).reshape(B, H // 2, hd, 2 * hd)


def _unpack_state(sp, hd):
    B, NP = sp.shape[:2]
    return sp.reshape(B, NP, hd, 2, hd).transpose(0, 1, 3, 2, 4).reshape(B, 2 * NP, hd, hd)


def _layer(x, p, s_gla, s_rwkv, s_shift, s_conv, prm):
    B, L, D = x.shape
    M = B * L
    dm, dst = prm["dims"], prm["dst"]
    GW, RW, DV, HD, DFF, QK = dm["GW"], dm["RW"], dm["DV"], dm["HD"], dm["DFF"], dm["QK"]
    heads = GW // DV
    pad = lambda w: -(-w // LANE) * LANE
    x2 = x.reshape(M, D)

    proj = _norm_matmul(x2, prm["g_pre_mix"], prm["w_in"])
    proj3 = proj.reshape(B, L, -1)

    o_a, s_gla_new = _gla(proj3, dst, prm["w_alpha2"], prm["b_alpha"], prm["gla_norm"], s_gla,
                          dk=QK // heads, dv=DV, heads=heads)

    shift_parts = {n: _pad_cols(s_shift[:, None, prm["rsl"][n][0]:prm["rsl"][n][0] + prm["rsl"][n][1]],
                                pad(prm["rsl"][n][1])) for n in prm["rnames"]}
    r, lw, kr, vr, a, g = _rwkv_prep(proj3, dst, shift_parts, prm["mu"], prm["w0"], prm["w_decay2"],
                                     prm["a0"], prm["w_iclr2"], prm["w_gate2"], rw=RW, gl=dm["GL"])
    o_b, s_rwkv_new = _rwkv_chunk(r, lw, kr, vr, a, g, prm["k_k"], prm["k_a"], prm["r_k"],
                                  prm["ln_x_w"], prm["ln_x_b"], _pack_state(s_rwkv, HD), hd=HD)
    new_shift = jnp.concatenate(
        [proj3[:, L - 1, dst[n]:dst[n] + prm["rsl"][n][1]] for n in prm["rnames"]], axis=-1)

    mixed = _merge(o_a.reshape(M, GW), o_b.reshape(M, RW), prm["w_branch_a"], prm["w_branch_b"],
                   proj, dst["ga"], dst["gb"])
    x2 = _matmul_norm_residual(mixed, prm["w_out"], x2, prm["g_post_mix"], tm=256, tn=512)

    up = _norm_matmul(x2, prm["g_pre_ffn"], prm["w_up"])
    up3 = up.reshape(B, L, 2 * DFF)
    act = _conv_act(up3, s_conv, prm["conv_w"], prm["conv_b"], dff=DFF)
    new_conv = up3[:, L - 2:, :DFF]
    x2 = _matmul_norm_residual(act.reshape(M, DFF), prm["w_down"], x2, prm["g_post_ffn"], tm=256, tn=256)

    x2 = _pe_layer(x2, prm["g_pe"], prm["w_pe_gate"], p.reshape(M, -1), prm["w_pe"])
    return x2.reshape(B, L, D), s_gla_new, _unpack_state(s_rwkv_new, HD), new_shift, new_conv


def kernel(x_prompt, x_sample, state_gla, state_rwkv, state_shift, state_ffn_conv, p_prompt, p_sample, w_in, w_alpha2, b_alpha, gla_norm, w_branch_a, mu_shift, w0, w_decay2, a0, w_iclr2, w_gate2, k_k, k_a, r_k, ln_x_w, ln_x_b, w_branch_b, w_out, g_pre_mix, g_post_mix, g_pre_ffn, g_post_ffn, w_up, conv_w, conv_b, w_down, g_pe, w_pe_gate, w_pe):
    params = (w_in, w_alpha2, b_alpha, gla_norm, w_branch_a,
              mu_shift, w0, w_decay2, a0, w_iclr2, w_gate2, k_k, k_a, r_k, ln_x_w, ln_x_b, w_branch_b,
              w_out, g_pre_mix, g_post_mix, g_pre_ffn, g_post_ffn,
              w_up, conv_w, conv_b, w_down, g_pe, w_pe_gate, w_pe)
    depth = w_in.shape[0]
    nb = x_prompt.shape[0]
    yp, ys = x_prompt, x_sample
    outs_p = [[], [], [], []]
    outs_s = [[], [], [], []]
    for i in range(depth):
        prm = _prepare_params(tuple(t[i] for t in params))
        z_gla = jnp.zeros((nb,) + state_gla.shape[2:], F32)
        z_rwkv = jnp.zeros((nb,) + state_rwkv.shape[2:], F32)
        z_shift = jnp.zeros((nb,) + state_shift.shape[2:], x_prompt.dtype)
        z_conv = jnp.zeros((nb,) + state_ffn_conv.shape[2:], x_prompt.dtype)
        yp, *st = _layer(yp, p_prompt[i], z_gla, z_rwkv, z_shift, z_conv, prm)
        for acc, s in zip(outs_p, st):
            acc.append(s)
        ys, *st = _layer(ys, p_sample[i], state_gla[i], state_rwkv[i], state_shift[i],
                         state_ffn_conv[i], prm)
        for acc, s in zip(outs_s, st):
            acc.append(s)
    return (yp, ys, *(jnp.stack(a) for a in outs_p), *(jnp.stack(a) for a in outs_s))
```

```python
import functools
import math

import jax
import jax.numpy as jnp
from jax import lax
from jax.experimental import pallas as pl
from jax.experimental.pallas import tpu as pltpu

F32 = jnp.float32
BF16 = jnp.bfloat16

LANE = 128
SUBLANE = 8
VMEM_LIMIT_BYTES = 56 * 1024 * 1024

EPS = 1e-6
GN_EPS = 64e-5
GLA_TAU = 16.0
GLA_CHUNK = 64
RWKV_CHUNK = 64
KK_EPS = 1e-12
DECAY_SCALE = math.exp(-0.5)
GELU_C = math.sqrt(2.0 / math.pi)


def _cparams(sem):
    return pltpu.CompilerParams(dimension_semantics=sem, vmem_limit_bytes=VMEM_LIMIT_BYTES)


def _sigmoid(x):
    return 1.0 / (1.0 + jnp.exp(-x))


def _log_sigmoid(x):
    return jnp.minimum(x, 0.0) - jnp.log(1.0 + jnp.exp(-jnp.abs(x)))


def _gelu_tanh(x):
    return 0.5 * x * (1.0 + jnp.tanh(GELU_C * (x + 0.044715 * (x * x * x))))


def _split2(x):
    hi = x.astype(BF16)
    lo = (x - hi.astype(F32)).astype(BF16)
    return hi, lo


def _dot(a, b):
    return jnp.dot(a, b, preferred_element_type=F32)


def _dot_nt(a, b):
    return lax.dot_general(a, b, (((1,), (1,)), ((), ())), preferred_element_type=F32)


def _dot_tn(a, b):
    return lax.dot_general(a, b, (((0,), (0,)), ((), ())), preferred_element_type=F32)


def _ones_dot(m01, x):
    hi, lo = _split2(x)
    return _dot(m01, hi) + _dot(m01, lo)


def _dot_ones(x, m01):
    hi, lo = _split2(x)
    return _dot(hi, m01) + _dot(lo, m01)


def _lockstep(gens):
    gens = list(gens)
    while gens:
        alive = []
        for g in gens:
            try:
                next(g)
                alive.append(g)
            except StopIteration:
                pass
        gens = alive


def _tri_incl(n):
    r = lax.broadcasted_iota(jnp.int32, (n, n), 0)
    c = lax.broadcasted_iota(jnp.int32, (n, n), 1)
    return r >= c


def _pick_tile(n, target, mult):
    if n <= target:
        return n
    best = None
    t = mult
    while t <= target:
        if n % t == 0:
            best = t
        t += mult
    assert best is not None, (n, target, mult)
    return best


def _rms_rows(x_ref, g_ref, h_ref, rows):
    tm = x_ref.shape[0]

    def body(i, _):
        sl = pl.ds(pl.multiple_of(i * rows, rows), rows)
        x = x_ref[sl, :]
        ms = jnp.mean(x * x, axis=-1, keepdims=True)
        h_ref[sl, :] = (x * lax.rsqrt(ms + EPS) * g_ref[...]).astype(BF16)
        return 0

    lax.fori_loop(0, tm // rows, body, 0)


def _norm_mm_kernel(x_ref, g_ref, w_ref, o_ref, h_ref, *, rows):
    @pl.when(pl.program_id(1) == 0)
    def _():
        _rms_rows(x_ref, g_ref, h_ref, rows)

    o_ref[...] = _dot(h_ref[...], w_ref[...])


def _norm_matmul(x, gain, w, *, tm=512, tn=512):
    M, K = x.shape
    N = w.shape[1]
    tm = _pick_tile(M, tm, SUBLANE)
    tn = _pick_tile(N, tn, LANE)
    rows = _pick_tile(tm, 64, SUBLANE)
    return pl.pallas_call(
        functools.partial(_norm_mm_kernel, rows=rows),
        grid=(M // tm, N // tn),
        in_specs=[pl.BlockSpec((tm, K), lambda i, j: (i, 0)),
                  pl.BlockSpec((1, K), lambda i, j: (0, 0)),
                  pl.BlockSpec((K, tn), lambda i, j: (0, j))],
        out_specs=pl.BlockSpec((tm, tn), lambda i, j: (i, j)),
        out_shape=jax.ShapeDtypeStruct((M, N), F32),
        scratch_shapes=[pltpu.VMEM((tm, K), BF16)],
        compiler_params=_cparams(("parallel", "arbitrary")),
        name="norm_matmul",
    )(x, gain.reshape(1, K), w)


def _pe_kernel(x_ref, g_ref, w_ref, p_ref, wp_ref, o_ref, h_ref, *, rows, tn):
    j = pl.program_id(1)

    @pl.when(j == 0)
    def _():
        _rms_rows(x_ref, g_ref, h_ref, rows)

    gate = _sigmoid(_dot(h_ref[...], w_ref[...]))
    pe = _dot(p_ref[...].astype(BF16), wp_ref[...])
    xs = x_ref[:, pl.ds(pl.multiple_of(j * tn, tn), tn)]
    o_ref[...] = xs + gate * pe


def _pe_layer(x, gain, w_gate, p, w_pe, *, tm=512, tn=512):
    M, K = x.shape
    N = w_gate.shape[1]
    P = p.shape[1]
    tm = _pick_tile(M, tm, SUBLANE)
    tn = _pick_tile(N, tn, LANE)
    rows = _pick_tile(tm, 64, SUBLANE)
    return pl.pallas_call(
        functools.partial(_pe_kernel, rows=rows, tn=tn),
        grid=(M // tm, N // tn),
        in_specs=[pl.BlockSpec((tm, K), lambda i, j: (i, 0)),
                  pl.BlockSpec((1, K), lambda i, j: (0, 0)),
                  pl.BlockSpec((K, tn), lambda i, j: (0, j)),
                  pl.BlockSpec((tm, P), lambda i, j: (i, 0)),
                  pl.BlockSpec((P, tn), lambda i, j: (0, j))],
        out_specs=pl.BlockSpec((tm, tn), lambda i, j: (i, j)),
        out_shape=jax.ShapeDtypeStruct((M, N), F32),
        scratch_shapes=[pltpu.VMEM((tm, K), BF16)],
        compiler_params=_cparams(("parallel", "arbitrary")),
        name="pe_layer",
    )(x, gain.reshape(1, K), w_gate, p, w_pe)


def _mm_norm_res_kernel(a_ref, w_ref, x_ref, g_ref, o_ref, *, tn, rows):
    j = pl.program_id(1)
    o_ref[:, pl.ds(pl.multiple_of(j * tn, tn), tn)] = _dot(a_ref[...], w_ref[...])

    @pl.when(j == pl.num_programs(1) - 1)
    def _():
        tm = o_ref.shape[0]

        def body(i, _):
            sl = pl.ds(pl.multiple_of(i * rows, rows), rows)
            y = o_ref[sl, :]
            ms = jnp.mean(y * y, axis=-1, keepdims=True)
            o_ref[sl, :] = x_ref[sl, :] + y * lax.rsqrt(ms + EPS) * g_ref[...]
            return 0

        lax.fori_loop(0, tm // rows, body, 0)


def _matmul_norm_residual(a, w, x, gain, *, tm, tn):
    M, K = a.shape
    N = w.shape[1]
    tm = _pick_tile(M, tm, SUBLANE)
    tn = _pick_tile(N, tn, LANE)
    rows = _pick_tile(tm, 64, SUBLANE)
    return pl.pallas_call(
        functools.partial(_mm_norm_res_kernel, tn=tn, rows=rows),
        grid=(M // tm, N // tn),
        in_specs=[pl.BlockSpec((tm, K), lambda i, j: (i, 0)),
                  pl.BlockSpec((K, tn), lambda i, j: (0, j)),
                  pl.BlockSpec((tm, N), lambda i, j: (i, 0)),
                  pl.BlockSpec((1, N), lambda i, j: (0, 0))],
        out_specs=pl.BlockSpec((tm, N), lambda i, j: (i, 0)),
        out_shape=jax.ShapeDtypeStruct((M, N), F32),
        compiler_params=_cparams(("parallel", "arbitrary")),
        name="matmul_norm_residual",
    )(a, w, x, gain.reshape(1, N))


def _merge_kernel(oa_ref, ob_ref, wa_ref, wb_ref, ga_ref, gb_ref, o_ref):
    ya = _dot(oa_ref[...], wa_ref[...])
    yb = _dot(ob_ref[...], wb_ref[...])
    o_ref[...] = (_sigmoid(ga_ref[...]) * ya + _sigmoid(gb_ref[...]) * yb).astype(BF16)


def _merge(oa, ob, wa, wb, proj, ga_off, gb_off, *, tm=512, tn=512):
    M, KA = oa.shape
    KB = ob.shape[1]
    N = wa.shape[1]
    tm = _pick_tile(M, tm, SUBLANE)
    tn = _pick_tile(N, tn, LANE)
    assert ga_off % tn == 0 and gb_off % tn == 0
    ja, jb = ga_off // tn, gb_off // tn
    return pl.pallas_call(
        _merge_kernel,
        grid=(M // tm, N // tn),
        in_specs=[pl.BlockSpec((tm, KA), lambda i, j: (i, 0)),
                  pl.BlockSpec((tm, KB), lambda i, j: (i, 0)),
                  pl.BlockSpec((KA, tn), lambda i, j: (0, j)),
                  pl.BlockSpec((KB, tn), lambda i, j: (0, j)),
                  pl.BlockSpec((tm, tn), lambda i, j: (i, ja + j)),
                  pl.BlockSpec((tm, tn), lambda i, j: (i, jb + j))],
        out_specs=pl.BlockSpec((tm, tn), lambda i, j: (i, j)),
        out_shape=jax.ShapeDtypeStruct((M, N), BF16),
        compiler_params=_cparams(("parallel", "arbitrary")),
        name="merge",
    )(oa, ob, wa, wb, proj, proj)


def _gla_kernel(q_ref, k_ref, v_ref, za_ref, zg_ref, wa_ref, ba_ref, gn_ref, s0_ref,
                o_ref, sn_ref, s_ref, *, scale, c, heads):
    l = pl.program_id(1)

    @pl.when(l == 0)
    def _():
        s_ref[...] = s0_ref[0]

    dk = q_ref.shape[2] // heads
    dv = v_ref.shape[2] // heads
    x = _dot(za_ref[0].astype(BF16), wa_ref[...]) + ba_ref[...]
    la_all = _log_sigmoid(x) * (1.0 / GLA_TAU)
    tri = _tri_incl(c)
    b_all = _ones_dot(jnp.where(tri, 1.0, 0.0).astype(BF16), la_all)
    ones = jnp.ones((c, LANE), BF16)
    def head(h):
        ks = slice(h * dk, (h + 1) * dk)
        vs = slice(h * dv, (h + 1) * dv)
        la = la_all[:, ks]
        b = b_all[:, ks]
        b_end = b[c - 1:c, :]
        q = q_ref[0, :, ks] * scale
        k = k_ref[0, :, ks]
        v = v_ref[0, :, vs].astype(BF16)
        qd = (q * jnp.exp(b)).astype(BF16)
        kd = (k * jnp.exp(-b)).astype(BF16)
        s = s_ref[h]
        att = _dot_nt(qd, kd)
        o = _dot(qd, s.astype(BF16))
        k_end = (k * jnp.exp(b_end - b)).astype(BF16)
        la_hi, la_lo = _split2(la)
        dec = jnp.exp(_dot_tn(la_hi, ones) + _dot_tn(la_lo, ones))
        dec = jnp.concatenate([dec] * (dv // LANE), axis=1)
        s_ref[h] = s * dec + _dot_tn(k_end, v)
        yield
        o = o + _dot(jnp.where(tri, att, 0.0).astype(BF16), v)
        yield
        ms = jnp.mean(o * o, axis=-1, keepdims=True)
        on = o * lax.rsqrt(ms + EPS) * gn_ref[...]
        zg = zg_ref[0, :, vs]
        o_ref[0, :, vs] = (on * (zg * _sigmoid(zg))).astype(BF16)

    _lockstep([head(h) for h in range(heads)])

    @pl.when(l == pl.num_programs(1) - 1)
    def _():
        sn_ref[0] = s_ref[...]


def _gla(proj3, offs, w_alpha2p, b_alpha, gla_norm, s0, *, dk, dv, heads):
    B, L, _ = proj3.shape
    c = min(GLA_CHUNK, L)
    qk, gw = heads * dk, heads * dv
    assert L % c == 0 and dv % LANE == 0
    for name, wdt in (("q", qk), ("k", qk), ("v", gw), ("zg", gw), ("za", LANE)):
        assert offs[name] % wdt == 0
    jq, jk, jv, jg, jz = (offs["q"] // qk, offs["k"] // qk, offs["v"] // gw, offs["zg"] // gw,
                          offs["za"] // LANE)
    return pl.pallas_call(
        functools.partial(_gla_kernel, scale=dk ** -0.5, c=c, heads=heads),
        grid=(B, L // c),
        in_specs=[pl.BlockSpec((1, c, qk), lambda b, l: (b, l, jq)),
                  pl.BlockSpec((1, c, qk), lambda b, l: (b, l, jk)),
                  pl.BlockSpec((1, c, gw), lambda b, l: (b, l, jv)),
                  pl.BlockSpec((1, c, LANE), lambda b, l: (b, l, jz)),
                  pl.BlockSpec((1, c, gw), lambda b, l: (b, l, jg)),
                  pl.BlockSpec((LANE, qk), lambda b, l: (0, 0)),
                  pl.BlockSpec((1, qk), lambda b, l: (0, 0)),
                  pl.BlockSpec((1, dv), lambda b, l: (0, 0)),
                  pl.BlockSpec((1, heads, dk, dv), lambda b, l: (b, 0, 0, 0))],
        out_specs=[pl.BlockSpec((1, c, gw), lambda b, l: (b, l, 0)),
                   pl.BlockSpec((1, heads, dk, dv), lambda b, l: (b, 0, 0, 0))],
        out_shape=[jax.ShapeDtypeStruct((B, L, gw), BF16),
                   jax.ShapeDtypeStruct((B, heads, dk, dv), F32)],
        scratch_shapes=[pltpu.VMEM((heads, dk, dv), F32)],
        compiler_params=_cparams(("parallel", "arbitrary")),
        name="gla",
    )(proj3, proj3, proj3, proj3, proj3, w_alpha2p, b_alpha.reshape(1, -1), gla_norm.reshape(1, dv), s0)


def _shifted(cur, first_row):
    rolled = pltpu.roll(cur, 1, 0)
    row = lax.broadcasted_iota(jnp.int32, cur.shape, 0)
    return jnp.where(row == 0, first_row, rolled)


def _rwkv_prep_kernel(r_ref, xw_ref, kr_ref, vr_ref, xa_ref, xg_ref,
                      sr_ref, sxw_ref, skr_ref, svr_ref, sxa_ref, sxg_ref,
                      mr_ref, mxw_ref, mkr_ref, mvr_ref, mxa_ref, mxg_ref,
                      w0_ref, wd_ref, a0_ref, wi_ref, wg_ref,
                      ro_ref, lw_ref, ko_ref, vo_ref, ao_ref, go_ref,
                      cr_ref, cxw_ref, ckr_ref, cvr_ref, cxa_ref, cxg_ref):
    l = pl.program_id(1)
    tl = r_ref.shape[1]

    def shift_mix(z_ref, s_ref, m_ref, c_ref):
        cur = z_ref[0]
        first = jnp.where(l == 0, s_ref[0], c_ref[...])
        prev = _shifted(cur, first)
        c_ref[...] = cur[tl - 1:tl, :]
        return cur + (prev - cur) * m_ref[...]

    ro_ref[0] = shift_mix(r_ref, sr_ref, mr_ref, cr_ref)
    ko_ref[0] = shift_mix(kr_ref, skr_ref, mkr_ref, ckr_ref)
    vo_ref[0] = shift_mix(vr_ref, svr_ref, mvr_ref, cvr_ref)
    xw = shift_mix(xw_ref, sxw_ref, mxw_ref, cxw_ref)
    xa = shift_mix(xa_ref, sxa_ref, mxa_ref, cxa_ref)
    xg = shift_mix(xg_ref, sxg_ref, mxg_ref, cxg_ref)
    z = w0_ref[...] + _dot(jnp.tanh(xw).astype(BF16), wd_ref[...])
    lw_ref[0] = -DECAY_SCALE * _sigmoid(z)
    ao_ref[0] = _sigmoid(a0_ref[...] + _dot(xa.astype(BF16), wi_ref[...]))
    go_ref[0] = _dot(_sigmoid(xg).astype(BF16), wg_ref[...])


def _rwkv_prep(proj3, offs, shift_parts, mu_parts, w0, w_decay2p, a0, w_iclr2p, w_gate2, *, rw, gl):
    B, L, _ = proj3.shape
    tl = _pick_tile(L, 256, SUBLANE)
    names = ("r", "xw", "kr", "vr", "xa", "xg")
    widths = {"r": rw, "xw": LANE, "kr": rw, "vr": rw, "xa": LANE, "xg": gl}
    in_specs, args = [], []
    for n in names:
        w = widths[n]
        assert offs[n] % w == 0
        in_specs.append(pl.BlockSpec((1, tl, w), functools.partial(lambda b, l, j: (b, l, j), j=offs[n] // w)))
        args.append(proj3)
    for n in names:
        in_specs.append(pl.BlockSpec((1, 1, widths[n]), lambda b, l: (b, 0, 0)))
        args.append(shift_parts[n])
    for n in names:
        in_specs.append(pl.BlockSpec((1, widths[n]), lambda b, l: (0, 0)))
        args.append(mu_parts[n])
    for arr in (w0.reshape(1, rw), w_decay2p, a0.reshape(1, rw), w_iclr2p, w_gate2):
        in_specs.append(pl.BlockSpec(arr.shape, lambda b, l: (0, 0)))
        args.append(arr)
    out_spec = pl.BlockSpec((1, tl, rw), lambda b, l: (b, l, 0))
    out_sd = jax.ShapeDtypeStruct((B, L, rw), F32)
    return pl.pallas_call(
        _rwkv_prep_kernel,
        grid=(B, L // tl),
        in_specs=in_specs,
        out_specs=[out_spec] * 6,
        out_shape=[out_sd] * 6,
        scratch_shapes=[pltpu.VMEM((1, widths[n]), F32) for n in names],
        compiler_params=_cparams(("parallel", "arbitrary")),
        name="rwkv_prep",
    )(*args)


def _rwkv_chunk_kernel(r_ref, lw_ref, kr_ref, v_ref, a_ref, g_ref,
                       kk_w_ref, ka_w_ref, rk_w_ref, lnw_ref, lnb_ref, s0_ref,
                       o_ref, sn_ref, w_ref, *, c, hd, gp):
    l = pl.program_id(2)
    tl = r_ref.shape[1]
    n2 = 2 * c
    lane_c = lax.broadcasted_iota(jnp.int32, (c, LANE), 1)
    head0_c = lane_c < hd
    row2 = lax.broadcasted_iota(jnp.int32, (n2, n2), 0)
    col2 = lax.broadcasted_iota(jnp.int32, (n2, n2), 1)
    same = (2 * row2 + 1 - n2) * (2 * col2 + 1 - n2) > 0
    strict = jnp.logical_and(same, row2 > col2)
    incl = jnp.logical_and(same, row2 >= col2)
    eye = jnp.where(row2 == col2, 1.0, 0.0)
    tri = jnp.where(_tri_incl(c), 1.0, 0.0).astype(BF16)
    lr = lax.broadcasted_iota(jnp.int32, (LANE, LANE), 0)
    lc = lax.broadcasted_iota(jnp.int32, (LANE, LANE), 1)
    seg = jnp.where((2 * lr + 1 - LANE) * (2 * lc + 1 - LANE) > 0, 1.0, 0.0).astype(BF16)

    @pl.when(l == 0)
    def _():
        lane_s = lax.broadcasted_iota(jnp.int32, (hd, LANE), 1)
        for gi in range(gp):
            s0 = s0_ref[0, gi]
            w_ref[gi] = jnp.concatenate([jnp.where(lane_s < hd, s0, 0.0),
                                         jnp.where(lane_s < hd, 0.0, s0)], axis=0)

    def stack(x):
        return jnp.concatenate([jnp.where(head0_c, x, 0.0), jnp.where(head0_c, 0.0, x)], axis=0)

    def unstack(x):
        return x[0:c] + x[c:n2]

    def chunk_pair(sl, gi):
        ls = slice(gi * LANE, (gi + 1) * LANE)
        r = r_ref[0, sl, ls]
        lw = lw_ref[0, sl, ls]
        kr = kr_ref[0, sl, ls]
        v = v_ref[0, sl, ls]
        a = a_ref[0, sl, ls]
        kk = kr * kk_w_ref[:, ls]
        kk_ss = _dot_ones(kk * kk, seg)
        cum = _ones_dot(tri, lw)
        yield
        kk = kk / jnp.maximum(jnp.sqrt(kk_ss), KK_EPS)
        k = kr * (1.0 + (a - 1.0) * ka_w_ref[:, ls])
        cum_end = cum[c - 1:c, :]
        e_neg = jnp.exp(-cum)
        e_rem = jnp.exp(cum_end - cum)
        al = kk * a
        al_t = stack(al * e_neg).astype(BF16)
        k_t = stack(k * e_neg).astype(BF16)
        be_t = stack(-kk * jnp.exp(cum - lw)).astype(BF16)
        r_t = stack(r * jnp.exp(cum)).astype(BF16)
        v_s = stack(v).astype(BF16)
        lhs = jnp.concatenate([be_t, r_t], axis=0)
        sc_a = _dot_nt(lhs, al_t)
        sc_k = _dot_nt(lhs, k_t)
        w = w_ref[gi]
        rd = _dot_nt(lhs, w.astype(BF16))
        yield
        l_a =jnp.where(strict, sc_a[0:n2], 0.0)
        l_k = jnp.where(strict, sc_k[0:n2], 0.0).astype(BF16)
        m_a = jnp.where(incl, sc_a[n2:2 * n2], 0.0).astype(BF16)
        m_k = jnp.where(incl, sc_k[n2:2 * n2], 0.0).astype(BF16)
        t_inv = eye + l_a
        lp = l_a
        rhs_u = rd[0:n2] + _dot(l_k, v_s)
        o_s = rd[n2:2 * n2] + _dot(m_k, v_s)
        bonus = _dot_ones(r * k * rk_w_ref[:, ls], seg) * v
        span = 1
        while 2 * span < c:
            lpb = lp.astype(BF16)
            lp = _dot(lpb, lpb)
            yield
            t_inv = t_inv + _dot(t_inv.astype(BF16), lp.astype(BF16))
            yield
            span *= 2
        u_s = _dot(t_inv.astype(BF16), rhs_u.astype(BF16))
        yield
        u_b = u_s.astype(BF16)
        o_s = o_s + _dot(m_a, u_b)
        uv = jnp.concatenate([u_b, v_s], axis=0)
        ak = jnp.concatenate([stack(al * e_rem), stack(k * e_rem)], axis=0).astype(BF16)
        w_ref[gi] = w * jnp.exp(cum_end) + _dot_tn(uv, ak)
        yield
        o = unstack(o_s)
        mu = _dot_ones(o, seg) * (1.0 / hd)
        yield
        d = o - mu
        var = _dot_ones(d * d, seg) * (1.0 / hd)
        yield
        on = d * lax.rsqrt(var + GN_EPS) * lnw_ref[:, ls] + lnb_ref[:, ls]
        o_ref[0, sl, ls] = ((on + bonus) * g_ref[0, sl, ls]).astype(BF16)

    def chunk(ci, _):
        sl = pl.ds(pl.multiple_of(ci * c, c), c)
        _lockstep([chunk_pair(sl, gi) for gi in range(gp)])
        return 0

    lax.fori_loop(0, tl // c, chunk, 0)

    @pl.when(l == pl.num_programs(2) - 1)
    def _():
        for gi in range(gp):
            wf = w_ref[gi]
            sn_ref[0, gi] = wf[0:hd] + wf[hd:2 * hd]


def _rwkv_chunk(r, lw, kr, vr, a, g, k_k, k_a, r_k, ln_w, ln_b, s0p, *, hd):
    B, L, RW = r.shape
    assert 2 * hd == LANE and RW % LANE == 0
    c = min(RWKV_CHUNK, L)
    tl = _pick_tile(L, 512, c)
    assert L % c == 0 and tl % c == 0 and c % SUBLANE == 0
    npair = RW // LANE
    gp = _pick_tile(npair, 8 if L > c else 16, 1)
    seq = pl.BlockSpec((1, tl, gp * LANE), lambda b, p, l: (b, l, p))
    par = pl.BlockSpec((1, gp * LANE), lambda b, p, l: (0, p))
    st = pl.BlockSpec((1, gp, hd, LANE), lambda b, p, l: (b, p, 0, 0))
    return pl.pallas_call(
        functools.partial(_rwkv_chunk_kernel, c=c, hd=hd, gp=gp),
        grid=(B, npair // gp, L // tl),
        in_specs=[seq] * 6 + [par] * 5 + [st],
        out_specs=[seq, st],
        out_shape=[jax.ShapeDtypeStruct((B, L, RW), BF16),
                   jax.ShapeDtypeStruct((B, npair, hd, LANE), F32)],
        scratch_shapes=[pltpu.VMEM((gp, LANE, LANE), F32)],
        compiler_params=_cparams(("parallel", "parallel", "arbitrary")),
        name="rwkv_chunk",
    )(r, lw, kr, vr, a, g, k_k.reshape(1, RW), k_a.reshape(1, RW), r_k.reshape(1, RW),
      ln_w.reshape(1, RW), ln_b.reshape(1, RW), s0p)


def _conv_act_kernel(g_ref, v_ref, s_ref, cw_ref, cb_ref, o_ref, c_ref):
    l = pl.program_id(2)
    tl = g_ref.shape[1]
    g = g_ref[0]
    prev = jnp.where(l == 0, s_ref[0], c_ref[...])
    row = lax.broadcasted_iota(jnp.int32, g.shape, 0)
    g1 = jnp.where(row == 0, prev[1:2], pltpu.roll(g, 1, 0))
    g2 = jnp.where(row == 0, prev[0:1], jnp.where(row == 1, prev[1:2], pltpu.roll(g, 2, 0)))
    c_ref[...] = g[tl - 2:tl, :]
    conv = cb_ref[...] + g2 * cw_ref[0:1, :] + g1 * cw_ref[1:2, :] + g * cw_ref[2:3, :]
    o_ref[0] = (_gelu_tanh(conv) * v_ref[0]).astype(BF16)


def _conv_act(up3, s_conv, conv_w, conv_b, *, dff):
    B, L, _ = up3.shape
    assert conv_w.shape[0] == 3 and s_conv.shape[1] == 2 and L >= 2
    tl = _pick_tile(L, 2048, SUBLANE)
    tn = _pick_tile(dff, max(LANE, (1 << 20) // tl), LANE)
    nj = dff // tn
    return pl.pallas_call(
        _conv_act_kernel,
        grid=(B, nj, L // tl),
        in_specs=[pl.BlockSpec((1, tl, tn), lambda b, j, l: (b, l, j)),
                  pl.BlockSpec((1, tl, tn), lambda b, j, l: (b, l, nj + j)),
                  pl.BlockSpec((1, 2, tn), lambda b, j, l: (b, 0, j)),
                  pl.BlockSpec((3, tn), lambda b, j, l: (0, j)),
                  pl.BlockSpec((1, tn), lambda b, j, l: (0, j))],
        out_specs=pl.BlockSpec((1, tl, tn), lambda b, j, l: (b, l, j)),
        out_shape=jax.ShapeDtypeStruct((B, L, dff), BF16),
        scratch_shapes=[pltpu.VMEM((2, tn), F32)],
        compiler_params=_cparams(("parallel", "parallel", "arbitrary")),
        name="conv_act",
    )(up3, up3, s_conv, conv_w, conv_b.reshape(1, dff))


def _pad_cols(a, width):
    return a if a.shape[-1] == width else jnp.pad(a, [(0, 0)] * (a.ndim - 1) + [(0, width - a.shape[-1])])


def _pad_rows(a, height):
    return a if a.shape[0] == height else jnp.pad(a, [(0, height - a.shape[0])] + [(0, 0)] * (a.ndim - 1))


def _layout(D, QK, GW, R, RW, WL, AL, GL):
    src, o = {}, 0
    for name, w in (("q", QK), ("k", QK), ("v", GW), ("za", R), ("zg", GW), ("r", RW), ("xw", WL),
                    ("kr", RW), ("vr", RW), ("xa", AL), ("xg", GL), ("ga", D), ("gb", D)):
        src[name] = (o, w)
        o += w
    pad = lambda w: -(-w // LANE) * LANE
    order = sorted(src, key=lambda n: -pad(src[n][1]))
    dst, o = {}, 0
    for name in order:
        dst[name] = o
        o += pad(src[name][1])
    return src, dst, order, o


def _prepare_params(lp):
    (w_in, w_alpha2, b_alpha, gla_norm, w_branch_a,
     mu_shift, w0, w_decay2, a0, w_iclr2, w_gate2, k_k, k_a, r_k, ln_x_w, ln_x_b, w_branch_b,
     w_out, g_pre_mix, g_post_mix, g_pre_ffn, g_post_ffn,
     w_up, conv_w, conv_b, w_down, g_pe, w_pe_gate, w_pe) = lp
    D = w_in.shape[0]
    R, QK = w_alpha2.shape
    GW = w_branch_a.shape[0]
    RW = w_branch_b.shape[0]
    WL, AL, GL = w_decay2.shape[0], w_iclr2.shape[0], w_gate2.shape[0]
    src, dst, order, total = _layout(D, QK, GW, R, RW, WL, AL, GL)
    n_pad = -(-total // 512) * 512
    pad = lambda w: -(-w // LANE) * LANE
    cols = [_pad_cols(w_in[:, src[n][0]:src[n][0] + src[n][1]], pad(src[n][1])) for n in order]
    if n_pad > total:
        cols.append(jnp.zeros((D, n_pad - total), w_in.dtype))
    prm = dict(
        dims=dict(D=D, R=R, QK=QK, GW=GW, RW=RW, WL=WL, AL=AL, GL=GL, DV=gla_norm.shape[0],
                  HD=r_k.shape[1], DFF=conv_b.shape[0]),
        src=src, dst=dst,
        w_in=jnp.concatenate(cols, axis=1).astype(BF16),
        w_alpha2=_pad_rows(w_alpha2, LANE).astype(BF16), b_alpha=b_alpha, gla_norm=gla_norm,
        w_branch_a=w_branch_a.astype(BF16), w_branch_b=w_branch_b.astype(BF16),
        w0=w0, w_decay2=_pad_rows(w_decay2, LANE).astype(BF16), a0=a0,
        w_iclr2=_pad_rows(w_iclr2, LANE).astype(BF16), w_gate2=w_gate2.astype(BF16),
        k_k=k_k, k_a=k_a, r_k=r_k.reshape(-1), ln_x_w=ln_x_w, ln_x_b=ln_x_b,
        w_out=w_out.astype(BF16), g_pre_mix=g_pre_mix, g_post_mix=g_post_mix,
        g_pre_ffn=g_pre_ffn, g_post_ffn=g_post_ffn,
        w_up=w_up.astype(BF16), conv_w=conv_w, conv_b=conv_b, w_down=w_down.astype(BF16),
        g_pe=g_pe, w_pe_gate=w_pe_gate.astype(BF16), w_pe=w_pe.astype(BF16),
    )
    rsrc = src["r"][0]
    rnames = ("r", "xw", "kr", "vr", "xa", "xg")
    prm["rnames"] = rnames
    prm["rsl"] = {n: (src[n][0] - rsrc, src[n][1]) for n in rnames}
    prm["mu"] = {n: _pad_cols(mu_shift[None, prm["rsl"][n][0]:prm["rsl"][n][0] + prm["rsl"][n][1]],
                              pad(prm["rsl"][n][1])) for n in rnames}
    return prm


def _pack_state(s, hd):
    B, H = s.shape[:2]
    return s.reshape(B, H // 2, 2, hd, hd).transpose(0, 1, 3, 2, 4).reshape(B, H // 2, hd, 2 * hd)


def _unpack_state(sp, hd):
    B, NP = sp.shape[:2]
    return sp.reshape(B, NP, hd, 2, hd).transpose(0, 1, 3, 2, 4).reshape(B, 2 * NP, hd, hd)


def _layer(x, p, s_gla, s_rwkv, s_shift, s_conv, prm):
    B, L, D = x.shape
    M = B * L
    dm, dst = prm["dims"], prm["dst"]
    GW, RW, DV, HD, DFF, QK = dm["GW"], dm["RW"], dm["DV"], dm["HD"], dm["DFF"], dm["QK"]
    heads = GW // DV
    pad = lambda w: -(-w // LANE) * LANE
    x2 = x.reshape(M, D)

    proj = _norm_matmul(x2, prm["g_pre_mix"], prm["w_in"])
    proj3 = proj.reshape(B, L, -1)

    o_a, s_gla_new = _gla(proj3, dst, prm["w_alpha2"], prm["b_alpha"], prm["gla_norm"], s_gla,
                          dk=QK // heads, dv=DV, heads=heads)

    shift_parts = {n: _pad_cols(s_shift[:, None, prm["rsl"][n][0]:prm["rsl"][n][0] + prm["rsl"][n][1]],
                                pad(prm["rsl"][n][1])) for n in prm["rnames"]}
    r, lw, kr, vr, a, g = _rwkv_prep(proj3, dst, shift_parts, prm["mu"], prm["w0"], prm["w_decay2"],
                                     prm["a0"], prm["w_iclr2"], prm["w_gate2"], rw=RW, gl=dm["GL"])
    o_b, s_rwkv_new = _rwkv_chunk(r, lw, kr, vr, a, g, prm["k_k"], prm["k_a"], prm["r_k"],
                                  prm["ln_x_w"], prm["ln_x_b"], _pack_state(s_rwkv, HD), hd=HD)
    new_shift = jnp.concatenate(
        [proj3[:, L - 1, dst[n]:dst[n] + prm["rsl"][n][1]] for n in prm["rnames"]], axis=-1)

    mixed = _merge(o_a.reshape(M, GW), o_b.reshape(M, RW), prm["w_branch_a"], prm["w_branch_b"],
                   proj, dst["ga"], dst["gb"])
    x2 = _matmul_norm_residual(mixed, prm["w_out"], x2, prm["g_post_mix"], tm=256, tn=512)

    up = _norm_matmul(x2, prm["g_pre_ffn"], prm["w_up"])
    up3 = up.reshape(B, L, 2 * DFF)
    act = _conv_act(up3, s_conv, prm["conv_w"], prm["conv_b"], dff=DFF)
    new_conv = up3[:, L - 2:, :DFF]
    x2 = _matmul_norm_residual(act.reshape(M, DFF), prm["w_down"], x2, prm["g_post_ffn"], tm=256, tn=256)

    x2 = _pe_layer(x2, prm["g_pe"], prm["w_pe_gate"], p.reshape(M, -1), prm["w_pe"])
    return x2.reshape(B, L, D), s_gla_new, _unpack_state(s_rwkv_new, HD), new_shift, new_conv


def kernel(x_prompt, x_sample, state_gla, state_rwkv, state_shift, state_ffn_conv, p_prompt, p_sample, w_in, w_alpha2, b_alpha, gla_norm, w_branch_a, mu_shift, w0, w_decay2, a0, w_iclr2, w_gate2, k_k, k_a, r_k, ln_x_w, ln_x_b, w_branch_b, w_out, g_pre_mix, g_post_mix, g_pre_ffn, g_post_ffn, w_up, conv_w, conv_b, w_down, g_pe, w_pe_gate, w_pe):
    params = (w_in, w_alpha2, b_alpha, gla_norm, w_branch_a,
              mu_shift, w0, w_decay2, a0, w_iclr2, w_gate2, k_k, k_a, r_k, ln_x_w, ln_x_b, w_branch_b,
              w_out, g_pre_mix, g_post_mix, g_pre_ffn, g_post_ffn,
              w_up, conv_w, conv_b, w_down, g_pe, w_pe_gate, w_pe)
    depth = w_in.shape[0]
    nb = x_prompt.shape[0]
    yp, ys = x_prompt, x_sample
    outs_p = [[], [], [], []]
    outs_s = [[], [], [], []]
    for i in range(depth):
        prm = _prepare_params(tuple(t[i] for t in params))
        z_gla = jnp.zeros((nb,) + state_gla.shape[2:], F32)
        z_rwkv = jnp.zeros((nb,) + state_rwkv.shape[2:], F32)
        z_shift = jnp.zeros((nb,) + state_shift.shape[2:], x_prompt.dtype)
        z_conv = jnp.zeros((nb,) + state_ffn_conv.shape[2:], x_prompt.dtype)
        yp, *st = _layer(yp, p_prompt[i], z_gla, z_rwkv, z_shift, z_conv, prm)
        for acc, s in zip(outs_p, st):
            acc.append(s)
        ys, *st = _layer(ys, p_sample[i], state_gla[i], state_rwkv[i], state_shift[i],
                         state_ffn_conv[i], prm)
        for acc, s in zip(outs_s, st):
            acc.append(s)
    return (yp, ys, *(jnp.stack(a) for a in outs_p), *(jnp.stack(a) for a in outs_s))
```

```python
import functools
import math

import jax
import jax.numpy as jnp
from jax import lax
from jax.experimental import pallas as pl
from jax.experimental.pallas import tpu as pltpu

F32 = jnp.float32
BF16 = jnp.bfloat16

LANE = 128
SUBLANE = 8
VMEM_LIMIT_BYTES = 56 * 1024 * 1024

EPS = 1e-6
GN_EPS = 64e-5
GLA_TAU = 16.0
GLA_CHUNK = 64
RWKV_CHUNK = 64
KK_EPS = 1e-12
DECAY_SCALE = math.exp(-0.5)
GELU_C = math.sqrt(2.0 / math.pi)


ROW_RESIDENT = pl.Buffered(1)


def _cparams(sem):
    return pltpu.CompilerParams(dimension_semantics=sem, vmem_limit_bytes=VMEM_LIMIT_BYTES)


def _sigmoid(x):
    return 1.0 / (1.0 + jnp.exp(-x))


def _log_sigmoid(x):
    return jnp.minimum(x, 0.0) - jnp.log(1.0 + jnp.exp(-jnp.abs(x)))


def _gelu_tanh(x):
    return 0.5 * x * (1.0 + jnp.tanh(GELU_C * (x + 0.044715 * (x * x * x))))


def _split2(x):
    hi = x.astype(BF16)
    lo = (x - hi.astype(F32)).astype(BF16)
    return hi, lo


def _dot(a, b):
    return jnp.dot(a, b, preferred_element_type=F32)


def _dot_nt(a, b):
    return lax.dot_general(a, b, (((1,), (1,)), ((), ())), preferred_element_type=F32)


def _dot_tn(a, b):
    return lax.dot_general(a, b, (((0,), (0,)), ((), ())), preferred_element_type=F32)


def _ones_dot(m01, x):
    hi, lo = _split2(x)
    return _dot(m01, hi) + _dot(m01, lo)


def _dot_ones(x, m01):
    hi, lo = _split2(x)
    return _dot(hi, m01) + _dot(lo, m01)


def _lockstep(gens):
    gens = list(gens)
    while gens:
        alive = []
        for g in gens:
            try:
                next(g)
                alive.append(g)
            except StopIteration:
                pass
        gens = alive


def _tri_incl(n):
    r = lax.broadcasted_iota(jnp.int32, (n, n), 0)
    c = lax.broadcasted_iota(jnp.int32, (n, n), 1)
    return r >= c


def _pick_tile(n, target, mult):
    if n <= target:
        return n
    best = None
    t = mult
    while t <= target:
        if n % t == 0:
            best = t
        t += mult
    assert best is not None, (n, target, mult)
    return best


def _rms_rows(x_ref, g_ref, h_ref, rows):
    tm = x_ref.shape[0]

    def body(i, _):
        sl = pl.ds(pl.multiple_of(i * rows, rows), rows)
        x = x_ref[sl, :]
        ms = jnp.mean(x * x, axis=-1, keepdims=True)
        h_ref[sl, :] = (x * lax.rsqrt(ms + EPS) * g_ref[...]).astype(BF16)
        return 0

    lax.fori_loop(0, tm // rows, body, 0)


def _norm_mm_kernel(x_ref, g_ref, w_ref, o_ref, h_ref, *, rows):
    @pl.when(pl.program_id(1) == 0)
    def _():
        _rms_rows(x_ref, g_ref, h_ref, rows)

    o_ref[...] = _dot(h_ref[...], w_ref[...])


def _norm_matmul(x, gain, w, *, tm=1024, tn=512):
    M, K = x.shape
    N = w.shape[1]
    tm = _pick_tile(M, tm, SUBLANE)
    tn = _pick_tile(N, tn, LANE)
    rows = _pick_tile(tm, 64, SUBLANE)
    return pl.pallas_call(
        functools.partial(_norm_mm_kernel, rows=rows),
        grid=(M // tm, N // tn),
        in_specs=[pl.BlockSpec((tm, K), lambda i, j: (i, 0), pipeline_mode=ROW_RESIDENT),
                  pl.BlockSpec((1, K), lambda i, j: (0, 0)),
                  pl.BlockSpec((K, tn), lambda i, j: (0, j))],
        out_specs=pl.BlockSpec((tm, tn), lambda i, j: (i, j)),
        out_shape=jax.ShapeDtypeStruct((M, N), F32),
        scratch_shapes=[pltpu.VMEM((tm, K), BF16)],
        compiler_params=_cparams(("parallel", "arbitrary")),
        name="norm_matmul",
    )(x, gain.reshape(1, K), w)


def _pe_kernel(x_ref, g_ref, w_ref, p_ref, wp_ref, o_ref, h_ref, *, rows, tn):
    j = pl.program_id(1)

    @pl.when(j == 0)
    def _():
        _rms_rows(x_ref, g_ref, h_ref, rows)

    gate = _sigmoid(_dot(h_ref[...], w_ref[...]))
    pe = _dot(p_ref[...].astype(BF16), wp_ref[...])
    xs = x_ref[:, pl.ds(pl.multiple_of(j * tn, tn), tn)]
    o_ref[...] = xs + gate * pe


def _pe_layer(x, gain, w_gate, p, w_pe, *, tm=1024, tn=512):
    M, K = x.shape
    N = w_gate.shape[1]
    P = p.shape[1]
    tm = _pick_tile(M, tm, SUBLANE)
    tn = _pick_tile(N, tn, LANE)
    rows = _pick_tile(tm, 64, SUBLANE)
    return pl.pallas_call(
        functools.partial(_pe_kernel, rows=rows, tn=tn),
        grid=(M // tm, N // tn),
        in_specs=[pl.BlockSpec((tm, K), lambda i, j: (i, 0), pipeline_mode=ROW_RESIDENT),
                  pl.BlockSpec((1, K), lambda i, j: (0, 0)),
                  pl.BlockSpec((K, tn), lambda i, j: (0, j)),
                  pl.BlockSpec((tm, P), lambda i, j: (i, 0)),
                  pl.BlockSpec((P, tn), lambda i, j: (0, j))],
        out_specs=pl.BlockSpec((tm, tn), lambda i, j: (i, j)),
        out_shape=jax.ShapeDtypeStruct((M, N), F32),
        scratch_shapes=[pltpu.VMEM((tm, K), BF16)],
        compiler_params=_cparams(("parallel", "arbitrary")),
        name="pe_layer",
    )(x, gain.reshape(1, K), w_gate, p, w_pe)


def _mm_norm_res_kernel(a_ref, w_ref, x_ref, g_ref, o_ref, *, tn, rows):
    j = pl.program_id(1)
    o_ref[:, pl.ds(pl.multiple_of(j * tn, tn), tn)] = _dot(a_ref[...], w_ref[...])

    @pl.when(j == pl.num_programs(1) - 1)
    def _():
        tm = o_ref.shape[0]

        def body(i, _):
            sl = pl.ds(pl.multiple_of(i * rows, rows), rows)
            y = o_ref[sl, :]
            ms = jnp.mean(y * y, axis=-1, keepdims=True)
            o_ref[sl, :] = x_ref[sl, :] + y * lax.rsqrt(ms + EPS) * g_ref[...]
            return 0

        lax.fori_loop(0, tm // rows, body, 0)


def _matmul_norm_residual(a, w, x, gain, *, tm, tn):
    M, K = a.shape
    N = w.shape[1]
    tm = _pick_tile(M, tm, SUBLANE)
    tn = _pick_tile(N, tn, LANE)
    rows = _pick_tile(tm, 64, SUBLANE)
    return pl.pallas_call(
        functools.partial(_mm_norm_res_kernel, tn=tn, rows=rows),
        grid=(M // tm, N // tn),
        in_specs=[pl.BlockSpec((tm, K), lambda i, j: (i, 0), pipeline_mode=ROW_RESIDENT),
                  pl.BlockSpec((K, tn), lambda i, j: (0, j)),
                  pl.BlockSpec((tm, N), lambda i, j: (i, 0), pipeline_mode=ROW_RESIDENT),
                  pl.BlockSpec((1, N), lambda i, j: (0, 0))],
        out_specs=pl.BlockSpec((tm, N), lambda i, j: (i, 0)),
        out_shape=jax.ShapeDtypeStruct((M, N), F32),
        compiler_params=_cparams(("parallel", "arbitrary")),
        name="matmul_norm_residual",
    )(a, w, x, gain.reshape(1, N))


def _merge_kernel(oa_ref, ob_ref, wa_ref, wb_ref, ga_ref, gb_ref, o_ref):
    ya = _dot(oa_ref[...], wa_ref[...])
    yb = _dot(ob_ref[...], wb_ref[...])
    o_ref[...] = (_sigmoid(ga_ref[...]) * ya + _sigmoid(gb_ref[...]) * yb).astype(BF16)


def _merge(oa, ob, wa, wb, proj, ga_off, gb_off, *, tm=512, tn=1024):
    M, KA = oa.shape
    KB = ob.shape[1]
    N = wa.shape[1]
    tm = _pick_tile(M, tm, SUBLANE)
    tn = _pick_tile(N, tn, LANE)
    assert ga_off % tn == 0 and gb_off % tn == 0
    ja, jb = ga_off // tn, gb_off // tn
    return pl.pallas_call(
        _merge_kernel,
        grid=(M // tm, N // tn),
        in_specs=[pl.BlockSpec((tm, KA), lambda i, j: (i, 0)),
                  pl.BlockSpec((tm, KB), lambda i, j: (i, 0)),
                  pl.BlockSpec((KA, tn), lambda i, j: (0, j)),
                  pl.BlockSpec((KB, tn), lambda i, j: (0, j)),
                  pl.BlockSpec((tm, tn), lambda i, j: (i, ja + j)),
                  pl.BlockSpec((tm, tn), lambda i, j: (i, jb + j))],
        out_specs=pl.BlockSpec((tm, tn), lambda i, j: (i, j)),
        out_shape=jax.ShapeDtypeStruct((M, N), BF16),
        compiler_params=_cparams(("parallel", "arbitrary")),
        name="merge",
    )(oa, ob, wa, wb, proj, proj)


def _gla_kernel(q_ref, k_ref, v_ref, za_ref, zg_ref, wa_ref, ba_ref, gn_ref, s0_ref,
                o_ref, sn_ref, s_ref, *, scale, c, heads):
    l = pl.program_id(1)

    @pl.when(l == 0)
    def _():
        s_ref[...] = s0_ref[0]

    dk = q_ref.shape[2] // heads
    dv = v_ref.shape[2] // heads
    x = _dot(za_ref[0].astype(BF16), wa_ref[...]) + ba_ref[...]
    la_all = _log_sigmoid(x) * (1.0 / GLA_TAU)
    tri = _tri_incl(c)
    b_all = _ones_dot(jnp.where(tri, 1.0, 0.0).astype(BF16), la_all)
    ones = jnp.ones((c, LANE), BF16)
    def head(h):
        ks = slice(h * dk, (h + 1) * dk)
        vs = slice(h * dv, (h + 1) * dv)
        la = la_all[:, ks]
        b = b_all[:, ks]
        b_end = b[c - 1:c, :]
        q = q_ref[0, :, ks] * scale
        k = k_ref[0, :, ks]
        v = v_ref[0, :, vs].astype(BF16)
        qd = (q * jnp.exp(b)).astype(BF16)
        kd = (k * jnp.exp(-b)).astype(BF16)
        s = s_ref[h]
        att = _dot_nt(qd, kd)
        o = _dot(qd, s.astype(BF16))
        k_end = (k * jnp.exp(b_end - b)).astype(BF16)
        la_hi, la_lo = _split2(la)
        dec = jnp.exp(_dot_tn(la_hi, ones) + _dot_tn(la_lo, ones))
        dec = jnp.concatenate([dec] * (dv // LANE), axis=1)
        s_ref[h] = s * dec + _dot_tn(k_end, v)
        yield
        o = o + _dot(jnp.where(tri, att, 0.0).astype(BF16), v)
        yield
        ms = jnp.mean(o * o, axis=-1, keepdims=True)
        on = o * lax.rsqrt(ms + EPS) * gn_ref[...]
        zg = zg_ref[0, :, vs]
        o_ref[0, :, vs] = (on * (zg * _sigmoid(zg))).astype(BF16)

    _lockstep([head(h) for h in range(heads)])

    @pl.when(l == pl.num_programs(1) - 1)
    def _():
        sn_ref[0] = s_ref[...]


def _gla(proj3, offs, w_alpha2p, b_alpha, gla_norm, s0, *, dk, dv, heads):
    B, L, _ = proj3.shape
    c = min(GLA_CHUNK, L)
    qk, gw = heads * dk, heads * dv
    assert L % c == 0 and dv % LANE == 0
    for name, wdt in (("q", qk), ("k", qk), ("v", gw), ("zg", gw), ("za", LANE)):
        assert offs[name] % wdt == 0
    jq, jk, jv, jg, jz = (offs["q"] // qk, offs["k"] // qk, offs["v"] // gw, offs["zg"] // gw,
                          offs["za"] // LANE)
    return pl.pallas_call(
        functools.partial(_gla_kernel, scale=dk ** -0.5, c=c, heads=heads),
        grid=(B, L // c),
        in_specs=[pl.BlockSpec((1, c, qk), lambda b, l: (b, l, jq)),
                  pl.BlockSpec((1, c, qk), lambda b, l: (b, l, jk)),
                  pl.BlockSpec((1, c, gw), lambda b, l: (b, l, jv)),
                  pl.BlockSpec((1, c, LANE), lambda b, l: (b, l, jz)),
                  pl.BlockSpec((1, c, gw), lambda b, l: (b, l, jg)),
                  pl.BlockSpec((LANE, qk), lambda b, l: (0, 0)),
                  pl.BlockSpec((1, qk), lambda b, l: (0, 0)),
                  pl.BlockSpec((1, dv), lambda b, l: (0, 0)),
                  pl.BlockSpec((1, heads, dk, dv), lambda b, l: (b, 0, 0, 0))],
        out_specs=[pl.BlockSpec((1, c, gw), lambda b, l: (b, l, 0)),
                   pl.BlockSpec((1, heads, dk, dv), lambda b, l: (b, 0, 0, 0))],
        out_shape=[jax.ShapeDtypeStruct((B, L, gw), BF16),
                   jax.ShapeDtypeStruct((B, heads, dk, dv), F32)],
        scratch_shapes=[pltpu.VMEM((heads, dk, dv), F32)],
        compiler_params=_cparams(("parallel", "arbitrary")),
        name="gla",
    )(proj3, proj3, proj3, proj3, proj3, w_alpha2p, b_alpha.reshape(1, -1), gla_norm.reshape(1, dv), s0)


def _shifted(cur, first_row):
    rolled = pltpu.roll(cur, 1, 0)
    row = lax.broadcasted_iota(jnp.int32, cur.shape, 0)
    return jnp.where(row == 0, first_row, rolled)


def _rwkv_prep_kernel(r_ref, xw_ref, kr_ref, vr_ref, xa_ref, xg_ref,
                      sr_ref, sxw_ref, skr_ref, svr_ref, sxa_ref, sxg_ref,
                      mr_ref, mxw_ref, mkr_ref, mvr_ref, mxa_ref, mxg_ref,
                      w0_ref, wd_ref, a0_ref, wi_ref, wg_ref,
                      ro_ref, lw_ref, ko_ref, vo_ref, ao_ref, go_ref,
                      cr_ref, cxw_ref, ckr_ref, cvr_ref, cxa_ref, cxg_ref):
    l = pl.program_id(1)
    tl = r_ref.shape[1]

    def shift_mix(z_ref, s_ref, m_ref, c_ref):
        cur = z_ref[0]
        first = jnp.where(l == 0, s_ref[0], c_ref[...])
        prev = _shifted(cur, first)
        c_ref[...] = cur[tl - 1:tl, :]
        return cur + (prev - cur) * m_ref[...]

    ro_ref[0] = shift_mix(r_ref, sr_ref, mr_ref, cr_ref)
    ko_ref[0] = shift_mix(kr_ref, skr_ref, mkr_ref, ckr_ref)
    vo_ref[0] = shift_mix(vr_ref, svr_ref, mvr_ref, cvr_ref)
    xw = shift_mix(xw_ref, sxw_ref, mxw_ref, cxw_ref)
    xa = shift_mix(xa_ref, sxa_ref, mxa_ref, cxa_ref)
    xg = shift_mix(xg_ref, sxg_ref, mxg_ref, cxg_ref)
    z = w0_ref[...] + _dot(jnp.tanh(xw).astype(BF16), wd_ref[...])
    lw_ref[0] = -DECAY_SCALE * _sigmoid(z)
    ao_ref[0] = _sigmoid(a0_ref[...] + _dot(xa.astype(BF16), wi_ref[...]))
    go_ref[0] = _dot(_sigmoid(xg).astype(BF16), wg_ref[...])


def _rwkv_prep(proj3, offs, shift_parts, mu_parts, w0, w_decay2p, a0, w_iclr2p, w_gate2, *, rw, gl):
    B, L, _ = proj3.shape
    tl = _pick_tile(L, 256, SUBLANE)
    names = ("r", "xw", "kr", "vr", "xa", "xg")
    widths = {"r": rw, "xw": LANE, "kr": rw, "vr": rw, "xa": LANE, "xg": gl}
    in_specs, args = [], []
    for n in names:
        w = widths[n]
        assert offs[n] % w == 0
        in_specs.append(pl.BlockSpec((1, tl, w), functools.partial(lambda b, l, j: (b, l, j), j=offs[n] // w)))
        args.append(proj3)
    for n in names:
        in_specs.append(pl.BlockSpec((1, 1, widths[n]), lambda b, l: (b, 0, 0)))
        args.append(shift_parts[n])
    for n in names:
        in_specs.append(pl.BlockSpec((1, widths[n]), lambda b, l: (0, 0)))
        args.append(mu_parts[n])
    for arr in (w0.reshape(1, rw), w_decay2p, a0.reshape(1, rw), w_iclr2p, w_gate2):
        in_specs.append(pl.BlockSpec(arr.shape, lambda b, l: (0, 0)))
        args.append(arr)
    out_spec = pl.BlockSpec((1, tl, rw), lambda b, l: (b, l, 0))
    out_sd = jax.ShapeDtypeStruct((B, L, rw), F32)
    return pl.pallas_call(
        _rwkv_prep_kernel,
        grid=(B, L // tl),
        in_specs=in_specs,
        out_specs=[out_spec] * 6,
        out_shape=[out_sd] * 6,
        scratch_shapes=[pltpu.VMEM((1, widths[n]), F32) for n in names],
        compiler_params=_cparams(("parallel", "arbitrary")),
        name="rwkv_prep",
    )(*args)


def _rwkv_chunk_kernel(r_ref, lw_ref, kr_ref, v_ref, a_ref, g_ref,
                       kk_w_ref, ka_w_ref, rk_w_ref, lnw_ref, lnb_ref, s0_ref,
                       o_ref, sn_ref, w_ref, *, c, hd, gp):
    l = pl.program_id(2)
    tl = r_ref.shape[1]
    n2 = 2 * c
    lane_c = lax.broadcasted_iota(jnp.int32, (c, LANE), 1)
    head0_c = lane_c < hd
    row2 = lax.broadcasted_iota(jnp.int32, (n2, n2), 0)
    col2 = lax.broadcasted_iota(jnp.int32, (n2, n2), 1)
    same = (2 * row2 + 1 - n2) * (2 * col2 + 1 - n2) > 0
    strict = jnp.logical_and(same, row2 > col2)
    incl = jnp.logical_and(same, row2 >= col2)
    eye = jnp.where(row2 == col2, 1.0, 0.0)
    tri = jnp.where(_tri_incl(c), 1.0, 0.0).astype(BF16)
    lr = lax.broadcasted_iota(jnp.int32, (LANE, LANE), 0)
    lc = lax.broadcasted_iota(jnp.int32, (LANE, LANE), 1)
    seg = jnp.where((2 * lr + 1 - LANE) * (2 * lc + 1 - LANE) > 0, 1.0, 0.0).astype(BF16)

    @pl.when(l == 0)
    def _():
        lane_s = lax.broadcasted_iota(jnp.int32, (hd, LANE), 1)
        for gi in range(gp):
            s0 = s0_ref[0, gi]
            w_ref[gi] = jnp.concatenate([jnp.where(lane_s < hd, s0, 0.0),
                                         jnp.where(lane_s < hd, 0.0, s0)], axis=0)

    def stack(x):
        return jnp.concatenate([jnp.where(head0_c, x, 0.0), jnp.where(head0_c, 0.0, x)], axis=0)

    def unstack(x):
        return x[0:c] + x[c:n2]

    def chunk_pair(sl, gi):
        ls = slice(gi * LANE, (gi + 1) * LANE)
        r = r_ref[0, sl, ls]
        lw = lw_ref[0, sl, ls]
        kr = kr_ref[0, sl, ls]
        v = v_ref[0, sl, ls]
        a = a_ref[0, sl, ls]
        kk = kr * kk_w_ref[:, ls]
        kk_ss = _dot_ones(kk * kk, seg)
        cum = _ones_dot(tri, lw)
        yield
        kk = kk / jnp.maximum(jnp.sqrt(kk_ss), KK_EPS)
        k = kr * (1.0 + (a - 1.0) * ka_w_ref[:, ls])
        cum_end = cum[c - 1:c, :]
        e_neg = jnp.exp(-cum)
        e_rem = jnp.exp(cum_end - cum)
        al = kk * a
        al_t = stack(al * e_neg).astype(BF16)
        k_t = stack(k * e_neg).astype(BF16)
        be_t = stack(-kk * jnp.exp(cum - lw)).astype(BF16)
        r_t = stack(r * jnp.exp(cum)).astype(BF16)
        v_s = stack(v).astype(BF16)
        lhs = jnp.concatenate([be_t, r_t], axis=0)
        sc_a = _dot_nt(lhs, al_t)
        sc_k = _dot_nt(lhs, k_t)
        w = w_ref[gi]
        rd = _dot_nt(lhs, w.astype(BF16))
        yield
        l_a =jnp.where(strict, sc_a[0:n2], 0.0)
        l_k = jnp.where(strict, sc_k[0:n2], 0.0).astype(BF16)
        m_a = jnp.where(incl, sc_a[n2:2 * n2], 0.0).astype(BF16)
        m_k = jnp.where(incl, sc_k[n2:2 * n2], 0.0).astype(BF16)
        t_inv = eye + l_a
        lp = l_a
        rhs_u = rd[0:n2] + _dot(l_k, v_s)
        o_s = rd[n2:2 * n2] + _dot(m_k, v_s)
        bonus = _dot_ones(r * k * rk_w_ref[:, ls], seg) * v
        span = 1
        while 2 * span < c:
            lpb = lp.astype(BF16)
            lp = _dot(lpb, lpb)
            yield
            t_inv = t_inv + _dot(t_inv.astype(BF16), lp.astype(BF16))
            yield
            span *= 2
        u_s = _dot(t_inv.astype(BF16), rhs_u.astype(BF16))
        yield
        u_b = u_s.astype(BF16)
        o_s = o_s + _dot(m_a, u_b)
        uv = jnp.concatenate([u_b, v_s], axis=0)
        ak = jnp.concatenate([stack(al * e_rem), stack(k * e_rem)], axis=0).astype(BF16)
        w_ref[gi] = w * jnp.exp(cum_end) + _dot_tn(uv, ak)
        yield
        o = unstack(o_s)
        mu = _dot_ones(o, seg) * (1.0 / hd)
        yield
        d = o - mu
        var = _dot_ones(d * d, seg) * (1.0 / hd)
        yield
        on = d * lax.rsqrt(var + GN_EPS) * lnw_ref[:, ls] + lnb_ref[:, ls]
        o_ref[0, sl, ls] = ((on + bonus) * g_ref[0, sl, ls]).astype(BF16)

    def chunk(ci, _):
        sl = pl.ds(pl.multiple_of(ci * c, c), c)
        _lockstep([chunk_pair(sl, gi) for gi in range(gp)])
        return 0

    lax.fori_loop(0, tl // c, chunk, 0)

    @pl.when(l == pl.num_programs(2) - 1)
    def _():
        for gi in range(gp):
            wf = w_ref[gi]
            sn_ref[0, gi] = wf[0:hd] + wf[hd:2 * hd]


def _rwkv_chunk(r, lw, kr, vr, a, g, k_k, k_a, r_k, ln_w, ln_b, s0p, *, hd):
    B, L, RW = r.shape
    assert 2 * hd == LANE and RW % LANE == 0
    c = min(RWKV_CHUNK, L)
    tl = _pick_tile(L, 512, c)
    assert L % c == 0 and tl % c == 0 and c % SUBLANE == 0
    npair = RW // LANE
    gp = _pick_tile(npair, 8 if L > c else 16, 1)
    seq = pl.BlockSpec((1, tl, gp * LANE), lambda b, p, l: (b, l, p))
    par = pl.BlockSpec((1, gp * LANE), lambda b, p, l: (0, p))
    st = pl.BlockSpec((1, gp, hd, LANE), lambda b, p, l: (b, p, 0, 0))
    return pl.pallas_call(
        functools.partial(_rwkv_chunk_kernel, c=c, hd=hd, gp=gp),
        grid=(B, npair // gp, L // tl),
        in_specs=[seq] * 6 + [par] * 5 + [st],
        out_specs=[seq, st],
        out_shape=[jax.ShapeDtypeStruct((B, L, RW), BF16),
                   jax.ShapeDtypeStruct((B, npair, hd, LANE), F32)],
        scratch_shapes=[pltpu.VMEM((gp, LANE, LANE), F32)],
        compiler_params=_cparams(("parallel", "parallel", "arbitrary")),
        name="rwkv_chunk",
    )(r, lw, kr, vr, a, g, k_k.reshape(1, RW), k_a.reshape(1, RW), r_k.reshape(1, RW),
      ln_w.reshape(1, RW), ln_b.reshape(1, RW), s0p)


def _conv_act_kernel(g_ref, v_ref, s_ref, cw_ref, cb_ref, o_ref, c_ref):
    l = pl.program_id(2)
    tl = g_ref.shape[1]
    g = g_ref[0]
    prev = jnp.where(l == 0, s_ref[0], c_ref[...])
    row = lax.broadcasted_iota(jnp.int32, g.shape, 0)
    g1 = jnp.where(row == 0, prev[1:2], pltpu.roll(g, 1, 0))
    g2 = jnp.where(row == 0, prev[0:1], jnp.where(row == 1, prev[1:2], pltpu.roll(g, 2, 0)))
    c_ref[...] = g[tl - 2:tl, :]
    conv = cb_ref[...] + g2 * cw_ref[0:1, :] + g1 * cw_ref[1:2, :] + g * cw_ref[2:3, :]
    o_ref[0] = (_gelu_tanh(conv) * v_ref[0]).astype(BF16)


def _conv_act(up3, s_conv, conv_w, conv_b, *, dff):
    B, L, _ = up3.shape
    assert conv_w.shape[0] == 3 and s_conv.shape[1] == 2 and L >= 2
    tl = _pick_tile(L, 2048, SUBLANE)
    tn = _pick_tile(dff, max(LANE, (1 << 20) // tl), LANE)
    nj = dff // tn
    return pl.pallas_call(
        _conv_act_kernel,
        grid=(B, nj, L // tl),
        in_specs=[pl.BlockSpec((1, tl, tn), lambda b, j, l: (b, l, j)),
                  pl.BlockSpec((1, tl, tn), lambda b, j, l: (b, l, nj + j)),
                  pl.BlockSpec((1, 2, tn), lambda b, j, l: (b, 0, j)),
                  pl.BlockSpec((3, tn), lambda b, j, l: (0, j)),
                  pl.BlockSpec((1, tn), lambda b, j, l: (0, j))],
        out_specs=pl.BlockSpec((1, tl, tn), lambda b, j, l: (b, l, j)),
        out_shape=jax.ShapeDtypeStruct((B, L, dff), BF16),
        scratch_shapes=[pltpu.VMEM((2, tn), F32)],
        compiler_params=_cparams(("parallel", "parallel", "arbitrary")),
        name="conv_act",
    )(up3, up3, s_conv, conv_w, conv_b.reshape(1, dff))


def _up_conv_kernel(x_ref, gn_ref, wg_ref, wv_ref, s_ref, cw_ref, cb_ref, act_ref, nc_ref, h_ref, c_ref,
                    *, rows, tiles_per_seq):
    i, j = pl.program_id(0), pl.program_id(1)
    tm = x_ref.shape[0]

    @pl.when(j == 0)
    def _():
        _rms_rows(x_ref, gn_ref, h_ref, rows)

    h = h_ref[...]
    g = _dot(h, wg_ref[...])
    v = _dot(h, wv_ref[...])
    first = (i % tiles_per_seq) == 0
    prev = jnp.where(first, s_ref[0], c_ref[j])
    row = lax.broadcasted_iota(jnp.int32, g.shape, 0)
    g1 = jnp.where(row == 0, prev[1:2], pltpu.roll(g, 1, 0))
    g2 = jnp.where(row == 0, prev[0:1], jnp.where(row == 1, prev[1:2], pltpu.roll(g, 2, 0)))
    last2 = g[tm - 2:tm, :]
    c_ref[j] = last2
    nc_ref[0] = last2
    conv = cb_ref[...] + g2 * cw_ref[0:1, :] + g1 * cw_ref[1:2, :] + g * cw_ref[2:3, :]
    act_ref[...] = (_gelu_tanh(conv) * v).astype(BF16)


def _up_conv_act(x, gain, w_up, s_conv, conv_w, conv_b, *, seq_len, tm=1024, tn=256):
    M, K = x.shape
    dff = conv_b.shape[0]
    B = M // seq_len
    tm = _pick_tile(seq_len, tm, SUBLANE)
    tn = _pick_tile(dff, tn, LANE)
    nj = dff // tn
    tps = seq_len // tm
    rows = _pick_tile(tm, 64, SUBLANE)
    assert conv_w.shape[0] == 3 and s_conv.shape[1] == 2 and tm >= 2
    act, tails = pl.pallas_call(
        functools.partial(_up_conv_kernel, rows=rows, tiles_per_seq=tps),
        grid=(M // tm, nj),
        in_specs=[pl.BlockSpec((tm, K), lambda i, j: (i, 0), pipeline_mode=ROW_RESIDENT),
                  pl.BlockSpec((1, K), lambda i, j: (0, 0)),
                  pl.BlockSpec((K, tn), lambda i, j: (0, j)),
                  pl.BlockSpec((K, tn), lambda i, j: (0, nj + j)),
                  pl.BlockSpec((1, 2, tn), lambda i, j: (i // tps, 0, j)),
                  pl.BlockSpec((3, tn), lambda i, j: (0, j)),
                  pl.BlockSpec((1, tn), lambda i, j: (0, j))],
        out_specs=[pl.BlockSpec((tm, tn), lambda i, j: (i, j)),
                   pl.BlockSpec((1, 2, tn), lambda i, j: (i, 0, j))],
        out_shape=[jax.ShapeDtypeStruct((M, dff), BF16),
                   jax.ShapeDtypeStruct((M // tm, 2, dff), F32)],
        scratch_shapes=[pltpu.VMEM((tm, K), BF16), pltpu.VMEM((nj, 2, tn), F32)],
        compiler_params=_cparams(("arbitrary", "arbitrary")),
        name="up_conv_act",
    )(x, gain.reshape(1, K), w_up, w_up, s_conv, conv_w, conv_b.reshape(1, dff))
    return act, tails[tps - 1::tps]


def _pad_cols(a, width):
    return a if a.shape[-1] == width else jnp.pad(a, [(0, 0)] * (a.ndim - 1) + [(0, width - a.shape[-1])])


def _pad_rows(a, height):
    return a if a.shape[0] == height else jnp.pad(a, [(0, height - a.shape[0])] + [(0, 0)] * (a.ndim - 1))


def _layout(D, QK, GW, R, RW, WL, AL, GL):
    src, o = {}, 0
    for name, w in (("q", QK), ("k", QK), ("v", GW), ("za", R), ("zg", GW), ("r", RW), ("xw", WL),
                    ("kr", RW), ("vr", RW), ("xa", AL), ("xg", GL), ("ga", D), ("gb", D)):
        src[name] = (o, w)
        o += w
    pad = lambda w: -(-w // LANE) * LANE
    order = sorted(src, key=lambda n: -pad(src[n][1]))
    dst, o = {}, 0
    for name in order:
        dst[name] = o
        o += pad(src[name][1])
    return src, dst, order, o


def _prepare_params(lp):
    (w_in, w_alpha2, b_alpha, gla_norm, w_branch_a,
     mu_shift, w0, w_decay2, a0, w_iclr2, w_gate2, k_k, k_a, r_k, ln_x_w, ln_x_b, w_branch_b,
     w_out, g_pre_mix, g_post_mix, g_pre_ffn, g_post_ffn,
     w_up, conv_w, conv_b, w_down, g_pe, w_pe_gate, w_pe) = lp
    D = w_in.shape[0]
    R, QK = w_alpha2.shape
    GW = w_branch_a.shape[0]
    RW = w_branch_b.shape[0]
    WL, AL, GL = w_decay2.shape[0], w_iclr2.shape[0], w_gate2.shape[0]
    src, dst, order, total = _layout(D, QK, GW, R, RW, WL, AL, GL)
    n_pad = -(-total // 512) * 512
    pad = lambda w: -(-w // LANE) * LANE
    cols = [_pad_cols(w_in[:, src[n][0]:src[n][0] + src[n][1]], pad(src[n][1])) for n in order]
    if n_pad > total:
        cols.append(jnp.zeros((D, n_pad - total), w_in.dtype))
    prm = dict(
        dims=dict(D=D, R=R, QK=QK, GW=GW, RW=RW, WL=WL, AL=AL, GL=GL, DV=gla_norm.shape[0],
                  HD=r_k.shape[1], DFF=conv_b.shape[0]),
        src=src, dst=dst,
        w_in=jnp.concatenate(cols, axis=1).astype(BF16),
        w_alpha2=_pad_rows(w_alpha2, LANE).astype(BF16), b_alpha=b_alpha, gla_norm=gla_norm,
        w_branch_a=w_branch_a.astype(BF16), w_branch_b=w_branch_b.astype(BF16),
        w0=w0, w_decay2=_pad_rows(w_decay2, LANE).astype(BF16), a0=a0,
        w_iclr2=_pad_rows(w_iclr2, LANE).astype(BF16), w_gate2=w_gate2.astype(BF16),
        k_k=k_k, k_a=k_a, r_k=r_k.reshape(-1), ln_x_w=ln_x_w, ln_x_b=ln_x_b,
        w_out=w_out.astype(BF16), g_pre_mix=g_pre_mix, g_post_mix=g_post_mix,
        g_pre_ffn=g_pre_ffn, g_post_ffn=g_post_ffn,
        w_up=w_up.astype(BF16), conv_w=conv_w, conv_b=conv_b, w_down=w_down.astype(BF16),
        g_pe=g_pe, w_pe_gate=w_pe_gate.astype(BF16), w_pe=w_pe.astype(BF16),
    )
    rsrc = src["r"][0]
    rnames = ("r", "xw", "kr", "vr", "xa", "xg")
    prm["rnames"] = rnames
    prm["rsl"] = {n: (src[n][0] - rsrc, src[n][1]) for n in rnames}
    prm["mu"] = {n: _pad_cols(mu_shift[None, prm["rsl"][n][0]:prm["rsl"][n][0] + prm["rsl"][n][1]],
                              pad(prm["rsl"][n][1])) for n in rnames}
    return prm


def _pack_state(s, hd):
    B, H = s.shape[:2]
    return s.reshape(B, H // 2, 2, hd, hd).transpose(0, 1, 3, 2, 4).reshape(B, H // 2, hd, 2 * hd)


def _unpack_state(sp, hd):
    B, NP = sp.shape[:2]
    return sp.reshape(B, NP, hd, 2, hd).transpose(0, 1, 3, 2, 4).reshape(B, 2 * NP, hd, hd)


def _layer(x, p, s_gla, s_rwkv, s_shift, s_conv, prm):
    B, L, D = x.shape
    M = B * L
    dm, dst = prm["dims"], prm["dst"]
    GW, RW, DV, HD, DFF, QK = dm["GW"], dm["RW"], dm["DV"], dm["HD"], dm["DFF"], dm["QK"]
    heads = GW // DV
    pad = lambda w: -(-w // LANE) * LANE
    x2 = x.reshape(M, D)

    proj = _norm_matmul(x2, prm["g_pre_mix"], prm["w_in"])
    proj3 = proj.reshape(B, L, -1)

    o_a, s_gla_new = _gla(proj3, dst, prm["w_alpha2"], prm["b_alpha"], prm["gla_norm"], s_gla,
                          dk=QK // heads, dv=DV, heads=heads)

    shift_parts = {n: _pad_cols(s_shift[:, None, prm["rsl"][n][0]:prm["rsl"][n][0] + prm["rsl"][n][1]],
                                pad(prm["rsl"][n][1])) for n in prm["rnames"]}
    r, lw, kr, vr, a, g = _rwkv_prep(proj3, dst, shift_parts, prm["mu"], prm["w0"], prm["w_decay2"],
                                     prm["a0"], prm["w_iclr2"], prm["w_gate2"], rw=RW, gl=dm["GL"])
    o_b, s_rwkv_new = _rwkv_chunk(r, lw, kr, vr, a, g, prm["k_k"], prm["k_a"], prm["r_k"],
                                  prm["ln_x_w"], prm["ln_x_b"], _pack_state(s_rwkv, HD), hd=HD)
    new_shift = jnp.concatenate(
        [proj3[:, L - 1, dst[n]:dst[n] + prm["rsl"][n][1]] for n in prm["rnames"]], axis=-1)

    mixed = _merge(o_a.reshape(M, GW), o_b.reshape(M, RW), prm["w_branch_a"], prm["w_branch_b"],
                   proj, dst["ga"], dst["gb"])
    x2 = _matmul_norm_residual(mixed, prm["w_out"], x2, prm["g_post_mix"], tm=512, tn=512)

    if L >= 256:
        act, new_conv = _up_conv_act(x2, prm["g_pre_ffn"], prm["w_up"], s_conv, prm["conv_w"],
                                     prm["conv_b"], seq_len=L)
    else:
        up = _norm_matmul(x2, prm["g_pre_ffn"], prm["w_up"])
        up3 = up.reshape(B, L, 2 * DFF)
        act = _conv_act(up3, s_conv, prm["conv_w"], prm["conv_b"], dff=DFF)
        new_conv = up3[:, L - 2:, :DFF]
    x2 = _matmul_norm_residual(act.reshape(M, DFF), prm["w_down"], x2, prm["g_post_ffn"], tm=512, tn=256)

    x2 = _pe_layer(x2, prm["g_pe"], prm["w_pe_gate"], p.reshape(M, -1), prm["w_pe"])
    return x2.reshape(B, L, D), s_gla_new, _unpack_state(s_rwkv_new, HD), new_shift, new_conv


def kernel(x_prompt, x_sample, state_gla, state_rwkv, state_shift, state_ffn_conv, p_prompt, p_sample, w_in, w_alpha2, b_alpha, gla_norm, w_branch_a, mu_shift, w0, w_decay2, a0, w_iclr2, w_gate2, k_k, k_a, r_k, ln_x_w, ln_x_b, w_branch_b, w_out, g_pre_mix, g_post_mix, g_pre_ffn, g_post_ffn, w_up, conv_w, conv_b, w_down, g_pe, w_pe_gate, w_pe):
    params = (w_in, w_alpha2, b_alpha, gla_norm, w_branch_a,
              mu_shift, w0, w_decay2, a0, w_iclr2, w_gate2, k_k, k_a, r_k, ln_x_w, ln_x_b, w_branch_b,
              w_out, g_pre_mix, g_post_mix, g_pre_ffn, g_post_ffn,
              w_up, conv_w, conv_b, w_down, g_pe, w_pe_gate, w_pe)
    depth = w_in.shape[0]
    nb = x_prompt.shape[0]
    yp, ys = x_prompt, x_sample
    outs_p = [[], [], [], []]
    outs_s = [[], [], [], []]
    for i in range(depth):
        prm = _prepare_params(tuple(t[i] for t in params))
        z_gla = jnp.zeros((nb,) + state_gla.shape[2:], F32)
        z_rwkv = jnp.zeros((nb,) + state_rwkv.shape[2:], F32)
        z_shift = jnp.zeros((nb,) + state_shift.shape[2:], x_prompt.dtype)
        z_conv = jnp.zeros((nb,) + state_ffn_conv.shape[2:], x_prompt.dtype)
        yp, *st = _layer(yp, p_prompt[i], z_gla, z_rwkv, z_shift, z_conv, prm)
        for acc, s in zip(outs_p, st):
            acc.append(s)
        ys, *st = _layer(ys, p_sample[i], state_gla[i], state_rwkv[i], state_shift[i],
                         state_ffn_conv[i], prm)
        for acc, s in zip(outs_s, st):
            acc.append(s)
    return (yp, ys, *(jnp.stack(a) for a in outs_p), *(jnp.stack(a) for a in outs_s))
```

```python
import functools
import math

import jax
import jax.numpy as jnp
from jax import lax
from jax.experimental import pallas as pl
from jax.experimental.pallas import tpu as pltpu

F32 = jnp.float32
BF16 = jnp.bfloat16

LANE = 128
SUBLANE = 8
VMEM_LIMIT_BYTES = 56 * 1024 * 1024

EPS = 1e-6
GN_EPS = 64e-5
GLA_TAU = 16.0
GLA_CHUNK = 64
RWKV_CHUNK = 64
KK_EPS = 1e-12
DECAY_SCALE = math.exp(-0.5)
GELU_C = math.sqrt(2.0 / math.pi)


ROW_RESIDENT = pl.Buffered(1)


def _cparams(sem):
    return pltpu.CompilerParams(dimension_semantics=sem, vmem_limit_bytes=VMEM_LIMIT_BYTES)


def _sigmoid(x):
    return 1.0 / (1.0 + jnp.exp(-x))


def _log_sigmoid(x):
    return jnp.minimum(x, 0.0) - jnp.log(1.0 + jnp.exp(-jnp.abs(x)))


def _gelu_tanh(x):
    return 0.5 * x * (1.0 + jnp.tanh(GELU_C * (x + 0.044715 * (x * x * x))))


def _split2(x):
    hi = x.astype(BF16)
    lo = (x - hi.astype(F32)).astype(BF16)
    return hi, lo


def _dot(a, b):
    return jnp.dot(a, b, preferred_element_type=F32)


def _dot_nt(a, b):
    return lax.dot_general(a, b, (((1,), (1,)), ((), ())), preferred_element_type=F32)


def _dot_tn(a, b):
    return lax.dot_general(a, b, (((0,), (0,)), ((), ())), preferred_element_type=F32)


def _ones_dot(m01, x):
    hi, lo = _split2(x)
    return _dot(m01, hi) + _dot(m01, lo)


def _dot_ones(x, m01):
    hi, lo = _split2(x)
    return _dot(hi, m01) + _dot(lo, m01)


def _lockstep(gens):
    gens = list(gens)
    while gens:
        alive = []
        for g in gens:
            try:
                next(g)
                alive.append(g)
            except StopIteration:
                pass
        gens = alive


def _tri_incl(n):
    r = lax.broadcasted_iota(jnp.int32, (n, n), 0)
    c = lax.broadcasted_iota(jnp.int32, (n, n), 1)
    return r >= c


def _pick_tile(n, target, mult):
    if n <= target:
        return n
    best = None
    t = mult
    while t <= target:
        if n % t == 0:
            best = t
        t += mult
    assert best is not None, (n, target, mult)
    return best


def _rms_rows(x_ref, g_ref, h_ref, rows):
    tm = x_ref.shape[0]

    def body(i, _):
        sl = pl.ds(pl.multiple_of(i * rows, rows), rows)
        x = x_ref[sl, :]
        ms = jnp.mean(x * x, axis=-1, keepdims=True)
        h_ref[sl, :] = (x * lax.rsqrt(ms + EPS) * g_ref[...]).astype(BF16)
        return 0

    lax.fori_loop(0, tm // rows, body, 0)


def _norm_mm_kernel(x_ref, g_ref, w_ref, o_ref, h_ref, *, rows):
    @pl.when(pl.program_id(1) == 0)
    def _():
        _rms_rows(x_ref, g_ref, h_ref, rows)

    o_ref[...] = _dot(h_ref[...], w_ref[...].astype(BF16))


def _window(src, width):
    base = src // LANE * LANE
    shift = src - base
    wp = -(-width // LANE) * LANE
    return base, shift, (wp if shift == 0 else wp + LANE), wp


def _norm_matmul(x, gain, w, *, tm=1024, tn=512):
    M, K = x.shape
    tm = _pick_tile(M, tm, SUBLANE)
    tn = min(tn, -(-w.shape[1] // LANE) * LANE)
    N = -(-w.shape[1] // tn) * tn
    rows = _pick_tile(tm, 64, SUBLANE)
    return pl.pallas_call(
        functools.partial(_norm_mm_kernel, rows=rows),
        grid=(M // tm, N // tn),
        in_specs=[pl.BlockSpec((tm, K), lambda i, j: (i, 0), pipeline_mode=ROW_RESIDENT),
                  pl.BlockSpec((1, K), lambda i, j: (0, 0)),
                  pl.BlockSpec((K, tn), lambda i, j: (0, j))],
        out_specs=pl.BlockSpec((tm, tn), lambda i, j: (i, j)),
        out_shape=jax.ShapeDtypeStruct((M, N), F32),
        scratch_shapes=[pltpu.VMEM((tm, K), BF16)],
        compiler_params=_cparams(("parallel", "arbitrary")),
        name="norm_matmul",
    )(x, gain.reshape(1, K), w)


def _pe_kernel(x_ref, g_ref, w_ref, p_ref, wp_ref, o_ref, h_ref, *, rows, tn):
    j = pl.program_id(1)

    @pl.when(j == 0)
    def _():
        _rms_rows(x_ref, g_ref, h_ref, rows)

    gate = _sigmoid(_dot(h_ref[...], w_ref[...]))
    pe = _dot(p_ref[...].astype(BF16), wp_ref[...])
    xs = x_ref[:, pl.ds(pl.multiple_of(j * tn, tn), tn)]
    o_ref[...] = xs + gate * pe


def _pe_layer(x, gain, w_gate, p, w_pe, *, tm=1024, tn=512):
    M, K = x.shape
    N = w_gate.shape[1]
    P = p.shape[1]
    tm = _pick_tile(M, tm, SUBLANE)
    tn = _pick_tile(N, tn, LANE)
    rows = _pick_tile(tm, 64, SUBLANE)
    return pl.pallas_call(
        functools.partial(_pe_kernel, rows=rows, tn=tn),
        grid=(M // tm, N // tn),
        in_specs=[pl.BlockSpec((tm, K), lambda i, j: (i, 0), pipeline_mode=ROW_RESIDENT),
                  pl.BlockSpec((1, K), lambda i, j: (0, 0)),
                  pl.BlockSpec((K, tn), lambda i, j: (0, j)),
                  pl.BlockSpec((tm, P), lambda i, j: (i, 0)),
                  pl.BlockSpec((P, tn), lambda i, j: (0, j))],
        out_specs=pl.BlockSpec((tm, tn), lambda i, j: (i, j)),
        out_shape=jax.ShapeDtypeStruct((M, N), F32),
        scratch_shapes=[pltpu.VMEM((tm, K), BF16)],
        compiler_params=_cparams(("parallel", "arbitrary")),
        name="pe_layer",
    )(x, gain.reshape(1, K), w_gate, p, w_pe)


def _mm_norm_res_kernel(a_ref, w_ref, x_ref, g_ref, o_ref, *, tn, rows):
    j = pl.program_id(1)
    o_ref[:, pl.ds(pl.multiple_of(j * tn, tn), tn)] = _dot(a_ref[...], w_ref[...])

    @pl.when(j == pl.num_programs(1) - 1)
    def _():
        tm = o_ref.shape[0]

        def body(i, _):
            sl = pl.ds(pl.multiple_of(i * rows, rows), rows)
            y = o_ref[sl, :]
            ms = jnp.mean(y * y, axis=-1, keepdims=True)
            o_ref[sl, :] = x_ref[sl, :] + y * lax.rsqrt(ms + EPS) * g_ref[...]
            return 0

        lax.fori_loop(0, tm // rows, body, 0)


def _matmul_norm_residual(a, w, x, gain, *, tm, tn):
    M, K = a.shape
    N = w.shape[1]
    tm = _pick_tile(M, tm, SUBLANE)
    tn = _pick_tile(N, tn, LANE)
    rows = _pick_tile(tm, 64, SUBLANE)
    return pl.pallas_call(
        functools.partial(_mm_norm_res_kernel, tn=tn, rows=rows),
        grid=(M // tm, N // tn),
        in_specs=[pl.BlockSpec((tm, K), lambda i, j: (i, 0), pipeline_mode=ROW_RESIDENT),
                  pl.BlockSpec((K, tn), lambda i, j: (0, j)),
                  pl.BlockSpec((tm, N), lambda i, j: (i, 0), pipeline_mode=ROW_RESIDENT),
                  pl.BlockSpec((1, N), lambda i, j: (0, 0))],
        out_specs=pl.BlockSpec((tm, N), lambda i, j: (i, 0)),
        out_shape=jax.ShapeDtypeStruct((M, N), F32),
        compiler_params=_cparams(("parallel", "arbitrary")),
        name="matmul_norm_residual",
    )(a, w, x, gain.reshape(1, N))


def _merge_kernel(oa_ref, ob_ref, wa_ref, wb_ref, ga_ref, gb_ref, o_ref, *, sa, sb):
    tn = o_ref.shape[1]
    ya = _dot(oa_ref[...], wa_ref[...])
    yb = _dot(ob_ref[...], wb_ref[...])
    ga = ga_ref[:, sa:sa + tn]
    gb = gb_ref[:, sb:sb + tn]
    o_ref[...] = (_sigmoid(ga) * ya + _sigmoid(gb) * yb).astype(BF16)


def _gate_window_spec(tm, tn, off, n_cols):
    base, shift, _, _ = _window(off, tn)
    ww = tn if shift == 0 else tn + LANE
    spec = pl.BlockSpec((pl.Element(tm), pl.Element(ww)),
                        lambda i, j: (pl.multiple_of(i * tm, SUBLANE), pl.multiple_of(base + j * tn, LANE)))
    return spec, shift, base + n_cols - tn + ww


def _merge(oa, ob, wa, wb, proj, ga_off, gb_off, *, tm=512, tn=1024):
    M, KA = oa.shape
    KB = ob.shape[1]
    N = wa.shape[1]
    tm = _pick_tile(M, tm, SUBLANE)
    tn = _pick_tile(N, tn, LANE)
    ga_spec, sa, enda = _gate_window_spec(tm, tn, ga_off, N)
    gb_spec, sb, endb = _gate_window_spec(tm, tn, gb_off, N)
    assert max(enda, endb) <= proj.shape[1]
    return pl.pallas_call(
        functools.partial(_merge_kernel, sa=sa, sb=sb),
        grid=(M // tm, N // tn),
        in_specs=[pl.BlockSpec((tm, KA), lambda i, j: (i, 0)),
                  pl.BlockSpec((tm, KB), lambda i, j: (i, 0)),
                  pl.BlockSpec((KA, tn), lambda i, j: (0, j)),
                  pl.BlockSpec((KB, tn), lambda i, j: (0, j)),
                  ga_spec, gb_spec],
        out_specs=pl.BlockSpec((tm, tn), lambda i, j: (i, j)),
        out_shape=jax.ShapeDtypeStruct((M, N), BF16),
        compiler_params=_cparams(("parallel", "arbitrary")),
        name="merge",
    )(oa, ob, wa, wb, proj, proj)


def _gla_kernel(q_ref, k_ref, v_ref, za_ref, zg_ref, wa_ref, ba_ref, gn_ref, s0_ref,
                o_ref, sn_ref, s_ref, *, scale, c, heads, zshift):
    l = pl.program_id(1)

    @pl.when(l == 0)
    def _():
        s_ref[...] = s0_ref[0]

    dk = q_ref.shape[2] // heads
    dv = v_ref.shape[2] // heads
    x = _dot(za_ref[0].astype(BF16), wa_ref[...]) + ba_ref[...]
    la_all = _log_sigmoid(x) * (1.0 / GLA_TAU)
    tri = _tri_incl(c)
    b_all = _ones_dot(jnp.where(tri, 1.0, 0.0).astype(BF16), la_all)
    ones = jnp.ones((c, LANE), BF16)
    def head(h):
        ks = slice(h * dk, (h + 1) * dk)
        vs = slice(h * dv, (h + 1) * dv)
        la = la_all[:, ks]
        b = b_all[:, ks]
        b_end = b[c - 1:c, :]
        q = q_ref[0, :, ks] * scale
        k = k_ref[0, :, ks]
        v = v_ref[0, :, vs].astype(BF16)
        qd = (q * jnp.exp(b)).astype(BF16)
        kd = (k * jnp.exp(-b)).astype(BF16)
        s = s_ref[h]
        att = _dot_nt(qd, kd)
        o = _dot(qd, s.astype(BF16))
        k_end = (k * jnp.exp(b_end - b)).astype(BF16)
        la_hi, la_lo = _split2(la)
        dec = jnp.exp(_dot_tn(la_hi, ones) + _dot_tn(la_lo, ones))
        dec = jnp.concatenate([dec] * (dv // LANE), axis=1)
        s_ref[h] = s * dec + _dot_tn(k_end, v)
        yield
        o = o + _dot(jnp.where(tri, att, 0.0).astype(BF16), v)
        yield
        ms = jnp.mean(o * o, axis=-1, keepdims=True)
        on = o * lax.rsqrt(ms + EPS) * gn_ref[...]
        zg = zg_ref[0, :, zshift + h * dv:zshift + (h + 1) * dv]
        o_ref[0, :, vs] = (on * (zg * _sigmoid(zg))).astype(BF16)

    _lockstep([head(h) for h in range(heads)])

    @pl.when(l == pl.num_programs(1) - 1)
    def _():
        sn_ref[0] = s_ref[...]


def _gla(proj3, offs, w_alpha2p, b_alpha, gla_norm, s0, *, dk, dv, heads):
    B, L, _ = proj3.shape
    c = min(GLA_CHUNK, L)
    qk, gw = heads * dk, heads * dv
    assert L % c == 0 and dv % LANE == 0
    for name, wdt in (("q", qk), ("k", qk), ("v", gw), ("za", LANE)):
        assert offs[name] % wdt == 0
    jq, jk, jv, jz = offs["q"] // qk, offs["k"] // qk, offs["v"] // gw, offs["za"] // LANE
    zbase, zshift, zww, _ = _window(offs["zg"], gw)
    assert zbase + zww <= proj3.shape[2]
    return pl.pallas_call(
        functools.partial(_gla_kernel, scale=dk ** -0.5, c=c, heads=heads, zshift=zshift),
        grid=(B, L // c),
        in_specs=[pl.BlockSpec((1, c, qk), lambda b, l: (b, l, jq)),
                  pl.BlockSpec((1, c, qk), lambda b, l: (b, l, jk)),
                  pl.BlockSpec((1, c, gw), lambda b, l: (b, l, jv)),
                  pl.BlockSpec((1, c, LANE), lambda b, l: (b, l, jz)),
                  pl.BlockSpec((pl.Element(1), pl.Element(c), pl.Element(zww)),
                               lambda b, l: (b, pl.multiple_of(l * c, c), zbase)),
                  pl.BlockSpec((LANE, qk), lambda b, l: (0, 0)),
                  pl.BlockSpec((1, qk), lambda b, l: (0, 0)),
                  pl.BlockSpec((1, dv), lambda b, l: (0, 0)),
                  pl.BlockSpec((1, heads, dk, dv), lambda b, l: (b, 0, 0, 0))],
        out_specs=[pl.BlockSpec((1, c, gw), lambda b, l: (b, l, 0)),
                   pl.BlockSpec((1, heads, dk, dv), lambda b, l: (b, 0, 0, 0))],
        out_shape=[jax.ShapeDtypeStruct((B, L, gw), BF16),
                   jax.ShapeDtypeStruct((B, heads, dk, dv), F32)],
        scratch_shapes=[pltpu.VMEM((heads, dk, dv), F32)],
        compiler_params=_cparams(("parallel", "arbitrary")),
        name="gla",
    )(proj3, proj3, proj3, proj3, proj3, w_alpha2p, b_alpha.reshape(1, -1), gla_norm.reshape(1, dv), s0)


def _shifted(cur, first_row):
    rolled = pltpu.roll(cur, 1, 0)
    row = lax.broadcasted_iota(jnp.int32, cur.shape, 0)
    return jnp.where(row == 0, first_row, rolled)


def _rwkv_prep_kernel(r_ref, xw_ref, kr_ref, vr_ref, xa_ref, xg_ref,
                      sr_ref, sxw_ref, skr_ref, svr_ref, sxa_ref, sxg_ref,
                      mr_ref, mxw_ref, mkr_ref, mvr_ref, mxa_ref, mxg_ref,
                      w0_ref, wd_ref, a0_ref, wi_ref, wg_ref,
                      ro_ref, lw_ref, ko_ref, vo_ref, ao_ref, go_ref,
                      cr_ref, cxw_ref, ckr_ref, cvr_ref, cxa_ref, cxg_ref, *, shifts):
    l = pl.program_id(1)
    tl = r_ref.shape[1]
    sh_r, sh_xw, sh_kr, sh_vr, sh_xa, sh_xg = shifts

    def shift_mix(z_ref, s0, s_ref, m_ref, c_ref):
        cur = z_ref[0, :, s0:s0 + m_ref.shape[1]]
        first = jnp.where(l == 0, s_ref[0], c_ref[...])
        prev = _shifted(cur, first)
        c_ref[...] = cur[tl - 1:tl, :]
        return cur + (prev - cur) * m_ref[...]

    ro_ref[0] = shift_mix(r_ref, sh_r, sr_ref, mr_ref, cr_ref)
    ko_ref[0] = shift_mix(kr_ref, sh_kr, skr_ref, mkr_ref, ckr_ref)
    vo_ref[0] = shift_mix(vr_ref, sh_vr, svr_ref, mvr_ref, cvr_ref)
    xw = shift_mix(xw_ref, sh_xw, sxw_ref, mxw_ref, cxw_ref)
    xa = shift_mix(xa_ref, sh_xa, sxa_ref, mxa_ref, cxa_ref)
    xg = shift_mix(xg_ref, sh_xg, sxg_ref, mxg_ref, cxg_ref)
    z = w0_ref[...] + _dot(jnp.tanh(xw).astype(BF16), wd_ref[...])
    lw_ref[0] = -DECAY_SCALE * _sigmoid(z)
    ao_ref[0] = _sigmoid(a0_ref[...] + _dot(xa.astype(BF16), wi_ref[...]))
    go_ref[0] = _dot(_sigmoid(xg).astype(BF16), wg_ref[...])


def _rwkv_prep(proj3, offs, shift_parts, mu_parts, w0, w_decay2p, a0, w_iclr2p, w_gate2, *, rw, gl):
    B, L, _ = proj3.shape
    tl = _pick_tile(L, 256, SUBLANE)
    names = ("r", "xw", "kr", "vr", "xa", "xg")
    widths = {"r": rw, "xw": LANE, "kr": rw, "vr": rw, "xa": LANE, "xg": gl}
    in_specs, args, shifts = [], [], []
    for n in names:
        base, shift, ww, wp = _window(offs[n], widths[n])
        assert wp == widths[n] and base + ww <= proj3.shape[2]
        shifts.append(shift)
        in_specs.append(pl.BlockSpec(
            (pl.Element(1), pl.Element(tl), pl.Element(ww)),
            functools.partial(lambda b, l, base: (b, pl.multiple_of(l * tl, SUBLANE), base), base=base)))
        args.append(proj3)
    for n in names:
        in_specs.append(pl.BlockSpec((1, 1, widths[n]), lambda b, l: (b, 0, 0)))
        args.append(shift_parts[n])
    for n in names:
        in_specs.append(pl.BlockSpec((1, widths[n]), lambda b, l: (0, 0)))
        args.append(mu_parts[n])
    for arr in (w0.reshape(1, rw), w_decay2p, a0.reshape(1, rw), w_iclr2p, w_gate2):
        in_specs.append(pl.BlockSpec(arr.shape, lambda b, l: (0, 0)))
        args.append(arr)
    out_spec = pl.BlockSpec((1, tl, rw), lambda b, l: (b, l, 0))
    out_sd = jax.ShapeDtypeStruct((B, L, rw), F32)
    return pl.pallas_call(
        functools.partial(_rwkv_prep_kernel, shifts=tuple(shifts)),
        grid=(B, L // tl),
        in_specs=in_specs,
        out_specs=[out_spec] * 6,
        out_shape=[out_sd] * 6,
        scratch_shapes=[pltpu.VMEM((1, widths[n]), F32) for n in names],
        compiler_params=_cparams(("parallel", "arbitrary")),
        name="rwkv_prep",
    )(*args)


def _rwkv_chunk_kernel(r_ref, lw_ref, kr_ref, v_ref, a_ref, g_ref,
                       kk_w_ref, ka_w_ref, rk_w_ref, lnw_ref, lnb_ref, s0_ref,
                       o_ref, sn_ref, w_ref, *, c, hd, gp):
    l = pl.program_id(2)
    tl = r_ref.shape[1]
    n2 = 2 * c
    lane_c = lax.broadcasted_iota(jnp.int32, (c, LANE), 1)
    head0_c = lane_c < hd
    row2 = lax.broadcasted_iota(jnp.int32, (n2, n2), 0)
    col2 = lax.broadcasted_iota(jnp.int32, (n2, n2), 1)
    same = (2 * row2 + 1 - n2) * (2 * col2 + 1 - n2) > 0
    strict = jnp.logical_and(same, row2 > col2)
    incl = jnp.logical_and(same, row2 >= col2)
    eye = jnp.where(row2 == col2, 1.0, 0.0)
    tri = jnp.where(_tri_incl(c), 1.0, 0.0).astype(BF16)
    lr = lax.broadcasted_iota(jnp.int32, (LANE, LANE), 0)
    lc = lax.broadcasted_iota(jnp.int32, (LANE, LANE), 1)
    seg = jnp.where((2 * lr + 1 - LANE) * (2 * lc + 1 - LANE) > 0, 1.0, 0.0).astype(BF16)

    @pl.when(l == 0)
    def _():
        lane_s = lax.broadcasted_iota(jnp.int32, (hd, LANE), 1)
        for gi in range(gp):
            s0 = s0_ref[0, gi]
            w_ref[gi] = jnp.concatenate([jnp.where(lane_s < hd, s0, 0.0),
                                         jnp.where(lane_s < hd, 0.0, s0)], axis=0)

    def stack(x):
        return jnp.concatenate([jnp.where(head0_c, x, 0.0), jnp.where(head0_c, 0.0, x)], axis=0)

    def unstack(x):
        return x[0:c] + x[c:n2]

    def chunk_pair(sl, gi):
        ls = slice(gi * LANE, (gi + 1) * LANE)
        r = r_ref[0, sl, ls]
        lw = lw_ref[0, sl, ls]
        kr = kr_ref[0, sl, ls]
        v = v_ref[0, sl, ls]
        a = a_ref[0, sl, ls]
        kk = kr * kk_w_ref[:, ls]
        kk_ss = _dot_ones(kk * kk, seg)
        cum = _ones_dot(tri, lw)
        yield
        kk = kk / jnp.maximum(jnp.sqrt(kk_ss), KK_EPS)
        k = kr * (1.0 + (a - 1.0) * ka_w_ref[:, ls])
        cum_end = cum[c - 1:c, :]
        e_neg = jnp.exp(-cum)
        e_rem = jnp.exp(cum_end - cum)
        al = kk * a
        al_t = stack(al * e_neg).astype(BF16)
        k_t = stack(k * e_neg).astype(BF16)
        be_t = stack(-kk * jnp.exp(cum - lw)).astype(BF16)
        r_t = stack(r * jnp.exp(cum)).astype(BF16)
        v_s = stack(v).astype(BF16)
        lhs = jnp.concatenate([be_t, r_t], axis=0)
        sc_a = _dot_nt(lhs, al_t)
        sc_k = _dot_nt(lhs, k_t)
        w = w_ref[gi]
        rd = _dot_nt(lhs, w.astype(BF16))
        yield
        l_a =jnp.where(strict, sc_a[0:n2], 0.0)
        l_k = jnp.where(strict, sc_k[0:n2], 0.0).astype(BF16)
        m_a = jnp.where(incl, sc_a[n2:2 * n2], 0.0).astype(BF16)
        m_k = jnp.where(incl, sc_k[n2:2 * n2], 0.0).astype(BF16)
        t_inv = eye + l_a
        lp = l_a
        rhs_u = rd[0:n2] + _dot(l_k, v_s)
        o_s = rd[n2:2 * n2] + _dot(m_k, v_s)
        bonus = _dot_ones(r * k * rk_w_ref[:, ls], seg) * v
        span = 1
        while 2 * span < c:
            lpb = lp.astype(BF16)
            lp = _dot(lpb, lpb)
            yield
            t_inv = t_inv + _dot(t_inv.astype(BF16), lp.astype(BF16))
            yield
            span *= 2
        u_s = _dot(t_inv.astype(BF16), rhs_u.astype(BF16))
        yield
        u_b = u_s.astype(BF16)
        o_s = o_s + _dot(m_a, u_b)
        uv = jnp.concatenate([u_b, v_s], axis=0)
        ak = jnp.concatenate([stack(al * e_rem), stack(k * e_rem)], axis=0).astype(BF16)
        w_ref[gi] = w * jnp.exp(cum_end) + _dot_tn(uv, ak)
        yield
        o = unstack(o_s)
        mu = _dot_ones(o, seg) * (1.0 / hd)
        yield
        d = o - mu
        var = _dot_ones(d * d, seg) * (1.0 / hd)
        yield
        on = d * lax.rsqrt(var + GN_EPS) * lnw_ref[:, ls] + lnb_ref[:, ls]
        o_ref[0, sl, ls] = ((on + bonus) * g_ref[0, sl, ls]).astype(BF16)

    def chunk(ci, _):
        sl = pl.ds(pl.multiple_of(ci * c, c), c)
        _lockstep([chunk_pair(sl, gi) for gi in range(gp)])
        return 0

    lax.fori_loop(0, tl // c, chunk, 0)

    @pl.when(l == pl.num_programs(2) - 1)
    def _():
        for gi in range(gp):
            wf = w_ref[gi]
            sn_ref[0, gi] = wf[0:hd] + wf[hd:2 * hd]


def _rwkv_chunk(r, lw, kr, vr, a, g, k_k, k_a, r_k, ln_w, ln_b, s0p, *, hd):
    B, L, RW = r.shape
    assert 2 * hd == LANE and RW % LANE == 0
    c = min(RWKV_CHUNK, L)
    tl = _pick_tile(L, 512, c)
    assert L % c == 0 and tl % c == 0 and c % SUBLANE == 0
    npair = RW // LANE
    gp = _pick_tile(npair, 8 if L > c else 16, 1)
    seq = pl.BlockSpec((1, tl, gp * LANE), lambda b, p, l: (b, l, p))
    par = pl.BlockSpec((1, gp * LANE), lambda b, p, l: (0, p))
    st = pl.BlockSpec((1, gp, hd, LANE), lambda b, p, l: (b, p, 0, 0))
    return pl.pallas_call(
        functools.partial(_rwkv_chunk_kernel, c=c, hd=hd, gp=gp),
        grid=(B, npair // gp, L // tl),
        in_specs=[seq] * 6 + [par] * 5 + [st],
        out_specs=[seq, st],
        out_shape=[jax.ShapeDtypeStruct((B, L, RW), BF16),
                   jax.ShapeDtypeStruct((B, npair, hd, LANE), F32)],
        scratch_shapes=[pltpu.VMEM((gp, LANE, LANE), F32)],
        compiler_params=_cparams(("parallel", "parallel", "arbitrary")),
        name="rwkv_chunk",
    )(r, lw, kr, vr, a, g, k_k.reshape(1, RW), k_a.reshape(1, RW), r_k.reshape(1, RW),
      ln_w.reshape(1, RW), ln_b.reshape(1, RW), s0p)


def _conv_act_kernel(g_ref, v_ref, s_ref, cw_ref, cb_ref, o_ref, c_ref):
    l = pl.program_id(2)
    tl = g_ref.shape[1]
    g = g_ref[0]
    prev = jnp.where(l == 0, s_ref[0], c_ref[...])
    row = lax.broadcasted_iota(jnp.int32, g.shape, 0)
    g1 = jnp.where(row == 0, prev[1:2], pltpu.roll(g, 1, 0))
    g2 = jnp.where(row == 0, prev[0:1], jnp.where(row == 1, prev[1:2], pltpu.roll(g, 2, 0)))
    c_ref[...] = g[tl - 2:tl, :]
    conv = cb_ref[...] + g2 * cw_ref[0:1, :] + g1 * cw_ref[1:2, :] + g * cw_ref[2:3, :]
    o_ref[0] = (_gelu_tanh(conv) * v_ref[0]).astype(BF16)


def _conv_act(up3, s_conv, conv_w, conv_b, *, dff):
    B, L, _ = up3.shape
    assert conv_w.shape[0] == 3 and s_conv.shape[1] == 2 and L >= 2
    tl = _pick_tile(L, 2048, SUBLANE)
    tn = _pick_tile(dff, max(LANE, (1 << 20) // tl), LANE)
    nj = dff // tn
    return pl.pallas_call(
        _conv_act_kernel,
        grid=(B, nj, L // tl),
        in_specs=[pl.BlockSpec((1, tl, tn), lambda b, j, l: (b, l, j)),
                  pl.BlockSpec((1, tl, tn), lambda b, j, l: (b, l, nj + j)),
                  pl.BlockSpec((1, 2, tn), lambda b, j, l: (b, 0, j)),
                  pl.BlockSpec((3, tn), lambda b, j, l: (0, j)),
                  pl.BlockSpec((1, tn), lambda b, j, l: (0, j))],
        out_specs=pl.BlockSpec((1, tl, tn), lambda b, j, l: (b, l, j)),
        out_shape=jax.ShapeDtypeStruct((B, L, dff), BF16),
        scratch_shapes=[pltpu.VMEM((2, tn), F32)],
        compiler_params=_cparams(("parallel", "parallel", "arbitrary")),
        name="conv_act",
    )(up3, up3, s_conv, conv_w, conv_b.reshape(1, dff))


def _up_conv_kernel(x_ref, gn_ref, wg_ref, wv_ref, s_ref, cw_ref, cb_ref, act_ref, nc_ref, h_ref, c_ref,
                    *, rows, tiles_per_seq):
    i, j = pl.program_id(0), pl.program_id(1)
    tm = x_ref.shape[0]

    @pl.when(j == 0)
    def _():
        _rms_rows(x_ref, gn_ref, h_ref, rows)

    h = h_ref[...]
    g = _dot(h, wg_ref[...].astype(BF16))
    v = _dot(h, wv_ref[...].astype(BF16))
    first = (i % tiles_per_seq) == 0
    prev = jnp.where(first, s_ref[0], c_ref[j])
    row = lax.broadcasted_iota(jnp.int32, g.shape, 0)
    g1 = jnp.where(row == 0, prev[1:2], pltpu.roll(g, 1, 0))
    g2 = jnp.where(row == 0, prev[0:1], jnp.where(row == 1, prev[1:2], pltpu.roll(g, 2, 0)))
    last2 = g[tm - 2:tm, :]
    c_ref[j] = last2
    nc_ref[0] = last2
    conv = cb_ref[...] + g2 * cw_ref[0:1, :] + g1 * cw_ref[1:2, :] + g * cw_ref[2:3, :]
    act_ref[...] = (_gelu_tanh(conv) * v).astype(BF16)


def _up_conv_act(x, gain, w_up, s_conv, conv_w, conv_b, *, seq_len, tm=1024, tn=256):
    M, K = x.shape
    dff = conv_b.shape[0]
    B = M // seq_len
    tm = _pick_tile(seq_len, tm, SUBLANE)
    tn = _pick_tile(dff, tn, LANE)
    nj = dff // tn
    tps = seq_len // tm
    rows = _pick_tile(tm, 64, SUBLANE)
    assert conv_w.shape[0] == 3 and s_conv.shape[1] == 2 and tm >= 2
    act, tails = pl.pallas_call(
        functools.partial(_up_conv_kernel, rows=rows, tiles_per_seq=tps),
        grid=(M // tm, nj),
        in_specs=[pl.BlockSpec((tm, K), lambda i, j: (i, 0), pipeline_mode=ROW_RESIDENT),
                  pl.BlockSpec((1, K), lambda i, j: (0, 0)),
                  pl.BlockSpec((K, tn), lambda i, j: (0, j)),
                  pl.BlockSpec((K, tn), lambda i, j: (0, nj + j)),
                  pl.BlockSpec((1, 2, tn), lambda i, j: (i // tps, 0, j)),
                  pl.BlockSpec((3, tn), lambda i, j: (0, j)),
                  pl.BlockSpec((1, tn), lambda i, j: (0, j))],
        out_specs=[pl.BlockSpec((tm, tn), lambda i, j: (i, j)),
                   pl.BlockSpec((1, 2, tn), lambda i, j: (i, 0, j))],
        out_shape=[jax.ShapeDtypeStruct((M, dff), BF16),
                   jax.ShapeDtypeStruct((M // tm, 2, dff), F32)],
        scratch_shapes=[pltpu.VMEM((tm, K), BF16), pltpu.VMEM((nj, 2, tn), F32)],
        compiler_params=_cparams(("arbitrary", "arbitrary")),
        name="up_conv_act",
    )(x, gain.reshape(1, K), w_up, w_up, s_conv, conv_w, conv_b.reshape(1, dff))
    return act, tails[tps - 1::tps]


def _pad_cols(a, width):
    return a if a.shape[-1] == width else jnp.pad(a, [(0, 0)] * (a.ndim - 1) + [(0, width - a.shape[-1])])


def _pad_rows(a, height):
    return a if a.shape[0] == height else jnp.pad(a, [(0, height - a.shape[0])] + [(0, 0)] * (a.ndim - 1))


def _layout(D, QK, GW, R, RW, WL, AL, GL):
    src, o = {}, 0
    for name, w in (("q", QK), ("k", QK), ("v", GW), ("za", R), ("zg", GW), ("r", RW), ("xw", WL),
                    ("kr", RW), ("vr", RW), ("xa", AL), ("xg", GL), ("ga", D), ("gb", D)):
        src[name] = (o, w)
        o += w
    return src


def _prepare_params(lp):
    (w_in, w_alpha2, b_alpha, gla_norm, w_branch_a,
     mu_shift, w0, w_decay2, a0, w_iclr2, w_gate2, k_k, k_a, r_k, ln_x_w, ln_x_b, w_branch_b,
     w_out, g_pre_mix, g_post_mix, g_pre_ffn, g_post_ffn,
     w_up, conv_w, conv_b, w_down, g_pe, w_pe_gate, w_pe) = lp
    D = w_in.shape[0]
    R, QK = w_alpha2.shape
    GW = w_branch_a.shape[0]
    RW = w_branch_b.shape[0]
    WL, AL, GL = w_decay2.shape[0], w_iclr2.shape[0], w_gate2.shape[0]
    src = _layout(D, QK, GW, R, RW, WL, AL, GL)
    pad = lambda w: -(-w // LANE) * LANE
    prm = dict(
        dims=dict(D=D, R=R, QK=QK, GW=GW, RW=RW, WL=WL, AL=AL, GL=GL, DV=gla_norm.shape[0],
                  HD=r_k.shape[1], DFF=conv_b.shape[0]),
        src=src, off={n: src[n][0] for n in src},
        w_in=w_in,
        w_alpha2=_pad_rows(w_alpha2, LANE).astype(BF16), b_alpha=b_alpha, gla_norm=gla_norm,
        w_branch_a=w_branch_a.astype(BF16), w_branch_b=w_branch_b.astype(BF16),
        w0=w0, w_decay2=_pad_rows(w_decay2, LANE).astype(BF16), a0=a0,
        w_iclr2=_pad_rows(w_iclr2, LANE).astype(BF16), w_gate2=w_gate2.astype(BF16),
        k_k=k_k, k_a=k_a, r_k=r_k.reshape(-1), ln_x_w=ln_x_w, ln_x_b=ln_x_b,
        w_out=w_out.astype(BF16), g_pre_mix=g_pre_mix, g_post_mix=g_post_mix,
        g_pre_ffn=g_pre_ffn, g_post_ffn=g_post_ffn,
        w_up=w_up, conv_w=conv_w, conv_b=conv_b, w_down=w_down.astype(BF16),
        g_pe=g_pe, w_pe_gate=w_pe_gate.astype(BF16), w_pe=w_pe.astype(BF16),
    )
    rsrc = src["r"][0]
    rnames = ("r", "xw", "kr", "vr", "xa", "xg")
    prm["rnames"] = rnames
    prm["rsl"] = {n: (src[n][0] - rsrc, src[n][1]) for n in rnames}
    prm["mu"] = {n: _pad_cols(mu_shift[None, prm["rsl"][n][0]:prm["rsl"][n][0] + prm["rsl"][n][1]],
                              pad(prm["rsl"][n][1])) for n in rnames}
    return prm


def _pack_state(s, hd):
    B, H = s.shape[:2]
    return s.reshape(B, H // 2, 2, hd, hd).transpose(0, 1, 3, 2, 4).reshape(B, H // 2, hd, 2 * hd)


def _unpack_state(sp, hd):
    B, NP = sp.shape[:2]
    return sp.reshape(B, NP, hd, 2, hd).transpose(0, 1, 3, 2, 4).reshape(B, 2 * NP, hd, hd)


def _layer(x, p, s_gla, s_rwkv, s_shift, s_conv, prm):
    B, L, D = x.shape
    M = B * L
    dm, dst = prm["dims"], prm["off"]
    GW, RW, DV, HD, DFF, QK = dm["GW"], dm["RW"], dm["DV"], dm["HD"], dm["DFF"], dm["QK"]
    heads = GW // DV
    pad = lambda w: -(-w // LANE) * LANE
    x2 = x.reshape(M, D)

    proj = _norm_matmul(x2, prm["g_pre_mix"], prm["w_in"])
    proj3 = proj.reshape(B, L, -1)

    o_a, s_gla_new = _gla(proj3, dst, prm["w_alpha2"], prm["b_alpha"], prm["gla_norm"], s_gla,
                          dk=QK // heads, dv=DV, heads=heads)

    shift_parts = {n: _pad_cols(s_shift[:, None, prm["rsl"][n][0]:prm["rsl"][n][0] + prm["rsl"][n][1]],
                                pad(prm["rsl"][n][1])) for n in prm["rnames"]}
    r, lw, kr, vr, a, g = _rwkv_prep(proj3, dst, shift_parts, prm["mu"], prm["w0"], prm["w_decay2"],
                                     prm["a0"], prm["w_iclr2"], prm["w_gate2"], rw=RW, gl=dm["GL"])
    o_b, s_rwkv_new = _rwkv_chunk(r, lw, kr, vr, a, g, prm["k_k"], prm["k_a"], prm["r_k"],
                                  prm["ln_x_w"], prm["ln_x_b"], _pack_state(s_rwkv, HD), hd=HD)
    new_shift = proj3[:, L - 1, dst["r"]:dst["ga"]]

    mixed = _merge(o_a.reshape(M, GW), o_b.reshape(M, RW), prm["w_branch_a"], prm["w_branch_b"],
                   proj, dst["ga"], dst["gb"])
    x2 = _matmul_norm_residual(mixed, prm["w_out"], x2, prm["g_post_mix"], tm=512, tn=512)

    if L >= 256:
        act, new_conv = _up_conv_act(x2, prm["g_pre_ffn"], prm["w_up"], s_conv, prm["conv_w"],
                                     prm["conv_b"], seq_len=L)
    else:
        up = _norm_matmul(x2, prm["g_pre_ffn"], prm["w_up"])
        up3 = up.reshape(B, L, 2 * DFF)
        act = _conv_act(up3, s_conv, prm["conv_w"], prm["conv_b"], dff=DFF)
        new_conv = up3[:, L - 2:, :DFF]
    x2 = _matmul_norm_residual(act.reshape(M, DFF), prm["w_down"], x2, prm["g_post_ffn"], tm=512, tn=256)

    x2 = _pe_layer(x2, prm["g_pe"], prm["w_pe_gate"], p.reshape(M, -1), prm["w_pe"])
    return x2.reshape(B, L, D), s_gla_new, _unpack_state(s_rwkv_new, HD), new_shift, new_conv


def kernel(x_prompt, x_sample, state_gla, state_rwkv, state_shift, state_ffn_conv, p_prompt, p_sample, w_in, w_alpha2, b_alpha, gla_norm, w_branch_a, mu_shift, w0, w_decay2, a0, w_iclr2, w_gate2, k_k, k_a, r_k, ln_x_w, ln_x_b, w_branch_b, w_out, g_pre_mix, g_post_mix, g_pre_ffn, g_post_ffn, w_up, conv_w, conv_b, w_down, g_pe, w_pe_gate, w_pe):
    params = (w_in, w_alpha2, b_alpha, gla_norm, w_branch_a,
              mu_shift, w0, w_decay2, a0, w_iclr2, w_gate2, k_k, k_a, r_k, ln_x_w, ln_x_b, w_branch_b,
              w_out, g_pre_mix, g_post_mix, g_pre_ffn, g_post_ffn,
              w_up, conv_w, conv_b, w_down, g_pe, w_pe_gate, w_pe)
    depth = w_in.shape[0]
    nb = x_prompt.shape[0]
    yp, ys = x_prompt, x_sample
    outs_p = [[], [], [], []]
    outs_s = [[], [], [], []]
    for i in range(depth):
        prm = _prepare_params(tuple(t[i] for t in params))
        z_gla = jnp.zeros((nb,) + state_gla.shape[2:], F32)
        z_rwkv = jnp.zeros((nb,) + state_rwkv.shape[2:], F32)
        z_shift = jnp.zeros((nb,) + state_shift.shape[2:], x_prompt.dtype)
        z_conv = jnp.zeros((nb,) + state_ffn_conv.shape[2:], x_prompt.dtype)
        yp, *st = _layer(yp, p_prompt[i], z_gla, z_rwkv, z_shift, z_conv, prm)
        for acc, s in zip(outs_p, st):
            acc.append(s)
        ys, *st = _layer(ys, p_sample[i], state_gla[i], state_rwkv[i], state_shift[i],
                         state_ffn_conv[i], prm)
        for acc, s in zip(outs_s, st):
            acc.append(s)
    return (yp, ys, *(jnp.stack(a) for a in outs_p), *(jnp.stack(a) for a in outs_s))
```

```python
import functools
import math

import jax
import jax.numpy as jnp
from jax import lax
from jax.experimental import pallas as pl
from jax.experimental.pallas import tpu as pltpu

F32 = jnp.float32
BF16 = jnp.bfloat16

LANE = 128
SUBLANE = 8
VMEM_LIMIT_BYTES = 56 * 1024 * 1024

EPS = 1e-6
GN_EPS = 64e-5
GLA_TAU = 16.0
GLA_CHUNK = 64
RWKV_CHUNK = 64
KK_EPS = 1e-12
DECAY_SCALE = math.exp(-0.5)
GELU_C = math.sqrt(2.0 / math.pi)


ROW_RESIDENT = pl.Buffered(1)


def _cparams(sem):
    return pltpu.CompilerParams(dimension_semantics=sem, vmem_limit_bytes=VMEM_LIMIT_BYTES)


def _sigmoid(x):
    return 1.0 / (1.0 + jnp.exp(-x))


def _log_sigmoid(x):
    return jnp.minimum(x, 0.0) - jnp.log(1.0 + jnp.exp(-jnp.abs(x)))


def _gelu_tanh(x):
    return 0.5 * x * (1.0 + jnp.tanh(GELU_C * (x + 0.044715 * (x * x * x))))


def _split2(x):
    hi = x.astype(BF16)
    lo = (x - hi.astype(F32)).astype(BF16)
    return hi, lo


def _dot(a, b):
    return jnp.dot(a, b, preferred_element_type=F32)


def _dot_nt(a, b):
    return lax.dot_general(a, b, (((1,), (1,)), ((), ())), preferred_element_type=F32)


def _dot_tn(a, b):
    return lax.dot_general(a, b, (((0,), (0,)), ((), ())), preferred_element_type=F32)


def _ones_dot(m01, x):
    hi, lo = _split2(x)
    return _dot(m01, hi) + _dot(m01, lo)


def _dot_ones(x, m01):
    hi, lo = _split2(x)
    return _dot(hi, m01) + _dot(lo, m01)


def _lockstep(gens):
    gens = list(gens)
    while gens:
        alive = []
        for g in gens:
            try:
                next(g)
                alive.append(g)
            except StopIteration:
                pass
        gens = alive


def _tri_incl(n):
    r = lax.broadcasted_iota(jnp.int32, (n, n), 0)
    c = lax.broadcasted_iota(jnp.int32, (n, n), 1)
    return r >= c


def _pick_tile(n, target, mult):
    if n <= target:
        return n
    best = None
    t = mult
    while t <= target:
        if n % t == 0:
            best = t
        t += mult
    assert best is not None, (n, target, mult)
    return best


def _rms_rows(x_ref, g_ref, h_ref, rows):
    tm = x_ref.shape[0]

    def body(i, _):
        sl = pl.ds(pl.multiple_of(i * rows, rows), rows)
        x = x_ref[sl, :]
        ms = jnp.mean(x * x, axis=-1, keepdims=True)
        h_ref[sl, :] = (x * lax.rsqrt(ms + EPS) * g_ref[...]).astype(BF16)
        return 0

    lax.fori_loop(0, tm // rows, body, 0)


def _norm_mm_kernel(x_ref, g_ref, w_ref, o_ref, h_ref, *, rows, w_is_transposed):
    @pl.when(pl.program_id(1) == 0)
    def _():
        _rms_rows(x_ref, g_ref, h_ref, rows)

    w = w_ref[...].astype(BF16)
    o_ref[...] = _dot_nt(h_ref[...], w) if w_is_transposed else _dot(h_ref[...], w)


def _window(src, width):
    base = src // LANE * LANE
    shift = src - base
    wp = -(-width // LANE) * LANE
    return base, shift, (wp if shift == 0 else wp + LANE), wp


def _norm_matmul(x, gain, w, *, w_is_transposed=False, tm=1024, tn=512):
    M, K = x.shape
    nw = w.shape[0] if w_is_transposed else w.shape[1]
    tm = _pick_tile(M, tm, SUBLANE)
    tn = min(tn, -(-nw // LANE) * LANE)
    N = -(-nw // tn) * tn
    rows = _pick_tile(tm, 64, SUBLANE)
    w_spec = (pl.BlockSpec((tn, K), lambda i, j: (j, 0)) if w_is_transposed
              else pl.BlockSpec((K, tn), lambda i, j: (0, j)))
    return pl.pallas_call(
        functools.partial(_norm_mm_kernel, rows=rows, w_is_transposed=w_is_transposed),
        grid=(M // tm, N // tn),
        in_specs=[pl.BlockSpec((tm, K), lambda i, j: (i, 0), pipeline_mode=ROW_RESIDENT),
                  pl.BlockSpec((1, K), lambda i, j: (0, 0)),
                  w_spec],
        out_specs=pl.BlockSpec((tm, tn), lambda i, j: (i, j)),
        out_shape=jax.ShapeDtypeStruct((M, N), F32),
        scratch_shapes=[pltpu.VMEM((tm, K), BF16)],
        compiler_params=_cparams(("parallel", "arbitrary")),
        name="norm_matmul",
    )(x, gain.reshape(1, K), w)


def _pe_kernel(x_ref, g_ref, w_ref, p_ref, wp_ref, o_ref, h_ref, *, rows, tn):
    j = pl.program_id(1)

    @pl.when(j == 0)
    def _():
        _rms_rows(x_ref, g_ref, h_ref, rows)

    gate = _sigmoid(_dot(h_ref[...], w_ref[...]))
    pe = _dot(p_ref[...].astype(BF16), wp_ref[...])
    xs = x_ref[:, pl.ds(pl.multiple_of(j * tn, tn), tn)]
    o_ref[...] = xs + gate * pe


def _pe_layer(x, gain, w_gate, p, w_pe, *, tm=1024, tn=512):
    M, K = x.shape
    N = w_gate.shape[1]
    P = p.shape[1]
    tm = _pick_tile(M, tm, SUBLANE)
    tn = _pick_tile(N, tn, LANE)
    rows = _pick_tile(tm, 64, SUBLANE)
    return pl.pallas_call(
        functools.partial(_pe_kernel, rows=rows, tn=tn),
        grid=(M // tm, N // tn),
        in_specs=[pl.BlockSpec((tm, K), lambda i, j: (i, 0), pipeline_mode=ROW_RESIDENT),
                  pl.BlockSpec((1, K), lambda i, j: (0, 0)),
                  pl.BlockSpec((K, tn), lambda i, j: (0, j)),
                  pl.BlockSpec((tm, P), lambda i, j: (i, 0)),
                  pl.BlockSpec((P, tn), lambda i, j: (0, j))],
        out_specs=pl.BlockSpec((tm, tn), lambda i, j: (i, j)),
        out_shape=jax.ShapeDtypeStruct((M, N), F32),
        scratch_shapes=[pltpu.VMEM((tm, K), BF16)],
        compiler_params=_cparams(("parallel", "arbitrary")),
        name="pe_layer",
    )(x, gain.reshape(1, K), w_gate, p, w_pe)


def _mm_norm_res_kernel(a_ref, w_ref, x_ref, g_ref, o_ref, *, tn, rows):
    j = pl.program_id(1)
    o_ref[:, pl.ds(pl.multiple_of(j * tn, tn), tn)] = _dot(a_ref[...], w_ref[...])

    @pl.when(j == pl.num_programs(1) - 1)
    def _():
        tm = o_ref.shape[0]

        def body(i, _):
            sl = pl.ds(pl.multiple_of(i * rows, rows), rows)
            y = o_ref[sl, :]
            ms = jnp.mean(y * y, axis=-1, keepdims=True)
            o_ref[sl, :] = x_ref[sl, :] + y * lax.rsqrt(ms + EPS) * g_ref[...]
            return 0

        lax.fori_loop(0, tm // rows, body, 0)


def _matmul_norm_residual(a, w, x, gain, *, tm, tn):
    M, K = a.shape
    N = w.shape[1]
    tm = _pick_tile(M, tm, SUBLANE)
    tn = _pick_tile(N, tn, LANE)
    rows = _pick_tile(tm, 64, SUBLANE)
    return pl.pallas_call(
        functools.partial(_mm_norm_res_kernel, tn=tn, rows=rows),
        grid=(M // tm, N // tn),
        in_specs=[pl.BlockSpec((tm, K), lambda i, j: (i, 0), pipeline_mode=ROW_RESIDENT),
                  pl.BlockSpec((K, tn), lambda i, j: (0, j)),
                  pl.BlockSpec((tm, N), lambda i, j: (i, 0), pipeline_mode=ROW_RESIDENT),
                  pl.BlockSpec((1, N), lambda i, j: (0, 0))],
        out_specs=pl.BlockSpec((tm, N), lambda i, j: (i, 0)),
        out_shape=jax.ShapeDtypeStruct((M, N), F32),
        compiler_params=_cparams(("parallel", "arbitrary")),
        name="matmul_norm_residual",
    )(a, w, x, gain.reshape(1, N))


def _merge_kernel(oa_ref, ob_ref, wa_ref, wb_ref, ga_ref, gb_ref, o_ref, *, sa, sb):
    tn = o_ref.shape[1]
    ya = _dot(oa_ref[...], wa_ref[...])
    yb = _dot(ob_ref[...], wb_ref[...])
    ga = ga_ref[:, sa:sa + tn]
    gb = gb_ref[:, sb:sb + tn]
    o_ref[...] = (_sigmoid(ga) * ya + _sigmoid(gb) * yb).astype(BF16)


def _gate_window_spec(tm, tn, off, n_cols):
    base, shift, _, _ = _window(off, tn)
    ww = tn if shift == 0 else tn + LANE
    spec = pl.BlockSpec((pl.Element(tm), pl.Element(ww)),
                        lambda i, j: (pl.multiple_of(i * tm, SUBLANE), pl.multiple_of(base + j * tn, LANE)))
    return spec, shift, base + n_cols - tn + ww


def _merge(oa, ob, wa, wb, proj, ga_off, gb_off, *, tm=512, tn=1024):
    M, KA = oa.shape
    KB = ob.shape[1]
    N = wa.shape[1]
    tm = _pick_tile(M, tm, SUBLANE)
    tn = _pick_tile(N, tn, LANE)
    ga_spec, sa, enda = _gate_window_spec(tm, tn, ga_off, N)
    gb_spec, sb, endb = _gate_window_spec(tm, tn, gb_off, N)
    assert max(enda, endb) <= proj.shape[1]
    return pl.pallas_call(
        functools.partial(_merge_kernel, sa=sa, sb=sb),
        grid=(M // tm, N // tn),
        in_specs=[pl.BlockSpec((tm, KA), lambda i, j: (i, 0)),
                  pl.BlockSpec((tm, KB), lambda i, j: (i, 0)),
                  pl.BlockSpec((KA, tn), lambda i, j: (0, j)),
                  pl.BlockSpec((KB, tn), lambda i, j: (0, j)),
                  ga_spec, gb_spec],
        out_specs=pl.BlockSpec((tm, tn), lambda i, j: (i, j)),
        out_shape=jax.ShapeDtypeStruct((M, N), BF16),
        compiler_params=_cparams(("parallel", "arbitrary")),
        name="merge",
    )(oa, ob, wa, wb, proj, proj)


def _gla_kernel(q_ref, k_ref, v_ref, za_ref, zg_ref, wa_ref, ba_ref, gn_ref, s0_ref,
                o_ref, sn_ref, s_ref, *, scale, c, heads, zshift):
    l = pl.program_id(1)

    @pl.when(l == 0)
    def _():
        s_ref[...] = s0_ref[0]

    dk = q_ref.shape[2] // heads
    dv = v_ref.shape[2] // heads
    x = _dot(za_ref[0].astype(BF16), wa_ref[...]) + ba_ref[...]
    la_all = _log_sigmoid(x) * (1.0 / GLA_TAU)
    tri = _tri_incl(c)
    b_all = _ones_dot(jnp.where(tri, 1.0, 0.0).astype(BF16), la_all)
    ones = jnp.ones((c, LANE), BF16)
    def head(h):
        ks = slice(h * dk, (h + 1) * dk)
        vs = slice(h * dv, (h + 1) * dv)
        la = la_all[:, ks]
        b = b_all[:, ks]
        b_end = b[c - 1:c, :]
        q = q_ref[0, :, ks] * scale
        k = k_ref[0, :, ks]
        v = v_ref[0, :, vs].astype(BF16)
        qd = (q * jnp.exp(b)).astype(BF16)
        kd = (k * jnp.exp(-b)).astype(BF16)
        s = s_ref[h]
        att = _dot_nt(qd, kd)
        o = _dot(qd, s.astype(BF16))
        k_end = (k * jnp.exp(b_end - b)).astype(BF16)
        la_hi, la_lo = _split2(la)
        dec = jnp.exp(_dot_tn(la_hi, ones) + _dot_tn(la_lo, ones))
        dec = jnp.concatenate([dec] * (dv // LANE), axis=1)
        s_ref[h] = s * dec + _dot_tn(k_end, v)
        yield
        o = o + _dot(jnp.where(tri, att, 0.0).astype(BF16), v)
        yield
        ms = jnp.mean(o * o, axis=-1, keepdims=True)
        on = o * lax.rsqrt(ms + EPS) * gn_ref[...]
        zg = zg_ref[0, :, zshift + h * dv:zshift + (h + 1) * dv]
        o_ref[0, :, vs] = (on * (zg * _sigmoid(zg))).astype(BF16)

    _lockstep([head(h) for h in range(heads)])

    @pl.when(l == pl.num_programs(1) - 1)
    def _():
        sn_ref[0] = s_ref[...]


def _gla(proj3, offs, w_alpha2p, b_alpha, gla_norm, s0, *, dk, dv, heads):
    B, L, _ = proj3.shape
    c = min(GLA_CHUNK, L)
    qk, gw = heads * dk, heads * dv
    assert L % c == 0 and dv % LANE == 0
    for name, wdt in (("q", qk), ("k", qk), ("v", gw), ("za", LANE)):
        assert offs[name] % wdt == 0
    jq, jk, jv, jz = offs["q"] // qk, offs["k"] // qk, offs["v"] // gw, offs["za"] // LANE
    zbase, zshift, zww, _ = _window(offs["zg"], gw)
    assert zbase + zww <= proj3.shape[2]
    return pl.pallas_call(
        functools.partial(_gla_kernel, scale=dk ** -0.5, c=c, heads=heads, zshift=zshift),
        grid=(B, L // c),
        in_specs=[pl.BlockSpec((1, c, qk), lambda b, l: (b, l, jq)),
                  pl.BlockSpec((1, c, qk), lambda b, l: (b, l, jk)),
                  pl.BlockSpec((1, c, gw), lambda b, l: (b, l, jv)),
                  pl.BlockSpec((1, c, LANE), lambda b, l: (b, l, jz)),
                  pl.BlockSpec((pl.Element(1), pl.Element(c), pl.Element(zww)),
                               lambda b, l: (b, pl.multiple_of(l * c, c), zbase)),
                  pl.BlockSpec((LANE, qk), lambda b, l: (0, 0)),
                  pl.BlockSpec((1, qk), lambda b, l: (0, 0)),
                  pl.BlockSpec((1, dv), lambda b, l: (0, 0)),
                  pl.BlockSpec((1, heads, dk, dv), lambda b, l: (b, 0, 0, 0))],
        out_specs=[pl.BlockSpec((1, c, gw), lambda b, l: (b, l, 0)),
                   pl.BlockSpec((1, heads, dk, dv), lambda b, l: (b, 0, 0, 0))],
        out_shape=[jax.ShapeDtypeStruct((B, L, gw), BF16),
                   jax.ShapeDtypeStruct((B, heads, dk, dv), F32)],
        scratch_shapes=[pltpu.VMEM((heads, dk, dv), F32)],
        compiler_params=_cparams(("parallel", "arbitrary")),
        name="gla",
    )(proj3, proj3, proj3, proj3, proj3, w_alpha2p, b_alpha.reshape(1, -1), gla_norm.reshape(1, dv), s0)


def _shifted(cur, first_row):
    rolled = pltpu.roll(cur, 1, 0)
    row = lax.broadcasted_iota(jnp.int32, cur.shape, 0)
    return jnp.where(row == 0, first_row, rolled)


def _rwkv_prep_kernel(r_ref, xw_ref, kr_ref, vr_ref, xa_ref, xg_ref,
                      sr_ref, sxw_ref, skr_ref, svr_ref, sxa_ref, sxg_ref,
                      mr_ref, mxw_ref, mkr_ref, mvr_ref, mxa_ref, mxg_ref,
                      w0_ref, wd_ref, a0_ref, wi_ref, wg_ref,
                      ro_ref, lw_ref, ko_ref, vo_ref, ao_ref, go_ref,
                      cr_ref, cxw_ref, ckr_ref, cvr_ref, cxa_ref, cxg_ref, *, shifts):
    l = pl.program_id(1)
    tl = r_ref.shape[1]
    sh_r, sh_xw, sh_kr, sh_vr, sh_xa, sh_xg = shifts

    def shift_mix(z_ref, s0, s_ref, m_ref, c_ref):
        cur = z_ref[0, :, s0:s0 + m_ref.shape[1]]
        first = jnp.where(l == 0, s_ref[0], c_ref[...])
        prev = _shifted(cur, first)
        c_ref[...] = cur[tl - 1:tl, :]
        return cur + (prev - cur) * m_ref[...]

    ro_ref[0] = shift_mix(r_ref, sh_r, sr_ref, mr_ref, cr_ref)
    ko_ref[0] = shift_mix(kr_ref, sh_kr, skr_ref, mkr_ref, ckr_ref)
    vo_ref[0] = shift_mix(vr_ref, sh_vr, svr_ref, mvr_ref, cvr_ref)
    xw = shift_mix(xw_ref, sh_xw, sxw_ref, mxw_ref, cxw_ref)
    xa = shift_mix(xa_ref, sh_xa, sxa_ref, mxa_ref, cxa_ref)
    xg = shift_mix(xg_ref, sh_xg, sxg_ref, mxg_ref, cxg_ref)
    z = w0_ref[...] + _dot(jnp.tanh(xw).astype(BF16), wd_ref[...])
    lw_ref[0] = -DECAY_SCALE * _sigmoid(z)
    ao_ref[0] = _sigmoid(a0_ref[...] + _dot(xa.astype(BF16), wi_ref[...]))
    go_ref[0] = _dot(_sigmoid(xg).astype(BF16), wg_ref[...])


def _rwkv_prep(proj3, offs, shift_parts, mu_parts, w0, w_decay2p, a0, w_iclr2p, w_gate2, *, rw, gl):
    B, L, _ = proj3.shape
    tl = _pick_tile(L, 256, SUBLANE)
    names = ("r", "xw", "kr", "vr", "xa", "xg")
    widths = {"r": rw, "xw": LANE, "kr": rw, "vr": rw, "xa": LANE, "xg": gl}
    in_specs, args, shifts = [], [], []
    for n in names:
        base, shift, ww, wp = _window(offs[n], widths[n])
        assert wp == widths[n] and base + ww <= proj3.shape[2]
        shifts.append(shift)
        in_specs.append(pl.BlockSpec(
            (pl.Element(1), pl.Element(tl), pl.Element(ww)),
            functools.partial(lambda b, l, base: (b, pl.multiple_of(l * tl, SUBLANE), base), base=base)))
        args.append(proj3)
    for n in names:
        in_specs.append(pl.BlockSpec((1, 1, widths[n]), lambda b, l: (b, 0, 0)))
        args.append(shift_parts[n])
    for n in names:
        in_specs.append(pl.BlockSpec((1, widths[n]), lambda b, l: (0, 0)))
        args.append(mu_parts[n])
    for arr in (w0.reshape(1, rw), w_decay2p, a0.reshape(1, rw), w_iclr2p, w_gate2):
        in_specs.append(pl.BlockSpec(arr.shape, lambda b, l: (0, 0)))
        args.append(arr)
    out_spec = pl.BlockSpec((1, tl, rw), lambda b, l: (b, l, 0))
    out_sd = jax.ShapeDtypeStruct((B, L, rw), F32)
    return pl.pallas_call(
        functools.partial(_rwkv_prep_kernel, shifts=tuple(shifts)),
        grid=(B, L // tl),
        in_specs=in_specs,
        out_specs=[out_spec] * 6,
        out_shape=[out_sd] * 6,
        scratch_shapes=[pltpu.VMEM((1, widths[n]), F32) for n in names],
        compiler_params=_cparams(("parallel", "arbitrary")),
        name="rwkv_prep",
    )(*args)


def _rwkv_chunk_kernel(r_ref, lw_ref, kr_ref, v_ref, a_ref, g_ref,
                       kk_w_ref, ka_w_ref, rk_w_ref, lnw_ref, lnb_ref, s0_ref,
                       o_ref, sn_ref, w_ref, *, c, hd, gp):
    l = pl.program_id(2)
    tl = r_ref.shape[1]
    n2 = 2 * c
    lane_c = lax.broadcasted_iota(jnp.int32, (c, LANE), 1)
    head0_c = lane_c < hd
    row2 = lax.broadcasted_iota(jnp.int32, (n2, n2), 0)
    col2 = lax.broadcasted_iota(jnp.int32, (n2, n2), 1)
    same = (2 * row2 + 1 - n2) * (2 * col2 + 1 - n2) > 0
    strict = jnp.logical_and(same, row2 > col2)
    incl = jnp.logical_and(same, row2 >= col2)
    eye = jnp.where(row2 == col2, 1.0, 0.0)
    tri = jnp.where(_tri_incl(c), 1.0, 0.0).astype(BF16)
    lr = lax.broadcasted_iota(jnp.int32, (LANE, LANE), 0)
    lc = lax.broadcasted_iota(jnp.int32, (LANE, LANE), 1)
    seg = jnp.where((2 * lr + 1 - LANE) * (2 * lc + 1 - LANE) > 0, 1.0, 0.0).astype(BF16)

    @pl.when(l == 0)
    def _():
        lane_s = lax.broadcasted_iota(jnp.int32, (hd, LANE), 1)
        for gi in range(gp):
            s0 = s0_ref[0, gi]
            w_ref[gi] = jnp.concatenate([jnp.where(lane_s < hd, s0, 0.0),
                                         jnp.where(lane_s < hd, 0.0, s0)], axis=0)

    def stack(x):
        return jnp.concatenate([jnp.where(head0_c, x, 0.0), jnp.where(head0_c, 0.0, x)], axis=0)

    def unstack(x):
        return x[0:c] + x[c:n2]

    def chunk_pair(sl, gi):
        ls = slice(gi * LANE, (gi + 1) * LANE)
        r = r_ref[0, sl, ls]
        lw = lw_ref[0, sl, ls]
        kr = kr_ref[0, sl, ls]
        v = v_ref[0, sl, ls]
        a = a_ref[0, sl, ls]
        kk = kr * kk_w_ref[:, ls]
        kk_ss = _dot_ones(kk * kk, seg)
        cum = _ones_dot(tri, lw)
        yield
        kk = kk / jnp.maximum(jnp.sqrt(kk_ss), KK_EPS)
        k = kr * (1.0 + (a - 1.0) * ka_w_ref[:, ls])
        cum_end = cum[c - 1:c, :]
        e_neg = jnp.exp(-cum)
        e_rem = jnp.exp(cum_end - cum)
        al = kk * a
        al_t = stack(al * e_neg).astype(BF16)
        k_t = stack(k * e_neg).astype(BF16)
        be_t = stack(-kk * jnp.exp(cum - lw)).astype(BF16)
        r_t = stack(r * jnp.exp(cum)).astype(BF16)
        v_s = stack(v).astype(BF16)
        lhs = jnp.concatenate([be_t, r_t], axis=0)
        sc_a = _dot_nt(lhs, al_t)
        sc_k = _dot_nt(lhs, k_t)
        w = w_ref[gi]
        rd = _dot_nt(lhs, w.astype(BF16))
        yield
        l_a =jnp.where(strict, sc_a[0:n2], 0.0)
        l_k = jnp.where(strict, sc_k[0:n2], 0.0).astype(BF16)
        m_a = jnp.where(incl, sc_a[n2:2 * n2], 0.0).astype(BF16)
        m_k = jnp.where(incl, sc_k[n2:2 * n2], 0.0).astype(BF16)
        t_inv = eye + l_a
        lp = l_a
        rhs_u = rd[0:n2] + _dot(l_k, v_s)
        o_s = rd[n2:2 * n2] + _dot(m_k, v_s)
        bonus = _dot_ones(r * k * rk_w_ref[:, ls], seg) * v
        span = 1
        while 2 * span < c:
            lpb = lp.astype(BF16)
            lp = _dot(lpb, lpb)
            yield
            t_inv = t_inv + _dot(t_inv.astype(BF16), lp.astype(BF16))
            yield
            span *= 2
        u_s = _dot(t_inv.astype(BF16), rhs_u.astype(BF16))
        yield
        u_b = u_s.astype(BF16)
        o_s = o_s + _dot(m_a, u_b)
        uv = jnp.concatenate([u_b, v_s], axis=0)
        ak = jnp.concatenate([stack(al * e_rem), stack(k * e_rem)], axis=0).astype(BF16)
        w_ref[gi] = w * jnp.exp(cum_end) + _dot_tn(uv, ak)
        yield
        o = unstack(o_s)
        mu = _dot_ones(o, seg) * (1.0 / hd)
        yield
        d = o - mu
        var = _dot_ones(d * d, seg) * (1.0 / hd)
        yield
        on = d * lax.rsqrt(var + GN_EPS) * lnw_ref[:, ls] + lnb_ref[:, ls]
        o_ref[0, sl, ls] = ((on + bonus) * g_ref[0, sl, ls]).astype(BF16)

    def chunk(ci, _):
        sl = pl.ds(pl.multiple_of(ci * c, c), c)
        _lockstep([chunk_pair(sl, gi) for gi in range(gp)])
        return 0

    lax.fori_loop(0, tl // c, chunk, 0)

    @pl.when(l == pl.num_programs(2) - 1)
    def _():
        for gi in range(gp):
            wf = w_ref[gi]
            sn_ref[0, gi] = wf[0:hd] + wf[hd:2 * hd]


def _rwkv_chunk(r, lw, kr, vr, a, g, k_k, k_a, r_k, ln_w, ln_b, s0p, *, hd):
    B, L, RW = r.shape
    assert 2 * hd == LANE and RW % LANE == 0
    c = min(RWKV_CHUNK, L)
    tl = _pick_tile(L, 512, c)
    assert L % c == 0 and tl % c == 0 and c % SUBLANE == 0
    npair = RW // LANE
    gp = _pick_tile(npair, 8 if L > c else 16, 1)
    seq = pl.BlockSpec((1, tl, gp * LANE), lambda b, p, l: (b, l, p))
    par = pl.BlockSpec((1, gp * LANE), lambda b, p, l: (0, p))
    st = pl.BlockSpec((1, gp, hd, LANE), lambda b, p, l: (b, p, 0, 0))
    return pl.pallas_call(
        functools.partial(_rwkv_chunk_kernel, c=c, hd=hd, gp=gp),
        grid=(B, npair // gp, L // tl),
        in_specs=[seq] * 6 + [par] * 5 + [st],
        out_specs=[seq, st],
        out_shape=[jax.ShapeDtypeStruct((B, L, RW), BF16),
                   jax.ShapeDtypeStruct((B, npair, hd, LANE), F32)],
        scratch_shapes=[pltpu.VMEM((gp, LANE, LANE), F32)],
        compiler_params=_cparams(("parallel", "parallel", "arbitrary")),
        name="rwkv_chunk",
    )(r, lw, kr, vr, a, g, k_k.reshape(1, RW), k_a.reshape(1, RW), r_k.reshape(1, RW),
      ln_w.reshape(1, RW), ln_b.reshape(1, RW), s0p)


def _conv_act_kernel(g_ref, v_ref, s_ref, cw_ref, cb_ref, o_ref, c_ref):
    l = pl.program_id(2)
    tl = g_ref.shape[1]
    g = g_ref[0]
    prev = jnp.where(l == 0, s_ref[0], c_ref[...])
    row = lax.broadcasted_iota(jnp.int32, g.shape, 0)
    g1 = jnp.where(row == 0, prev[1:2], pltpu.roll(g, 1, 0))
    g2 = jnp.where(row == 0, prev[0:1], jnp.where(row == 1, prev[1:2], pltpu.roll(g, 2, 0)))
    c_ref[...] = g[tl - 2:tl, :]
    conv = cb_ref[...] + g2 * cw_ref[0:1, :] + g1 * cw_ref[1:2, :] + g * cw_ref[2:3, :]
    o_ref[0] = (_gelu_tanh(conv) * v_ref[0]).astype(BF16)


def _conv_act(up3, s_conv, conv_w, conv_b, *, dff):
    B, L, _ = up3.shape
    assert conv_w.shape[0] == 3 and s_conv.shape[1] == 2 and L >= 2
    tl = _pick_tile(L, 2048, SUBLANE)
    tn = _pick_tile(dff, max(LANE, (1 << 20) // tl), LANE)
    nj = dff // tn
    return pl.pallas_call(
        _conv_act_kernel,
        grid=(B, nj, L // tl),
        in_specs=[pl.BlockSpec((1, tl, tn), lambda b, j, l: (b, l, j)),
                  pl.BlockSpec((1, tl, tn), lambda b, j, l: (b, l, nj + j)),
                  pl.BlockSpec((1, 2, tn), lambda b, j, l: (b, 0, j)),
                  pl.BlockSpec((3, tn), lambda b, j, l: (0, j)),
                  pl.BlockSpec((1, tn), lambda b, j, l: (0, j))],
        out_specs=pl.BlockSpec((1, tl, tn), lambda b, j, l: (b, l, j)),
        out_shape=jax.ShapeDtypeStruct((B, L, dff), BF16),
        scratch_shapes=[pltpu.VMEM((2, tn), F32)],
        compiler_params=_cparams(("parallel", "parallel", "arbitrary")),
        name="conv_act",
    )(up3, up3, s_conv, conv_w, conv_b.reshape(1, dff))


def _up_conv_kernel(x_ref, gn_ref, wg_ref, wv_ref, s_ref, cw_ref, cb_ref, act_ref, nc_ref, h_ref, c_ref,
                    *, rows, tiles_per_seq):
    i, j = pl.program_id(0), pl.program_id(1)
    tm = x_ref.shape[0]

    @pl.when(j == 0)
    def _():
        _rms_rows(x_ref, gn_ref, h_ref, rows)

    h = h_ref[...]
    g = _dot(h, wg_ref[...].astype(BF16))
    v = _dot(h, wv_ref[...].astype(BF16))
    first = (i % tiles_per_seq) == 0
    prev = jnp.where(first, s_ref[0], c_ref[j])
    row = lax.broadcasted_iota(jnp.int32, g.shape, 0)
    g1 = jnp.where(row == 0, prev[1:2], pltpu.roll(g, 1, 0))
    g2 = jnp.where(row == 0, prev[0:1], jnp.where(row == 1, prev[1:2], pltpu.roll(g, 2, 0)))
    last2 = g[tm - 2:tm, :]
    c_ref[j] = last2
    nc_ref[0] = last2
    conv = cb_ref[...] + g2 * cw_ref[0:1, :] + g1 * cw_ref[1:2, :] + g * cw_ref[2:3, :]
    act_ref[...] = (_gelu_tanh(conv) * v).astype(BF16)


def _up_conv_act(x, gain, w_up, s_conv, conv_w, conv_b, *, seq_len, tm=1024, tn=256):
    M, K = x.shape
    dff = conv_b.shape[0]
    B = M // seq_len
    tm = _pick_tile(seq_len, tm, SUBLANE)
    tn = _pick_tile(dff, tn, LANE)
    nj = dff // tn
    tps = seq_len // tm
    rows = _pick_tile(tm, 64, SUBLANE)
    assert conv_w.shape[0] == 3 and s_conv.shape[1] == 2 and tm >= 2
    act, tails = pl.pallas_call(
        functools.partial(_up_conv_kernel, rows=rows, tiles_per_seq=tps),
        grid=(M // tm, nj),
        in_specs=[pl.BlockSpec((tm, K), lambda i, j: (i, 0), pipeline_mode=ROW_RESIDENT),
                  pl.BlockSpec((1, K), lambda i, j: (0, 0)),
                  pl.BlockSpec((K, tn), lambda i, j: (0, j)),
                  pl.BlockSpec((K, tn), lambda i, j: (0, nj + j)),
                  pl.BlockSpec((1, 2, tn), lambda i, j: (i // tps, 0, j)),
                  pl.BlockSpec((3, tn), lambda i, j: (0, j)),
                  pl.BlockSpec((1, tn), lambda i, j: (0, j))],
        out_specs=[pl.BlockSpec((tm, tn), lambda i, j: (i, j)),
                   pl.BlockSpec((1, 2, tn), lambda i, j: (i, 0, j))],
        out_shape=[jax.ShapeDtypeStruct((M, dff), BF16),
                   jax.ShapeDtypeStruct((M // tm, 2, dff), F32)],
        scratch_shapes=[pltpu.VMEM((tm, K), BF16), pltpu.VMEM((nj, 2, tn), F32)],
        compiler_params=_cparams(("arbitrary", "arbitrary")),
        name="up_conv_act",
    )(x, gain.reshape(1, K), w_up, w_up, s_conv, conv_w, conv_b.reshape(1, dff))
    return act, tails[tps - 1::tps]


def _pad_cols(a, width):
    return a if a.shape[-1] == width else jnp.pad(a, [(0, 0)] * (a.ndim - 1) + [(0, width - a.shape[-1])])


def _pad_rows(a, height):
    return a if a.shape[0] == height else jnp.pad(a, [(0, height - a.shape[0])] + [(0, 0)] * (a.ndim - 1))


def _layout(D, QK, GW, R, RW, WL, AL, GL):
    src, o = {}, 0
    for name, w in (("q", QK), ("k", QK), ("v", GW), ("za", R), ("zg", GW), ("r", RW), ("xw", WL),
                    ("kr", RW), ("vr", RW), ("xa", AL), ("xg", GL), ("ga", D), ("gb", D)):
        src[name] = (o, w)
        o += w
    return src


def _prepare_params(lp):
    (w_in, w_alpha2, b_alpha, gla_norm, w_branch_a,
     mu_shift, w0, w_decay2, a0, w_iclr2, w_gate2, k_k, k_a, r_k, ln_x_w, ln_x_b, w_branch_b,
     w_out, g_pre_mix, g_post_mix, g_pre_ffn, g_post_ffn,
     w_up, conv_w, conv_b, w_down, g_pe, w_pe_gate, w_pe) = lp
    D = w_in.shape[0]
    R, QK = w_alpha2.shape
    GW = w_branch_a.shape[0]
    RW = w_branch_b.shape[0]
    WL, AL, GL = w_decay2.shape[0], w_iclr2.shape[0], w_gate2.shape[0]
    src = _layout(D, QK, GW, R, RW, WL, AL, GL)
    pad = lambda w: -(-w // LANE) * LANE
    prm = dict(
        dims=dict(D=D, R=R, QK=QK, GW=GW, RW=RW, WL=WL, AL=AL, GL=GL, DV=gla_norm.shape[0],
                  HD=r_k.shape[1], DFF=conv_b.shape[0]),
        src=src, off={n: src[n][0] for n in src},
        w_in_t=w_in.T,
        w_alpha2=_pad_rows(w_alpha2, LANE).astype(BF16), b_alpha=b_alpha, gla_norm=gla_norm,
        w_branch_a=w_branch_a.astype(BF16), w_branch_b=w_branch_b.astype(BF16),
        w0=w0, w_decay2=_pad_rows(w_decay2, LANE).astype(BF16), a0=a0,
        w_iclr2=_pad_rows(w_iclr2, LANE).astype(BF16), w_gate2=w_gate2.astype(BF16),
        k_k=k_k, k_a=k_a, r_k=r_k.reshape(-1), ln_x_w=ln_x_w, ln_x_b=ln_x_b,
        w_out=w_out.astype(BF16), g_pre_mix=g_pre_mix, g_post_mix=g_post_mix,
        g_pre_ffn=g_pre_ffn, g_post_ffn=g_post_ffn,
        w_up=w_up, conv_w=conv_w, conv_b=conv_b, w_down=w_down.astype(BF16),
        g_pe=g_pe, w_pe_gate=w_pe_gate.astype(BF16), w_pe=w_pe.astype(BF16),
    )
    rsrc = src["r"][0]
    rnames = ("r", "xw", "kr", "vr", "xa", "xg")
    prm["rnames"] = rnames
    prm["rsl"] = {n: (src[n][0] - rsrc, src[n][1]) for n in rnames}
    prm["mu"] = {n: _pad_cols(mu_shift[None, prm["rsl"][n][0]:prm["rsl"][n][0] + prm["rsl"][n][1]],
                              pad(prm["rsl"][n][1])) for n in rnames}
    return prm


def _pack_state(s, hd):
    B, H = s.shape[:2]
    return s.reshape(B, H // 2, 2, hd, hd).transpose(0, 1, 3, 2, 4).reshape(B, H // 2, hd, 2 * hd)


def _unpack_state(sp, hd):
    B, NP = sp.shape[:2]
    return sp.reshape(B, NP, hd, 2, hd).transpose(0, 1, 3, 2, 4).reshape(B, 2 * NP, hd, hd)


def _layer(x, p, s_gla, s_rwkv, s_shift, s_conv, prm):
    B, L, D = x.shape
    M = B * L
    dm, dst = prm["dims"], prm["off"]
    GW, RW, DV, HD, DFF, QK = dm["GW"], dm["RW"], dm["DV"], dm["HD"], dm["DFF"], dm["QK"]
    heads = GW // DV
    pad = lambda w: -(-w // LANE) * LANE
    x2 = x.reshape(M, D)

    proj = _norm_matmul(x2, prm["g_pre_mix"], prm["w_in_t"], w_is_transposed=True)
    proj3 = proj.reshape(B, L, -1)

    o_a, s_gla_new = _gla(proj3, dst, prm["w_alpha2"], prm["b_alpha"], prm["gla_norm"], s_gla,
                          dk=QK // heads, dv=DV, heads=heads)

    shift_parts = {n: _pad_cols(s_shift[:, None, prm["rsl"][n][0]:prm["rsl"][n][0] + prm["rsl"][n][1]],
                                pad(prm["rsl"][n][1])) for n in prm["rnames"]}
    r, lw, kr, vr, a, g = _rwkv_prep(proj3, dst, shift_parts, prm["mu"], prm["w0"], prm["w_decay2"],
                                     prm["a0"], prm["w_iclr2"], prm["w_gate2"], rw=RW, gl=dm["GL"])
    o_b, s_rwkv_new = _rwkv_chunk(r, lw, kr, vr, a, g, prm["k_k"], prm["k_a"], prm["r_k"],
                                  prm["ln_x_w"], prm["ln_x_b"], _pack_state(s_rwkv, HD), hd=HD)
    new_shift = proj3[:, L - 1, dst["r"]:dst["ga"]]

    mixed = _merge(o_a.reshape(M, GW), o_b.reshape(M, RW), prm["w_branch_a"], prm["w_branch_b"],
                   proj, dst["ga"], dst["gb"])
    x2 = _matmul_norm_residual(mixed, prm["w_out"], x2, prm["g_post_mix"], tm=512, tn=512)

    if L >= 256:
        act, new_conv = _up_conv_act(x2, prm["g_pre_ffn"], prm["w_up"], s_conv, prm["conv_w"],
                                     prm["conv_b"], seq_len=L)
    else:
        up = _norm_matmul(x2, prm["g_pre_ffn"], prm["w_up"])
        up3 = up.reshape(B, L, 2 * DFF)
        act = _conv_act(up3, s_conv, prm["conv_w"], prm["conv_b"], dff=DFF)
        new_conv = up3[:, L - 2:, :DFF]
    x2 = _matmul_norm_residual(act.reshape(M, DFF), prm["w_down"], x2, prm["g_post_ffn"], tm=512, tn=256)

    x2 = _pe_layer(x2, prm["g_pe"], prm["w_pe_gate"], p.reshape(M, -1), prm["w_pe"])
    return x2.reshape(B, L, D), s_gla_new, _unpack_state(s_rwkv_new, HD), new_shift, new_conv


def kernel(x_prompt, x_sample, state_gla, state_rwkv, state_shift, state_ffn_conv, p_prompt, p_sample, w_in, w_alpha2, b_alpha, gla_norm, w_branch_a, mu_shift, w0, w_decay2, a0, w_iclr2, w_gate2, k_k, k_a, r_k, ln_x_w, ln_x_b, w_branch_b, w_out, g_pre_mix, g_post_mix, g_pre_ffn, g_post_ffn, w_up, conv_w, conv_b, w_down, g_pe, w_pe_gate, w_pe):
    params = (w_in, w_alpha2, b_alpha, gla_norm, w_branch_a,
              mu_shift, w0, w_decay2, a0, w_iclr2, w_gate2, k_k, k_a, r_k, ln_x_w, ln_x_b, w_branch_b,
              w_out, g_pre_mix, g_post_mix, g_pre_ffn, g_post_ffn,
              w_up, conv_w, conv_b, w_down, g_pe, w_pe_gate, w_pe)
    depth = w_in.shape[0]
    nb = x_prompt.shape[0]
    yp, ys = x_prompt, x_sample
    outs_p = [[], [], [], []]
    outs_s = [[], [], [], []]
    for i in range(depth):
        prm = _prepare_params(tuple(t[i] for t in params))
        z_gla = jnp.zeros((nb,) + state_gla.shape[2:], F32)
        z_rwkv = jnp.zeros((nb,) + state_rwkv.shape[2:], F32)
        z_shift = jnp.zeros((nb,) + state_shift.shape[2:], x_prompt.dtype)
        z_conv = jnp.zeros((nb,) + state_ffn_conv.shape[2:], x_prompt.dtype)
        yp, *st = _layer(yp, p_prompt[i], z_gla, z_rwkv, z_shift, z_conv, prm)
        for acc, s in zip(outs_p, st):
            acc.append(s)
        ys, *st = _layer(ys, p_sample[i], state_gla[i], state_rwkv[i], state_shift[i],
                         state_ffn_conv[i], prm)
        for acc, s in zip(outs_s, st):
            acc.append(s)
    return (yp, ys, *(jnp.stack(a) for a in outs_p), *(jnp.stack(a) for a in outs_s))
```

```python
import functools
import math

import jax
import jax.numpy as jnp
from jax import lax
from jax.experimental import pallas as pl
from jax.experimental.pallas import tpu as pltpu

F32 = jnp.float32
BF16 = jnp.bfloat16

LANE = 128
SUBLANE = 8
VMEM_LIMIT_BYTES = 56 * 1024 * 1024

EPS = 1e-6
GN_EPS = 64e-5
GLA_TAU = 16.0
GLA_CHUNK = 64
RWKV_CHUNK = 64
KK_EPS = 1e-12
DECAY_SCALE = math.exp(-0.5)
GELU_C = math.sqrt(2.0 / math.pi)


ROW_RESIDENT = pl.Buffered(1)


def _cparams(sem):
    return pltpu.CompilerParams(dimension_semantics=sem, vmem_limit_bytes=VMEM_LIMIT_BYTES)


def _sigmoid(x):
    return 1.0 / (1.0 + jnp.exp(-x))


def _log_sigmoid(x):
    return jnp.minimum(x, 0.0) - jnp.log(1.0 + jnp.exp(-jnp.abs(x)))


def _gelu_tanh(x):
    return 0.5 * x * (1.0 + jnp.tanh(GELU_C * (x + 0.044715 * (x * x * x))))


def _split2(x):
    hi = x.astype(BF16)
    lo = (x - hi.astype(F32)).astype(BF16)
    return hi, lo


def _dot(a, b):
    return jnp.dot(a, b, preferred_element_type=F32)


def _dot_nt(a, b):
    return lax.dot_general(a, b, (((1,), (1,)), ((), ())), preferred_element_type=F32)


def _dot_tn(a, b):
    return lax.dot_general(a, b, (((0,), (0,)), ((), ())), preferred_element_type=F32)


def _ones_dot(m01, x):
    hi, lo = _split2(x)
    return _dot(m01, hi) + _dot(m01, lo)


def _dot_ones(x, m01):
    hi, lo = _split2(x)
    return _dot(hi, m01) + _dot(lo, m01)


def _lockstep(gens):
    gens = list(gens)
    while gens:
        alive = []
        for g in gens:
            try:
                next(g)
                alive.append(g)
            except StopIteration:
                pass
        gens = alive


def _tri_incl(n):
    r = lax.broadcasted_iota(jnp.int32, (n, n), 0)
    c = lax.broadcasted_iota(jnp.int32, (n, n), 1)
    return r >= c


def _pick_tile(n, target, mult):
    if n <= target:
        return n
    best = None
    t = mult
    while t <= target:
        if n % t == 0:
            best = t
        t += mult
    assert best is not None, (n, target, mult)
    return best


def _rms_rows(x_ref, g_ref, h_ref, rows):
    tm = x_ref.shape[0]

    def body(i, _):
        sl = pl.ds(pl.multiple_of(i * rows, rows), rows)
        x = x_ref[sl, :]
        ms = jnp.mean(x * x, axis=-1, keepdims=True)
        h_ref[sl, :] = (x * lax.rsqrt(ms + EPS) * g_ref[...]).astype(BF16)
        return 0

    lax.fori_loop(0, tm // rows, body, 0)


def _norm_mm_kernel(x_ref, g_ref, w_ref, o_ref, h_ref, *, rows, w_is_transposed):
    @pl.when(pl.program_id(1) == 0)
    def _():
        _rms_rows(x_ref, g_ref, h_ref, rows)

    w = w_ref[...].astype(BF16)
    o_ref[...] = _dot_nt(h_ref[...], w) if w_is_transposed else _dot(h_ref[...], w)


def _window(src, width):
    base = src // LANE * LANE
    shift = src - base
    wp = -(-width // LANE) * LANE
    return base, shift, (wp if shift == 0 else wp + LANE), wp


def _norm_matmul(x, gain, w, *, w_is_transposed=False, tm=1024, tn=512):
    M, K = x.shape
    nw = w.shape[0] if w_is_transposed else w.shape[1]
    tm = _pick_tile(M, tm, SUBLANE)
    tn = min(tn, -(-nw // LANE) * LANE)
    N = -(-nw // tn) * tn
    rows = _pick_tile(tm, 64, SUBLANE)
    w_spec = (pl.BlockSpec((tn, K), lambda i, j: (j, 0)) if w_is_transposed
              else pl.BlockSpec((K, tn), lambda i, j: (0, j)))
    return pl.pallas_call(
        functools.partial(_norm_mm_kernel, rows=rows, w_is_transposed=w_is_transposed),
        grid=(M // tm, N // tn),
        in_specs=[pl.BlockSpec((tm, K), lambda i, j: (i, 0), pipeline_mode=ROW_RESIDENT),
                  pl.BlockSpec((1, K), lambda i, j: (0, 0)),
                  w_spec],
        out_specs=pl.BlockSpec((tm, tn), lambda i, j: (i, j)),
        out_shape=jax.ShapeDtypeStruct((M, N), F32),
        scratch_shapes=[pltpu.VMEM((tm, K), BF16)],
        compiler_params=_cparams(("parallel", "arbitrary")),
        name="norm_matmul",
    )(x, gain.reshape(1, K), w)


def _pe_kernel(x_ref, g_ref, w_ref, p_ref, wp_ref, o_ref, h_ref, *, rows, tn):
    j = pl.program_id(1)

    @pl.when(j == 0)
    def _():
        _rms_rows(x_ref, g_ref, h_ref, rows)

    gate = _sigmoid(_dot(h_ref[...], w_ref[...].astype(BF16)))
    pe = _dot(p_ref[...].astype(BF16), wp_ref[...])
    xs = x_ref[:, pl.ds(pl.multiple_of(j * tn, tn), tn)]
    o_ref[...] = xs + gate * pe


def _pe_layer(x, gain, w_gate, p, w_pe, *, tm=1024, tn=512):
    M, K = x.shape
    N = w_gate.shape[1]
    P = p.shape[1]
    tm = _pick_tile(M, tm, SUBLANE)
    tn = _pick_tile(N, tn, LANE)
    rows = _pick_tile(tm, 64, SUBLANE)
    return pl.pallas_call(
        functools.partial(_pe_kernel, rows=rows, tn=tn),
        grid=(M // tm, N // tn),
        in_specs=[pl.BlockSpec((tm, K), lambda i, j: (i, 0), pipeline_mode=ROW_RESIDENT),
                  pl.BlockSpec((1, K), lambda i, j: (0, 0)),
                  pl.BlockSpec((K, tn), lambda i, j: (0, j)),
                  pl.BlockSpec((tm, P), lambda i, j: (i, 0)),
                  pl.BlockSpec((P, tn), lambda i, j: (0, j))],
        out_specs=pl.BlockSpec((tm, tn), lambda i, j: (i, j)),
        out_shape=jax.ShapeDtypeStruct((M, N), F32),
        scratch_shapes=[pltpu.VMEM((tm, K), BF16)],
        compiler_params=_cparams(("parallel", "arbitrary")),
        name="pe_layer",
    )(x, gain.reshape(1, K), w_gate, p, w_pe)


def _mm_norm_res_kernel(a_ref, w_ref, x_ref, g_ref, o_ref, *, tn, rows):
    j = pl.program_id(1)
    o_ref[:, pl.ds(pl.multiple_of(j * tn, tn), tn)] = _dot(a_ref[...], w_ref[...])

    @pl.when(j == pl.num_programs(1) - 1)
    def _():
        tm = o_ref.shape[0]

        def body(i, _):
            sl = pl.ds(pl.multiple_of(i * rows, rows), rows)
            y = o_ref[sl, :]
            ms = jnp.mean(y * y, axis=-1, keepdims=True)
            o_ref[sl, :] = x_ref[sl, :] + y * lax.rsqrt(ms + EPS) * g_ref[...]
            return 0

        lax.fori_loop(0, tm // rows, body, 0)


def _matmul_norm_residual(a, w, x, gain, *, tm, tn, single_buffer_rows):
    M, K = a.shape
    N = w.shape[1]
    tm = _pick_tile(M, tm, SUBLANE)
    tn = _pick_tile(N, tn, LANE)
    rows = _pick_tile(tm, 64, SUBLANE)
    mode = dict(pipeline_mode=ROW_RESIDENT) if single_buffer_rows else {}
    return pl.pallas_call(
        functools.partial(_mm_norm_res_kernel, tn=tn, rows=rows),
        grid=(M // tm, N // tn),
        in_specs=[pl.BlockSpec((tm, K), lambda i, j: (i, 0), **mode),
                  pl.BlockSpec((K, tn), lambda i, j: (0, j)),
                  pl.BlockSpec((tm, N), lambda i, j: (i, 0), **mode),
                  pl.BlockSpec((1, N), lambda i, j: (0, 0))],
        out_specs=pl.BlockSpec((tm, N), lambda i, j: (i, 0)),
        out_shape=jax.ShapeDtypeStruct((M, N), F32),
        compiler_params=_cparams(("parallel", "arbitrary")),
        name="matmul_norm_residual",
    )(a, w, x, gain.reshape(1, N))


def _merge_kernel(oa_ref, ob_ref, wa_ref, wb_ref, ga_ref, gb_ref, o_ref, *, sa, sb, row_groups):
    tm, tn = o_ref.shape
    wa = wa_ref[...]
    wb = wb_ref[...]
    tr = tm // row_groups
    for r in range(row_groups):
        rs = slice(r * tr, (r + 1) * tr)
        ya = _dot(oa_ref[rs, :], wa)
        yb = _dot(ob_ref[rs, :], wb)
        ga = ga_ref[rs, sa:sa + tn]
        gb = gb_ref[rs, sb:sb + tn]
        o_ref[rs, :] = (_sigmoid(ga) * ya + _sigmoid(gb) * yb).astype(BF16)


def _gate_window_spec(tm, tn, off, n_cols):
    base, shift, _, _ = _window(off, tn)
    ww = tn if shift == 0 else tn + LANE
    spec = pl.BlockSpec((pl.Element(tm), pl.Element(ww)),
                        lambda i, j: (pl.multiple_of(i * tm, SUBLANE), pl.multiple_of(base + j * tn, LANE)))
    return spec, shift, base + n_cols - tn + ww


def _merge(oa, ob, wa, wb, proj, ga_off, gb_off, *, tm=512, tn=1024):
    M, KA = oa.shape
    KB = ob.shape[1]
    N = wa.shape[1]
    tm = _pick_tile(M, tm, SUBLANE)
    tn = _pick_tile(N, tn, LANE)
    ga_spec, sa, enda = _gate_window_spec(tm, tn, ga_off, N)
    gb_spec, sb, endb = _gate_window_spec(tm, tn, gb_off, N)
    assert max(enda, endb) <= proj.shape[1]
    return pl.pallas_call(
        functools.partial(_merge_kernel, sa=sa, sb=sb, row_groups=4 if tm % (4 * SUBLANE) == 0 else 1),
        grid=(M // tm, N // tn),
        in_specs=[pl.BlockSpec((tm, KA), lambda i, j: (i, 0)),
                  pl.BlockSpec((tm, KB), lambda i, j: (i, 0)),
                  pl.BlockSpec((KA, tn), lambda i, j: (0, j)),
                  pl.BlockSpec((KB, tn), lambda i, j: (0, j)),
                  ga_spec, gb_spec],
        out_specs=pl.BlockSpec((tm, tn), lambda i, j: (i, j)),
        out_shape=jax.ShapeDtypeStruct((M, N), BF16),
        compiler_params=_cparams(("parallel", "arbitrary")),
        name="merge",
    )(oa, ob, wa, wb, proj, proj)


def _gla_kernel(q_ref, k_ref, v_ref, za_ref, zg_ref, wa_ref, ba_ref, gn_ref, s0_ref,
                o_ref, sn_ref, s_ref, *, scale, c, heads, zshift):
    l = pl.program_id(1)

    @pl.when(l == 0)
    def _():
        s_ref[...] = s0_ref[0]

    dk = q_ref.shape[2] // heads
    dv = v_ref.shape[2] // heads
    x = _dot(za_ref[0].astype(BF16), wa_ref[...]) + ba_ref[...]
    la_all = _log_sigmoid(x) * (1.0 / GLA_TAU)
    tri = _tri_incl(c)
    b_all = _ones_dot(jnp.where(tri, 1.0, 0.0).astype(BF16), la_all)
    ones = jnp.ones((c, LANE), BF16)
    def head(h):
        ks = slice(h * dk, (h + 1) * dk)
        vs = slice(h * dv, (h + 1) * dv)
        la = la_all[:, ks]
        b = b_all[:, ks]
        b_end = b[c - 1:c, :]
        q = q_ref[0, :, ks] * scale
        k = k_ref[0, :, ks]
        v = v_ref[0, :, vs].astype(BF16)
        qd = (q * jnp.exp(b)).astype(BF16)
        kd = (k * jnp.exp(-b)).astype(BF16)
        s = s_ref[h]
        att = _dot_nt(qd, kd)
        o = _dot(qd, s.astype(BF16))
        k_end = (k * jnp.exp(b_end - b)).astype(BF16)
        la_hi, la_lo = _split2(la)
        dec = jnp.exp(_dot_tn(la_hi, ones) + _dot_tn(la_lo, ones))
        dec = jnp.concatenate([dec] * (dv // LANE), axis=1)
        s_ref[h] = s * dec + _dot_tn(k_end, v)
        yield
        o = o + _dot(jnp.where(tri, att, 0.0).astype(BF16), v)
        yield
        ms = jnp.mean(o * o, axis=-1, keepdims=True)
        on = o * lax.rsqrt(ms + EPS) * gn_ref[...]
        zg = zg_ref[0, :, zshift + h * dv:zshift + (h + 1) * dv]
        o_ref[0, :, vs] = (on * (zg * _sigmoid(zg))).astype(BF16)

    _lockstep([head(h) for h in range(heads)])

    @pl.when(l == pl.num_programs(1) - 1)
    def _():
        sn_ref[0] = s_ref[...]


def _gla(proj3, offs, w_alpha2p, b_alpha, gla_norm, s0, *, dk, dv, heads):
    B, L, _ = proj3.shape
    c = min(GLA_CHUNK, L)
    qk, gw = heads * dk, heads * dv
    assert L % c == 0 and dv % LANE == 0
    for name, wdt in (("q", qk), ("k", qk), ("v", gw), ("za", LANE)):
        assert offs[name] % wdt == 0
    jq, jk, jv, jz = offs["q"] // qk, offs["k"] // qk, offs["v"] // gw, offs["za"] // LANE
    zbase, zshift, zww, _ = _window(offs["zg"], gw)
    assert zbase + zww <= proj3.shape[2]
    return pl.pallas_call(
        functools.partial(_gla_kernel, scale=dk ** -0.5, c=c, heads=heads, zshift=zshift),
        grid=(B, L // c),
        in_specs=[pl.BlockSpec((1, c, qk), lambda b, l: (b, l, jq)),
                  pl.BlockSpec((1, c, qk), lambda b, l: (b, l, jk)),
                  pl.BlockSpec((1, c, gw), lambda b, l: (b, l, jv)),
                  pl.BlockSpec((1, c, LANE), lambda b, l: (b, l, jz)),
                  pl.BlockSpec((pl.Element(1), pl.Element(c), pl.Element(zww)),
                               lambda b, l: (b, pl.multiple_of(l * c, c), zbase)),
                  pl.BlockSpec((LANE, qk), lambda b, l: (0, 0)),
                  pl.BlockSpec((1, qk), lambda b, l: (0, 0)),
                  pl.BlockSpec((1, dv), lambda b, l: (0, 0)),
                  pl.BlockSpec((1, heads, dk, dv), lambda b, l: (b, 0, 0, 0))],
        out_specs=[pl.BlockSpec((1, c, gw), lambda b, l: (b, l, 0)),
                   pl.BlockSpec((1, heads, dk, dv), lambda b, l: (b, 0, 0, 0))],
        out_shape=[jax.ShapeDtypeStruct((B, L, gw), BF16),
                   jax.ShapeDtypeStruct((B, heads, dk, dv), F32)],
        scratch_shapes=[pltpu.VMEM((heads, dk, dv), F32)],
        compiler_params=_cparams(("parallel", "arbitrary")),
        name="gla",
    )(proj3, proj3, proj3, proj3, proj3, w_alpha2p, b_alpha.reshape(1, -1), gla_norm.reshape(1, dv), s0)


def _shifted(cur, first_row):
    rolled = pltpu.roll(cur, 1, 0)
    row = lax.broadcasted_iota(jnp.int32, cur.shape, 0)
    return jnp.where(row == 0, first_row, rolled)


def _rwkv_prep_kernel(r_ref, xw_ref, kr_ref, vr_ref, xa_ref, xg_ref,
                      sr_ref, sxw_ref, skr_ref, svr_ref, sxa_ref, sxg_ref,
                      mr_ref, mxw_ref, mkr_ref, mvr_ref, mxa_ref, mxg_ref,
                      w0_ref, wd_ref, a0_ref, wi_ref, wg_ref,
                      ro_ref, lw_ref, ko_ref, vo_ref, ao_ref, go_ref,
                      cr_ref, cxw_ref, ckr_ref, cvr_ref, cxa_ref, cxg_ref, *, shifts):
    l = pl.program_id(1)
    tl = r_ref.shape[1]
    sh_r, sh_xw, sh_kr, sh_vr, sh_xa, sh_xg = shifts

    def shift_mix(z_ref, s0, s_ref, m_ref, c_ref):
        cur = z_ref[0, :, s0:s0 + m_ref.shape[1]]
        first = jnp.where(l == 0, s_ref[0], c_ref[...])
        prev = _shifted(cur, first)
        c_ref[...] = cur[tl - 1:tl, :]
        return cur + (prev - cur) * m_ref[...]

    ro_ref[0] = shift_mix(r_ref, sh_r, sr_ref, mr_ref, cr_ref)
    ko_ref[0] = shift_mix(kr_ref, sh_kr, skr_ref, mkr_ref, ckr_ref)
    vo_ref[0] = shift_mix(vr_ref, sh_vr, svr_ref, mvr_ref, cvr_ref)
    xw = shift_mix(xw_ref, sh_xw, sxw_ref, mxw_ref, cxw_ref)
    xa = shift_mix(xa_ref, sh_xa, sxa_ref, mxa_ref, cxa_ref)
    xg = shift_mix(xg_ref, sh_xg, sxg_ref, mxg_ref, cxg_ref)
    z = w0_ref[...] + _dot(jnp.tanh(xw).astype(BF16), wd_ref[...])
    lw_ref[0] = -DECAY_SCALE * _sigmoid(z)
    ao_ref[0] = _sigmoid(a0_ref[...] + _dot(xa.astype(BF16), wi_ref[...]))
    go_ref[0] = _dot(_sigmoid(xg).astype(BF16), wg_ref[...])


def _rwkv_prep(proj3, offs, shift_parts, mu_parts, w0, w_decay2p, a0, w_iclr2p, w_gate2, *, rw, gl):
    B, L, _ = proj3.shape
    tl = _pick_tile(L, 256, SUBLANE)
    names = ("r", "xw", "kr", "vr", "xa", "xg")
    widths = {"r": rw, "xw": LANE, "kr": rw, "vr": rw, "xa": LANE, "xg": gl}
    in_specs, args, shifts = [], [], []
    for n in names:
        base, shift, ww, wp = _window(offs[n], widths[n])
        assert wp == widths[n] and base + ww <= proj3.shape[2]
        shifts.append(shift)
        in_specs.append(pl.BlockSpec(
            (pl.Element(1), pl.Element(tl), pl.Element(ww)),
            functools.partial(lambda b, l, base: (b, pl.multiple_of(l * tl, SUBLANE), base), base=base)))
        args.append(proj3)
    for n in names:
        in_specs.append(pl.BlockSpec((1, 1, widths[n]), lambda b, l: (b, 0, 0)))
        args.append(shift_parts[n])
    for n in names:
        in_specs.append(pl.BlockSpec((1, widths[n]), lambda b, l: (0, 0)))
        args.append(mu_parts[n])
    for arr in (w0.reshape(1, rw), w_decay2p, a0.reshape(1, rw), w_iclr2p, w_gate2):
        in_specs.append(pl.BlockSpec(arr.shape, lambda b, l: (0, 0)))
        args.append(arr)
    out_spec = pl.BlockSpec((1, tl, rw), lambda b, l: (b, l, 0))
    out_sd = jax.ShapeDtypeStruct((B, L, rw), F32)
    return pl.pallas_call(
        functools.partial(_rwkv_prep_kernel, shifts=tuple(shifts)),
        grid=(B, L // tl),
        in_specs=in_specs,
        out_specs=[out_spec] * 6,
        out_shape=[out_sd] * 6,
        scratch_shapes=[pltpu.VMEM((1, widths[n]), F32) for n in names],
        compiler_params=_cparams(("parallel", "arbitrary")),
        name="rwkv_prep",
    )(*args)


def _rwkv_chunk_kernel(r_ref, lw_ref, kr_ref, v_ref, a_ref, g_ref,
                       kk_w_ref, ka_w_ref, rk_w_ref, lnw_ref, lnb_ref, s0_ref,
                       o_ref, sn_ref, w_ref, *, c, hd, gp):
    l = pl.program_id(2)
    tl = r_ref.shape[1]
    n2 = 2 * c
    lane_c = lax.broadcasted_iota(jnp.int32, (c, LANE), 1)
    head0_c = lane_c < hd
    row2 = lax.broadcasted_iota(jnp.int32, (n2, n2), 0)
    col2 = lax.broadcasted_iota(jnp.int32, (n2, n2), 1)
    same = (2 * row2 + 1 - n2) * (2 * col2 + 1 - n2) > 0
    strict = jnp.logical_and(same, row2 > col2)
    incl = jnp.logical_and(same, row2 >= col2)
    eye = jnp.where(row2 == col2, 1.0, 0.0)
    tri = jnp.where(_tri_incl(c), 1.0, 0.0).astype(BF16)
    lr = lax.broadcasted_iota(jnp.int32, (LANE, LANE), 0)
    lc = lax.broadcasted_iota(jnp.int32, (LANE, LANE), 1)
    seg = jnp.where((2 * lr + 1 - LANE) * (2 * lc + 1 - LANE) > 0, 1.0, 0.0).astype(BF16)

    @pl.when(l == 0)
    def _():
        zero = jnp.zeros((hd, hd), F32)
        for gi in range(gp):
            s_e = s0_ref[0, 2 * gi]
            s_o = s0_ref[0, 2 * gi + 1]
            w_ref[gi] = jnp.concatenate([jnp.concatenate([s_e, zero], axis=1),
                                         jnp.concatenate([zero, s_o], axis=1)], axis=0)

    def stack(x):
        return jnp.concatenate([jnp.where(head0_c, x, 0.0), jnp.where(head0_c, 0.0, x)], axis=0)

    def unstack(x):
        return x[0:c] + x[c:n2]

    def chunk_pair(sl, gi):
        ls = slice(gi * LANE, (gi + 1) * LANE)
        r = r_ref[0, sl, ls]
        lw = lw_ref[0, sl, ls]
        kr = kr_ref[0, sl, ls]
        v = v_ref[0, sl, ls]
        a = a_ref[0, sl, ls]
        kk = kr * kk_w_ref[:, ls]
        kk_ss = _dot_ones(kk * kk, seg)
        cum = _ones_dot(tri, lw)
        yield
        kk = kk / jnp.maximum(jnp.sqrt(kk_ss), KK_EPS)
        k = kr * (1.0 + (a - 1.0) * ka_w_ref[:, ls])
        cum_end = cum[c - 1:c, :]
        e_neg = jnp.exp(-cum)
        e_rem = jnp.exp(cum_end - cum)
        al = kk * a
        al_t = stack(al * e_neg).astype(BF16)
        k_t = stack(k * e_neg).astype(BF16)
        be_t = stack(-kk * jnp.exp(cum - lw)).astype(BF16)
        r_t = stack(r * jnp.exp(cum)).astype(BF16)
        v_s = stack(v).astype(BF16)
        lhs = jnp.concatenate([be_t, r_t], axis=0)
        sc_a = _dot_nt(lhs, al_t)
        sc_k = _dot_nt(lhs, k_t)
        w = w_ref[gi]
        rd = _dot_nt(lhs, w.astype(BF16))
        yield
        l_a =jnp.where(strict, sc_a[0:n2], 0.0)
        l_k = jnp.where(strict, sc_k[0:n2], 0.0).astype(BF16)
        m_a = jnp.where(incl, sc_a[n2:2 * n2], 0.0).astype(BF16)
        m_k = jnp.where(incl, sc_k[n2:2 * n2], 0.0).astype(BF16)
        t_inv = eye + l_a
        lp = l_a
        rhs_u = rd[0:n2] + _dot(l_k, v_s)
        o_s = rd[n2:2 * n2] + _dot(m_k, v_s)
        bonus = _dot_ones(r * k * rk_w_ref[:, ls], seg) * v
        span = 1
        while 2 * span < c:
            lpb = lp.astype(BF16)
            lp = _dot(lpb, lpb)
            yield
            t_inv = t_inv + _dot(t_inv.astype(BF16), lp.astype(BF16))
            yield
            span *= 2
        u_s = _dot(t_inv.astype(BF16), rhs_u.astype(BF16))
        yield
        u_b = u_s.astype(BF16)
        o_s = o_s + _dot(m_a, u_b)
        uv = jnp.concatenate([u_b, v_s], axis=0)
        ak = jnp.concatenate([stack(al * e_rem), stack(k * e_rem)], axis=0).astype(BF16)
        w_ref[gi] = w * jnp.exp(cum_end) + _dot_tn(uv, ak)
        yield
        o = unstack(o_s)
        mu = _dot_ones(o, seg) * (1.0 / hd)
        yield
        d = o - mu
        var = _dot_ones(d * d, seg) * (1.0 / hd)
        yield
        on = d * lax.rsqrt(var + GN_EPS) * lnw_ref[:, ls] + lnb_ref[:, ls]
        o_ref[0, sl, ls] = ((on + bonus) * g_ref[0, sl, ls]).astype(BF16)

    def chunk(ci, _):
        sl = pl.ds(pl.multiple_of(ci * c, c), c)
        _lockstep([chunk_pair(sl, gi) for gi in range(gp)])
        return 0

    lax.fori_loop(0, tl // c, chunk, 0)

    @pl.when(l == pl.num_programs(2) - 1)
    def _():
        for gi in range(gp):
            wf = w_ref[gi]
            sn_ref[0, 2 * gi] = wf[0:hd, 0:hd]
            sn_ref[0, 2 * gi + 1] = wf[hd:2 * hd, hd:2 * hd]


def _rwkv_chunk(r, lw, kr, vr, a, g, k_k, k_a, r_k, ln_w, ln_b, s0p, *, hd):
    B, L, RW = r.shape
    assert 2 * hd == LANE and RW % LANE == 0
    c = min(RWKV_CHUNK, L)
    tl = _pick_tile(L, 512, c)
    assert L % c == 0 and tl % c == 0 and c % SUBLANE == 0
    npair = RW // LANE
    gp = _pick_tile(npair, 8 if L > c else 16, 1)
    seq = pl.BlockSpec((1, tl, gp * LANE), lambda b, p, l: (b, l, p))
    par = pl.BlockSpec((1, gp * LANE), lambda b, p, l: (0, p))
    st = pl.BlockSpec((1, 2 * gp, hd, hd), lambda b, p, l: (b, p, 0, 0))
    return pl.pallas_call(
        functools.partial(_rwkv_chunk_kernel, c=c, hd=hd, gp=gp),
        grid=(B, npair // gp, L // tl),
        in_specs=[seq] * 6 + [par] * 5 + [st],
        out_specs=[seq, st],
        out_shape=[jax.ShapeDtypeStruct((B, L, RW), BF16),
                   jax.ShapeDtypeStruct((B, 2 * npair, hd, hd), F32)],
        scratch_shapes=[pltpu.VMEM((gp, LANE, LANE), F32)],
        compiler_params=_cparams(("parallel", "parallel", "arbitrary")),
        name="rwkv_chunk",
    )(r, lw, kr, vr, a, g, k_k.reshape(1, RW), k_a.reshape(1, RW), r_k.reshape(1, RW),
      ln_w.reshape(1, RW), ln_b.reshape(1, RW), s0p)


def _conv_act_kernel(g_ref, v_ref, s_ref, cw_ref, cb_ref, o_ref, c_ref):
    l = pl.program_id(2)
    tl = g_ref.shape[1]
    g = g_ref[0]
    prev = jnp.where(l == 0, s_ref[0], c_ref[...])
    row = lax.broadcasted_iota(jnp.int32, g.shape, 0)
    g1 = jnp.where(row == 0, prev[1:2], pltpu.roll(g, 1, 0))
    g2 = jnp.where(row == 0, prev[0:1], jnp.where(row == 1, prev[1:2], pltpu.roll(g, 2, 0)))
    c_ref[...] = g[tl - 2:tl, :]
    conv = cb_ref[...] + g2 * cw_ref[0:1, :] + g1 * cw_ref[1:2, :] + g * cw_ref[2:3, :]
    o_ref[0] = (_gelu_tanh(conv) * v_ref[0]).astype(BF16)


def _conv_act(up3, s_conv, conv_w, conv_b, *, dff):
    B, L, _ = up3.shape
    assert conv_w.shape[0] == 3 and s_conv.shape[1] == 2 and L >= 2
    tl = _pick_tile(L, 2048, SUBLANE)
    tn = _pick_tile(dff, max(LANE, (1 << 20) // tl), LANE)
    nj = dff // tn
    return pl.pallas_call(
        _conv_act_kernel,
        grid=(B, nj, L // tl),
        in_specs=[pl.BlockSpec((1, tl, tn), lambda b, j, l: (b, l, j)),
                  pl.BlockSpec((1, tl, tn), lambda b, j, l: (b, l, nj + j)),
                  pl.BlockSpec((1, 2, tn), lambda b, j, l: (b, 0, j)),
                  pl.BlockSpec((3, tn), lambda b, j, l: (0, j)),
                  pl.BlockSpec((1, tn), lambda b, j, l: (0, j))],
        out_specs=pl.BlockSpec((1, tl, tn), lambda b, j, l: (b, l, j)),
        out_shape=jax.ShapeDtypeStruct((B, L, dff), BF16),
        scratch_shapes=[pltpu.VMEM((2, tn), F32)],
        compiler_params=_cparams(("parallel", "parallel", "arbitrary")),
        name="conv_act",
    )(up3, up3, s_conv, conv_w, conv_b.reshape(1, dff))


def _up_conv_kernel(x_ref, gn_ref, wg_ref, wv_ref, s_ref, cw_ref, cb_ref, act_ref, nc_ref, h_ref, c_ref,
                    *, rows, tiles_per_seq, row_groups):
    i, j = pl.program_id(0), pl.program_id(1)
    tm = x_ref.shape[0]

    @pl.when(j == 0)
    def _():
        _rms_rows(x_ref, gn_ref, h_ref, rows)

    wg = wg_ref[...].astype(BF16)
    wv = wv_ref[...].astype(BF16)
    first = (i % tiles_per_seq) == 0
    prev = jnp.where(first, s_ref[0], c_ref[j])
    tr = tm // row_groups
    row = lax.broadcasted_iota(jnp.int32, (tr, wg.shape[1]), 0)
    for r in range(row_groups):
        rs = slice(r * tr, (r + 1) * tr)
        h = h_ref[rs, :]
        g = _dot(h, wg)
        v = _dot(h, wv)
        g1 = jnp.where(row == 0, prev[1:2], pltpu.roll(g, 1, 0))
        g2 = jnp.where(row == 0, prev[0:1], jnp.where(row == 1, prev[1:2], pltpu.roll(g, 2, 0)))
        prev = g[tr - 2:tr, :]
        conv = cb_ref[...] + g2 * cw_ref[0:1, :] + g1 * cw_ref[1:2, :] + g * cw_ref[2:3, :]
        act_ref[rs, :] = (_gelu_tanh(conv) * v).astype(BF16)
    c_ref[j] = prev
    nc_ref[0] = prev


def _up_conv_act(x, gain, w_up, s_conv, conv_w, conv_b, *, seq_len, tm=1024, tn=256):
    M, K = x.shape
    dff = conv_b.shape[0]
    B = M // seq_len
    tm = _pick_tile(seq_len, tm, SUBLANE)
    tn = _pick_tile(dff, tn, LANE)
    nj = dff // tn
    tps = seq_len // tm
    rows = _pick_tile(tm, 64, SUBLANE)
    row_groups = 8 if tm % (8 * SUBLANE) == 0 else 1
    assert conv_w.shape[0] == 3 and s_conv.shape[1] == 2 and tm // row_groups >= 2
    act, tails = pl.pallas_call(
        functools.partial(_up_conv_kernel, rows=rows, tiles_per_seq=tps, row_groups=row_groups),
        grid=(M // tm, nj),
        in_specs=[pl.BlockSpec((tm, K), lambda i, j: (i, 0), pipeline_mode=ROW_RESIDENT),
                  pl.BlockSpec((1, K), lambda i, j: (0, 0)),
                  pl.BlockSpec((K, tn), lambda i, j: (0, j)),
                  pl.BlockSpec((K, tn), lambda i, j: (0, nj + j)),
                  pl.BlockSpec((1, 2, tn), lambda i, j: (i // tps, 0, j)),
                  pl.BlockSpec((3, tn), lambda i, j: (0, j)),
                  pl.BlockSpec((1, tn), lambda i, j: (0, j))],
        out_specs=[pl.BlockSpec((tm, tn), lambda i, j: (i, j)),
                   pl.BlockSpec((1, 2, tn), lambda i, j: (i, 0, j))],
        out_shape=[jax.ShapeDtypeStruct((M, dff), BF16),
                   jax.ShapeDtypeStruct((M // tm, 2, dff), F32)],
        scratch_shapes=[pltpu.VMEM((tm, K), BF16), pltpu.VMEM((nj, 2, tn), F32)],
        compiler_params=_cparams(("arbitrary", "arbitrary")),
        name="up_conv_act",
    )(x, gain.reshape(1, K), w_up, w_up, s_conv, conv_w, conv_b.reshape(1, dff))
    return act, tails[tps - 1::tps]


def _pad_cols(a, width):
    return a if a.shape[-1] == width else jnp.pad(a, [(0, 0)] * (a.ndim - 1) + [(0, width - a.shape[-1])])


def _pad_rows(a, height):
    return a if a.shape[0] == height else jnp.pad(a, [(0, height - a.shape[0])] + [(0, 0)] * (a.ndim - 1))


def _layout(D, QK, GW, R, RW, WL, AL, GL):
    src, o = {}, 0
    for name, w in (("q", QK), ("k", QK), ("v", GW), ("za", R), ("zg", GW), ("r", RW), ("xw", WL),
                    ("kr", RW), ("vr", RW), ("xa", AL), ("xg", GL), ("ga", D), ("gb", D)):
        src[name] = (o, w)
        o += w
    return src


def _prepare_params(lp):
    (w_in, w_alpha2, b_alpha, gla_norm, w_branch_a,
     mu_shift, w0, w_decay2, a0, w_iclr2, w_gate2, k_k, k_a, r_k, ln_x_w, ln_x_b, w_branch_b,
     w_out, g_pre_mix, g_post_mix, g_pre_ffn, g_post_ffn,
     w_up, conv_w, conv_b, w_down, g_pe, w_pe_gate, w_pe) = lp
    D = w_in.shape[0]
    R, QK = w_alpha2.shape
    GW = w_branch_a.shape[0]
    RW = w_branch_b.shape[0]
    WL, AL, GL = w_decay2.shape[0], w_iclr2.shape[0], w_gate2.shape[0]
    src = _layout(D, QK, GW, R, RW, WL, AL, GL)
    pad = lambda w: -(-w // LANE) * LANE
    prm = dict(
        dims=dict(D=D, R=R, QK=QK, GW=GW, RW=RW, WL=WL, AL=AL, GL=GL, DV=gla_norm.shape[0],
                  HD=r_k.shape[1], DFF=conv_b.shape[0]),
        src=src, off={n: src[n][0] for n in src},
        w_in_t=w_in.T,
        w_alpha2=_pad_rows(w_alpha2, LANE).astype(BF16), b_alpha=b_alpha, gla_norm=gla_norm,
        w_branch_a=w_branch_a.astype(BF16), w_branch_b=w_branch_b.astype(BF16),
        w0=w0, w_decay2=_pad_rows(w_decay2, LANE).astype(BF16), a0=a0,
        w_iclr2=_pad_rows(w_iclr2, LANE).astype(BF16), w_gate2=w_gate2.astype(BF16),
        k_k=k_k, k_a=k_a, r_k=r_k.reshape(-1), ln_x_w=ln_x_w, ln_x_b=ln_x_b,
        w_out=w_out.astype(BF16), g_pre_mix=g_pre_mix, g_post_mix=g_post_mix,
        g_pre_ffn=g_pre_ffn, g_post_ffn=g_post_ffn,
        w_up=w_up, conv_w=conv_w, conv_b=conv_b, w_down=w_down.astype(BF16),
        g_pe=g_pe, w_pe_gate=w_pe_gate, w_pe=w_pe.astype(BF16),
    )
    rsrc = src["r"][0]
    rnames = ("r", "xw", "kr", "vr", "xa", "xg")
    prm["rnames"] = rnames
    prm["rsl"] = {n: (src[n][0] - rsrc, src[n][1]) for n in rnames}
    prm["mu"] = {n: _pad_cols(mu_shift[None, prm["rsl"][n][0]:prm["rsl"][n][0] + prm["rsl"][n][1]],
                              pad(prm["rsl"][n][1])) for n in rnames}
    return prm


def _layer(x, p, s_gla, s_rwkv, s_shift, s_conv, prm):
    B, L, D = x.shape
    M = B * L
    dm, dst = prm["dims"], prm["off"]
    GW, RW, DV, HD, DFF, QK = dm["GW"], dm["RW"], dm["DV"], dm["HD"], dm["DFF"], dm["QK"]
    heads = GW // DV
    pad = lambda w: -(-w // LANE) * LANE
    x2 = x.reshape(M, D)

    proj = _norm_matmul(x2, prm["g_pre_mix"], prm["w_in_t"], w_is_transposed=True)
    proj3 = proj.reshape(B, L, -1)

    o_a, s_gla_new = _gla(proj3, dst, prm["w_alpha2"], prm["b_alpha"], prm["gla_norm"], s_gla,
                          dk=QK // heads, dv=DV, heads=heads)

    shift_parts = {n: _pad_cols(s_shift[:, None, prm["rsl"][n][0]:prm["rsl"][n][0] + prm["rsl"][n][1]],
                                pad(prm["rsl"][n][1])) for n in prm["rnames"]}
    r, lw, kr, vr, a, g = _rwkv_prep(proj3, dst, shift_parts, prm["mu"], prm["w0"], prm["w_decay2"],
                                     prm["a0"], prm["w_iclr2"], prm["w_gate2"], rw=RW, gl=dm["GL"])
    o_b, s_rwkv_new = _rwkv_chunk(r, lw, kr, vr, a, g, prm["k_k"], prm["k_a"], prm["r_k"],
                                  prm["ln_x_w"], prm["ln_x_b"], s_rwkv, hd=HD)
    new_shift = proj3[:, L - 1, dst["r"]:dst["ga"]]

    mixed = _merge(o_a.reshape(M, GW), o_b.reshape(M, RW), prm["w_branch_a"], prm["w_branch_b"],
                   proj, dst["ga"], dst["gb"])
    x2 = _matmul_norm_residual(mixed, prm["w_out"], x2, prm["g_post_mix"], tm=512, tn=512,
                               single_buffer_rows=False)

    if L >= 256:
        act, new_conv = _up_conv_act(x2, prm["g_pre_ffn"], prm["w_up"], s_conv, prm["conv_w"],
                                     prm["conv_b"], seq_len=L)
    else:
        up = _norm_matmul(x2, prm["g_pre_ffn"], prm["w_up"])
        up3 = up.reshape(B, L, 2 * DFF)
        act = _conv_act(up3, s_conv, prm["conv_w"], prm["conv_b"], dff=DFF)
        new_conv = up3[:, L - 2:, :DFF]
    x2 = _matmul_norm_residual(act.reshape(M, DFF), prm["w_down"], x2, prm["g_post_ffn"], tm=512, tn=256,
                               single_buffer_rows=True)

    x2 = _pe_layer(x2, prm["g_pe"], prm["w_pe_gate"], p.reshape(M, -1), prm["w_pe"])
    return x2.reshape(B, L, D), s_gla_new, s_rwkv_new, new_shift, new_conv


def kernel(x_prompt, x_sample, state_gla, state_rwkv, state_shift, state_ffn_conv, p_prompt, p_sample, w_in, w_alpha2, b_alpha, gla_norm, w_branch_a, mu_shift, w0, w_decay2, a0, w_iclr2, w_gate2, k_k, k_a, r_k, ln_x_w, ln_x_b, w_branch_b, w_out, g_pre_mix, g_post_mix, g_pre_ffn, g_post_ffn, w_up, conv_w, conv_b, w_down, g_pe, w_pe_gate, w_pe):
    params = (w_in, w_alpha2, b_alpha, gla_norm, w_branch_a,
              mu_shift, w0, w_decay2, a0, w_iclr2, w_gate2, k_k, k_a, r_k, ln_x_w, ln_x_b, w_branch_b,
              w_out, g_pre_mix, g_post_mix, g_pre_ffn, g_post_ffn,
              w_up, conv_w, conv_b, w_down, g_pe, w_pe_gate, w_pe)
    depth = w_in.shape[0]
    nb = x_prompt.shape[0]
    yp, ys = x_prompt, x_sample
    outs_p = [[], [], [], []]
    outs_s = [[], [], [], []]
    for i in range(depth):
        prm = _prepare_params(tuple(t[i] for t in params))
        z_gla = jnp.zeros((nb,) + state_gla.shape[2:], F32)
        z_rwkv = jnp.zeros((nb,) + state_rwkv.shape[2:], F32)
        z_shift = jnp.zeros((nb,) + state_shift.shape[2:], x_prompt.dtype)
        z_conv = jnp.zeros((nb,) + state_ffn_conv.shape[2:], x_prompt.dtype)
        yp, *st = _layer(yp, p_prompt[i], z_gla, z_rwkv, z_shift, z_conv, prm)
        for acc, s in zip(outs_p, st):
            acc.append(s)
        ys, *st = _layer(ys, p_sample[i], state_gla[i], state_rwkv[i], state_shift[i],
                         state_ffn_conv[i], prm)
        for acc, s in zip(outs_s, st):
            acc.append(s)
    return (yp, ys, *(jnp.stack(a) for a in outs_p), *(jnp.stack(a) for a in outs_s))
```

```python
import functools
import math

import jax
import jax.numpy as jnp
from jax import lax
from jax.experimental import pallas as pl
from jax.experimental.pallas import tpu as pltpu

F32 = jnp.float32
BF16 = jnp.bfloat16

LANE = 128
SUBLANE = 8
VMEM_LIMIT_BYTES = 56 * 1024 * 1024

EPS = 1e-6
GN_EPS = 64e-5
GLA_TAU = 16.0
GLA_CHUNK = 64
RWKV_CHUNK = 64
KK_EPS = 1e-12
DECAY_SCALE = math.exp(-0.5)
GELU_C = math.sqrt(2.0 / math.pi)


ROW_RESIDENT = pl.Buffered(1)


def _cparams(sem):
    return pltpu.CompilerParams(dimension_semantics=sem, vmem_limit_bytes=VMEM_LIMIT_BYTES)


def _sigmoid(x):
    return 1.0 / (1.0 + jnp.exp(-x))


def _log_sigmoid(x):
    return jnp.minimum(x, 0.0) - jnp.log(1.0 + jnp.exp(-jnp.abs(x)))


def _gelu_tanh(x):
    return 0.5 * x * (1.0 + jnp.tanh(GELU_C * (x + 0.044715 * (x * x * x))))


def _split2(x):
    hi = x.astype(BF16)
    lo = (x - hi.astype(F32)).astype(BF16)
    return hi, lo


def _dot(a, b):
    return jnp.dot(a, b, preferred_element_type=F32)


def _dot_nt(a, b):
    return lax.dot_general(a, b, (((1,), (1,)), ((), ())), preferred_element_type=F32)


def _dot_tn(a, b):
    return lax.dot_general(a, b, (((0,), (0,)), ((), ())), preferred_element_type=F32)


def _ones_dot(m01, x):
    hi, lo = _split2(x)
    return _dot(m01, hi) + _dot(m01, lo)


def _dot_ones(x, m01):
    hi, lo = _split2(x)
    return _dot(hi, m01) + _dot(lo, m01)


def _lockstep(gens):
    gens = list(gens)
    while gens:
        alive = []
        for g in gens:
            try:
                next(g)
                alive.append(g)
            except StopIteration:
                pass
        gens = alive


def _tri_incl(n):
    r = lax.broadcasted_iota(jnp.int32, (n, n), 0)
    c = lax.broadcasted_iota(jnp.int32, (n, n), 1)
    return r >= c


def _pick_tile(n, target, mult):
    if n <= target:
        return n
    best = None
    t = mult
    while t <= target:
        if n % t == 0:
            best = t
        t += mult
    assert best is not None, (n, target, mult)
    return best


def _rms_rows(x_ref, g_ref, h_ref, rows):
    tm = x_ref.shape[0]

    def body(i, _):
        sl = pl.ds(pl.multiple_of(i * rows, rows), rows)
        x = x_ref[sl, :]
        ms = jnp.mean(x * x, axis=-1, keepdims=True)
        h_ref[sl, :] = (x * lax.rsqrt(ms + EPS) * g_ref[...]).astype(BF16)
        return 0

    lax.fori_loop(0, tm // rows, body, 0)


def _norm_mm_kernel(x_ref, g_ref, wt_ref, o_ref, h_ref, *, rows, n_valid):
    j = pl.program_id(1)

    @pl.when(j == 0)
    def _():
        _rms_rows(x_ref, g_ref, h_ref, rows)

    y = _dot_nt(h_ref[...], wt_ref[...].astype(BF16))
    tn = y.shape[1]
    if n_valid % tn:
        col = lax.broadcasted_iota(jnp.int32, y.shape, 1) + j * tn
        y = jnp.where(col < n_valid, y, 0.0)
    o_ref[...] = y


def _window(src, width):
    base = src // LANE * LANE
    shift = src - base
    wp = -(-width // LANE) * LANE
    return base, shift, (wp if shift == 0 else wp + LANE), wp


def _norm_matmul_nt(x, gain, wt, *, tm=1024, tn=512):
    M, K = x.shape
    nw = wt.shape[0]
    tm = _pick_tile(M, tm, SUBLANE)
    tn = min(tn, -(-nw // LANE) * LANE)
    N = -(-nw // tn) * tn
    rows = _pick_tile(tm, 64, SUBLANE)
    return pl.pallas_call(
        functools.partial(_norm_mm_kernel, rows=rows, n_valid=nw),
        grid=(M // tm, N // tn),
        in_specs=[pl.BlockSpec((tm, K), lambda i, j: (i, 0), pipeline_mode=ROW_RESIDENT),
                  pl.BlockSpec((1, K), lambda i, j: (0, 0)),
                  pl.BlockSpec((tn, K), lambda i, j: (j, 0))],
        out_specs=pl.BlockSpec((tm, tn), lambda i, j: (i, j)),
        out_shape=jax.ShapeDtypeStruct((M, N), F32),
        scratch_shapes=[pltpu.VMEM((tm, K), BF16)],
        compiler_params=_cparams(("parallel", "arbitrary")),
        name="norm_matmul",
    )(x, gain.reshape(1, K), wt)


def _pe_kernel(x_ref, g_ref, w_ref, p_ref, wp_ref, o_ref, h_ref, *, rows, tn):
    j = pl.program_id(1)

    @pl.when(j == 0)
    def _():
        _rms_rows(x_ref, g_ref, h_ref, rows)

    gate = _sigmoid(_dot(h_ref[...], w_ref[...].astype(BF16)))
    pe = _dot(p_ref[...].astype(BF16), wp_ref[...])
    xs = x_ref[:, pl.ds(pl.multiple_of(j * tn, tn), tn)]
    o_ref[...] = xs + gate * pe


def _pe_layer(x, gain, w_gate, p, w_pe, *, tm=1024, tn=512):
    M, K = x.shape
    N = w_gate.shape[1]
    P = p.shape[1]
    tm = _pick_tile(M, tm, SUBLANE)
    tn = _pick_tile(N, tn, LANE)
    rows = _pick_tile(tm, 64, SUBLANE)
    return pl.pallas_call(
        functools.partial(_pe_kernel, rows=rows, tn=tn),
        grid=(M // tm, N // tn),
        in_specs=[pl.BlockSpec((tm, K), lambda i, j: (i, 0), pipeline_mode=ROW_RESIDENT),
                  pl.BlockSpec((1, K), lambda i, j: (0, 0)),
                  pl.BlockSpec((K, tn), lambda i, j: (0, j)),
                  pl.BlockSpec((tm, P), lambda i, j: (i, 0)),
                  pl.BlockSpec((P, tn), lambda i, j: (0, j))],
        out_specs=pl.BlockSpec((tm, tn), lambda i, j: (i, j)),
        out_shape=jax.ShapeDtypeStruct((M, N), F32),
        scratch_shapes=[pltpu.VMEM((tm, K), BF16)],
        compiler_params=_cparams(("parallel", "arbitrary")),
        name="pe_layer",
    )(x, gain.reshape(1, K), w_gate, p, w_pe)


def _mm_norm_res_kernel(a_ref, w_ref, x_ref, g_ref, o_ref, *, tn, rows):
    j = pl.program_id(1)
    o_ref[:, pl.ds(pl.multiple_of(j * tn, tn), tn)] = _dot(a_ref[...], w_ref[...])

    @pl.when(j == pl.num_programs(1) - 1)
    def _():
        tm = o_ref.shape[0]

        def body(i, _):
            sl = pl.ds(pl.multiple_of(i * rows, rows), rows)
            y = o_ref[sl, :]
            ms = jnp.mean(y * y, axis=-1, keepdims=True)
            o_ref[sl, :] = x_ref[sl, :] + y * lax.rsqrt(ms + EPS) * g_ref[...]
            return 0

        lax.fori_loop(0, tm // rows, body, 0)


def _matmul_norm_residual(a, w, x, gain, *, tm, tn, single_buffer_rows):
    M, K = a.shape
    N = w.shape[1]
    tm = _pick_tile(M, tm, SUBLANE)
    tn = _pick_tile(N, tn, LANE)
    rows = _pick_tile(tm, 64, SUBLANE)
    mode = dict(pipeline_mode=ROW_RESIDENT) if single_buffer_rows else {}
    return pl.pallas_call(
        functools.partial(_mm_norm_res_kernel, tn=tn, rows=rows),
        grid=(M // tm, N // tn),
        in_specs=[pl.BlockSpec((tm, K), lambda i, j: (i, 0), **mode),
                  pl.BlockSpec((K, tn), lambda i, j: (0, j)),
                  pl.BlockSpec((tm, N), lambda i, j: (i, 0), **mode),
                  pl.BlockSpec((1, N), lambda i, j: (0, 0))],
        out_specs=pl.BlockSpec((tm, N), lambda i, j: (i, 0)),
        out_shape=jax.ShapeDtypeStruct((M, N), F32),
        compiler_params=_cparams(("parallel", "arbitrary")),
        name="matmul_norm_residual",
    )(a, w, x, gain.reshape(1, N))


def _merge_kernel(oa_ref, ob_ref, wa_ref, wb_ref, ga_ref, gb_ref, o_ref, *, sa, sb):
    tn = o_ref.shape[1]
    ya = _dot(oa_ref[...], wa_ref[...])
    yb = _dot(ob_ref[...], wb_ref[...])
    ga = ga_ref[:, sa:sa + tn]
    gb = gb_ref[:, sb:sb + tn]
    o_ref[...] = (_sigmoid(ga) * ya + _sigmoid(gb) * yb).astype(BF16)


def _gate_window_spec(tm, tn, off, n_cols):
    base, shift, _, _ = _window(off, tn)
    ww = tn if shift == 0 else tn + LANE
    spec = pl.BlockSpec((pl.Element(tm), pl.Element(ww)),
                        lambda i, j: (pl.multiple_of(i * tm, SUBLANE), pl.multiple_of(base + j * tn, LANE)))
    return spec, shift, base + n_cols - tn + ww


def _merge(oa, ob, wa, wb, proj, ga_off, gb_off, *, tm=512, tn=1024):
    M, KA = oa.shape
    KB = ob.shape[1]
    N = wa.shape[1]
    tm = _pick_tile(M, tm, SUBLANE)
    tn = _pick_tile(N, tn, LANE)
    ga_spec, sa, enda = _gate_window_spec(tm, tn, ga_off, N)
    gb_spec, sb, endb = _gate_window_spec(tm, tn, gb_off, N)
    assert max(enda, endb) <= proj.shape[1]
    return pl.pallas_call(
        functools.partial(_merge_kernel, sa=sa, sb=sb),
        grid=(M // tm, N // tn),
        in_specs=[pl.BlockSpec((tm, KA), lambda i, j: (i, 0)),
                  pl.BlockSpec((tm, KB), lambda i, j: (i, 0)),
                  pl.BlockSpec((KA, tn), lambda i, j: (0, j)),
                  pl.BlockSpec((KB, tn), lambda i, j: (0, j)),
                  ga_spec, gb_spec],
        out_specs=pl.BlockSpec((tm, tn), lambda i, j: (i, j)),
        out_shape=jax.ShapeDtypeStruct((M, N), BF16),
        compiler_params=_cparams(("parallel", "arbitrary")),
        name="merge",
    )(oa, ob, wa, wb, proj, proj)


def _gla_kernel(q_ref, k_ref, v_ref, za_ref, zg_ref, wa_ref, ba_ref, gn_ref, s0_ref,
                o_ref, sn_ref, s_ref, *, scale, c, heads, zshift):
    l = pl.program_id(1)

    @pl.when(l == 0)
    def _():
        s_ref[...] = s0_ref[0]

    dk = q_ref.shape[2] // heads
    dv = v_ref.shape[2] // heads
    x = _dot(za_ref[0].astype(BF16), wa_ref[...]) + ba_ref[...]
    la_all = _log_sigmoid(x) * (1.0 / GLA_TAU)
    tri = _tri_incl(c)
    b_all = _ones_dot(jnp.where(tri, 1.0, 0.0).astype(BF16), la_all)
    ones = jnp.ones((c, LANE), BF16)
    def head(h):
        ks = slice(h * dk, (h + 1) * dk)
        vs = slice(h * dv, (h + 1) * dv)
        la = la_all[:, ks]
        b = b_all[:, ks]
        b_end = b[c - 1:c, :]
        q = q_ref[0, :, ks] * scale
        k = k_ref[0, :, ks]
        v = v_ref[0, :, vs].astype(BF16)
        qd = (q * jnp.exp(b)).astype(BF16)
        kd = (k * jnp.exp(-b)).astype(BF16)
        s = s_ref[h]
        att = _dot_nt(qd, kd)
        o = _dot(qd, s.astype(BF16))
        k_end = (k * jnp.exp(b_end - b)).astype(BF16)
        la_hi, la_lo = _split2(la)
        dec = jnp.exp(_dot_tn(la_hi, ones) + _dot_tn(la_lo, ones))
        dec = jnp.concatenate([dec] * (dv // LANE), axis=1)
        s_ref[h] = s * dec + _dot_tn(k_end, v)
        yield
        o = o + _dot(jnp.where(tri, att, 0.0).astype(BF16), v)
        yield
        ms = jnp.mean(o * o, axis=-1, keepdims=True)
        on = o * lax.rsqrt(ms + EPS) * gn_ref[...]
        zg = zg_ref[0, :, zshift + h * dv:zshift + (h + 1) * dv]
        o_ref[0, :, vs] = (on * (zg * _sigmoid(zg))).astype(BF16)

    _lockstep([head(h) for h in range(heads)])

    @pl.when(l == pl.num_programs(1) - 1)
    def _():
        sn_ref[0] = s_ref[...]


def _gla(proj3, offs, w_alpha2p, b_alpha, gla_norm, s0, *, dk, dv, heads):
    B, L, _ = proj3.shape
    c = min(GLA_CHUNK, L)
    qk, gw = heads * dk, heads * dv
    assert L % c == 0 and dv % LANE == 0
    for name, wdt in (("q", qk), ("k", qk), ("v", gw), ("za", LANE)):
        assert offs[name] % wdt == 0
    jq, jk, jv, jz = offs["q"] // qk, offs["k"] // qk, offs["v"] // gw, offs["za"] // LANE
    zbase, zshift, zww, _ = _window(offs["zg"], gw)
    assert zbase + zww <= proj3.shape[2]
    return pl.pallas_call(
        functools.partial(_gla_kernel, scale=dk ** -0.5, c=c, heads=heads, zshift=zshift),
        grid=(B, L // c),
        in_specs=[pl.BlockSpec((1, c, qk), lambda b, l: (b, l, jq)),
                  pl.BlockSpec((1, c, qk), lambda b, l: (b, l, jk)),
                  pl.BlockSpec((1, c, gw), lambda b, l: (b, l, jv)),
                  pl.BlockSpec((1, c, LANE), lambda b, l: (b, l, jz)),
                  pl.BlockSpec((pl.Element(1), pl.Element(c), pl.Element(zww)),
                               lambda b, l: (b, pl.multiple_of(l * c, c), zbase)),
                  pl.BlockSpec((LANE, qk), lambda b, l: (0, 0)),
                  pl.BlockSpec((1, qk), lambda b, l: (0, 0)),
                  pl.BlockSpec((1, dv), lambda b, l: (0, 0)),
                  pl.BlockSpec((1, heads, dk, dv), lambda b, l: (b, 0, 0, 0))],
        out_specs=[pl.BlockSpec((1, c, gw), lambda b, l: (b, l, 0)),
                   pl.BlockSpec((1, heads, dk, dv), lambda b, l: (b, 0, 0, 0))],
        out_shape=[jax.ShapeDtypeStruct((B, L, gw), BF16),
                   jax.ShapeDtypeStruct((B, heads, dk, dv), F32)],
        scratch_shapes=[pltpu.VMEM((heads, dk, dv), F32)],
        compiler_params=_cparams(("parallel", "arbitrary")),
        name="gla",
    )(proj3, proj3, proj3, proj3, proj3, w_alpha2p, b_alpha.reshape(1, -1), gla_norm.reshape(1, dv), s0)


def _rwkv_prep_kernel(r_ref, xw_ref, kr_ref, vr_ref, xa_ref, xg_ref,
                      sr_ref, sxw_ref, skr_ref, svr_ref, sxa_ref, sxg_ref,
                      mr_ref, mxw_ref, mkr_ref, mvr_ref, mxa_ref, mxg_ref,
                      w0_ref, wd_ref, a0_ref, wi_ref, wg_ref,
                      ro_ref, lw_ref, ko_ref, vo_ref, ao_ref, go_ref,
                      cr_ref, cxw_ref, ckr_ref, cvr_ref, cxa_ref, cxg_ref, *, shifts):
    l = pl.program_id(1)
    bb, tl = r_ref.shape[0], r_ref.shape[1]
    sh_r, sh_xw, sh_kr, sh_vr, sh_xa, sh_xg = shifts

    def shift_mix(z_ref, s0, s_ref, m_ref, c_ref):
        cur = z_ref[:, :, s0:s0 + m_ref.shape[1]]
        first = jnp.where(l == 0, s_ref[...], c_ref[...])
        row = lax.broadcasted_iota(jnp.int32, cur.shape, 1)
        prev = jnp.where(row == 0, first, pltpu.roll(cur, 1, 1))
        c_ref[...] = cur[:, tl - 1:tl, :]
        return cur + (prev - cur) * m_ref[...]

    def flat(x):
        return x.reshape(bb * tl, x.shape[2])

    ro_ref[...] = shift_mix(r_ref, sh_r, sr_ref, mr_ref, cr_ref)
    ko_ref[...] = shift_mix(kr_ref, sh_kr, skr_ref, mkr_ref, ckr_ref)
    vo_ref[...] = shift_mix(vr_ref, sh_vr, svr_ref, mvr_ref, cvr_ref)
    xw = flat(shift_mix(xw_ref, sh_xw, sxw_ref, mxw_ref, cxw_ref))
    xa = flat(shift_mix(xa_ref, sh_xa, sxa_ref, mxa_ref, cxa_ref))
    xg = flat(shift_mix(xg_ref, sh_xg, sxg_ref, mxg_ref, cxg_ref))
    z = w0_ref[...] + _dot(jnp.tanh(xw).astype(BF16), wd_ref[...])
    rw = z.shape[1]
    lw_ref[...] = (-DECAY_SCALE * _sigmoid(z)).reshape(bb, tl, rw)
    ao_ref[...] = _sigmoid(a0_ref[...] + _dot(xa.astype(BF16), wi_ref[...])).reshape(bb, tl, rw)
    go_ref[...] = _dot(_sigmoid(xg).astype(BF16), wg_ref[...]).reshape(bb, tl, rw)


def _rwkv_prep(proj3, offs, shift_parts, mu_parts, w0, w_decay2p, a0, w_iclr2p, w_gate2, *, rw, gl):
    B, L, _ = proj3.shape
    tl = _pick_tile(L, 256, SUBLANE)
    bb = _pick_tile(B, max(1, 256 // L), 1) if tl == L else 1
    names = ("r", "xw", "kr", "vr", "xa", "xg")
    widths = {"r": rw, "xw": LANE, "kr": rw, "vr": rw, "xa": LANE, "xg": gl}
    in_specs, args, shifts = [], [], []
    for n in names:
        base, shift, ww, wp = _window(offs[n], widths[n])
        assert wp == widths[n] and base + ww <= proj3.shape[2]
        shifts.append(shift)
        in_specs.append(pl.BlockSpec(
            (pl.Element(bb), pl.Element(tl), pl.Element(ww)),
            functools.partial(lambda b, l, base: (b * bb, pl.multiple_of(l * tl, SUBLANE), base), base=base)))
        args.append(proj3)
    for n in names:
        in_specs.append(pl.BlockSpec((bb, 1, widths[n]), lambda b, l: (b, 0, 0)))
        args.append(shift_parts[n])
    for n in names:
        in_specs.append(pl.BlockSpec((1, widths[n]), lambda b, l: (0, 0)))
        args.append(mu_parts[n])
    for arr in (w0.reshape(1, rw), w_decay2p, a0.reshape(1, rw), w_iclr2p, w_gate2):
        in_specs.append(pl.BlockSpec(arr.shape, lambda b, l: (0, 0)))
        args.append(arr)
    out_spec = pl.BlockSpec((bb, tl, rw), lambda b, l: (b, l, 0))
    out_sd = jax.ShapeDtypeStruct((B, L, rw), F32)
    return pl.pallas_call(
        functools.partial(_rwkv_prep_kernel, shifts=tuple(shifts)),
        grid=(B // bb, L // tl),
        in_specs=in_specs,
        out_specs=[out_spec] * 6,
        out_shape=[out_sd] * 6,
        scratch_shapes=[pltpu.VMEM((bb, 1, widths[n]), F32) for n in names],
        compiler_params=_cparams(("parallel", "arbitrary")),
        name="rwkv_prep",
    )(*args)


def _rwkv_chunk_kernel(r_ref, lw_ref, kr_ref, v_ref, a_ref, g_ref,
                       kk_w_ref, ka_w_ref, rk_w_ref, lnw_ref, lnb_ref, s0_ref,
                       o_ref, sn_ref, w_ref, *, c, hd, gp):
    l = pl.program_id(2)
    tl = r_ref.shape[1]
    n2 = 2 * c
    lane_c = lax.broadcasted_iota(jnp.int32, (c, LANE), 1)
    head0_c = lane_c < hd
    row2 = lax.broadcasted_iota(jnp.int32, (n2, n2), 0)
    col2 = lax.broadcasted_iota(jnp.int32, (n2, n2), 1)
    same = (2 * row2 + 1 - n2) * (2 * col2 + 1 - n2) > 0
    strict = jnp.logical_and(same, row2 > col2)
    incl = jnp.logical_and(same, row2 >= col2)
    eye = jnp.where(row2 == col2, 1.0, 0.0)
    tri = jnp.where(_tri_incl(c), 1.0, 0.0).astype(BF16)
    lr = lax.broadcasted_iota(jnp.int32, (LANE, LANE), 0)
    lc = lax.broadcasted_iota(jnp.int32, (LANE, LANE), 1)
    seg = jnp.where((2 * lr + 1 - LANE) * (2 * lc + 1 - LANE) > 0, 1.0, 0.0).astype(BF16)

    @pl.when(l == 0)
    def _():
        zero = jnp.zeros((hd, hd), F32)
        for gi in range(gp):
            s_e = s0_ref[0, 2 * gi]
            s_o = s0_ref[0, 2 * gi + 1]
            w_ref[gi] = jnp.concatenate([jnp.concatenate([s_e, zero], axis=1),
                                         jnp.concatenate([zero, s_o], axis=1)], axis=0)

    def stack(x):
        return jnp.concatenate([jnp.where(head0_c, x, 0.0), jnp.where(head0_c, 0.0, x)], axis=0)

    def unstack(x):
        return x[0:c] + x[c:n2]

    def chunk_pair(sl, gi):
        ls = slice(gi * LANE, (gi + 1) * LANE)
        r = r_ref[0, sl, ls]
        lw = lw_ref[0, sl, ls]
        kr = kr_ref[0, sl, ls]
        v = v_ref[0, sl, ls]
        a = a_ref[0, sl, ls]
        kk = kr * kk_w_ref[:, ls]
        kk_ss = _dot_ones(kk * kk, seg)
        cum = _ones_dot(tri, lw)
        yield
        kk = kk / jnp.maximum(jnp.sqrt(kk_ss), KK_EPS)
        k = kr * (1.0 + (a - 1.0) * ka_w_ref[:, ls])
        cum_end = cum[c - 1:c, :]
        e_neg = jnp.exp(-cum)
        e_rem = jnp.exp(cum_end - cum)
        al = kk * a
        al_t = stack(al * e_neg).astype(BF16)
        k_t = stack(k * e_neg).astype(BF16)
        be_t = stack(-kk * jnp.exp(cum - lw)).astype(BF16)
        r_t = stack(r * jnp.exp(cum)).astype(BF16)
        v_s = stack(v).astype(BF16)
        lhs = jnp.concatenate([be_t, r_t], axis=0)
        sc_a = _dot_nt(lhs, al_t)
        sc_k = _dot_nt(lhs, k_t)
        w = w_ref[gi]
        rd = _dot_nt(lhs, w.astype(BF16))
        yield
        l_a =jnp.where(strict, sc_a[0:n2], 0.0)
        l_k = jnp.where(strict, sc_k[0:n2], 0.0).astype(BF16)
        m_a = jnp.where(incl, sc_a[n2:2 * n2], 0.0).astype(BF16)
        m_k = jnp.where(incl, sc_k[n2:2 * n2], 0.0).astype(BF16)
        t_inv = eye + l_a
        lp = l_a
        rhs_u = rd[0:n2] + _dot(l_k, v_s)
        o_s = rd[n2:2 * n2] + _dot(m_k, v_s)
        bonus = _dot_ones(r * k * rk_w_ref[:, ls], seg) * v
        span = 1
        while 2 * span < c:
            lpb = lp.astype(BF16)
            lp = _dot(lpb, lpb)
            yield
            t_inv = t_inv + _dot(t_inv.astype(BF16), lp.astype(BF16))
            yield
            span *= 2
        u_s = _dot(t_inv.astype(BF16), rhs_u.astype(BF16))
        yield
        u_b = u_s.astype(BF16)
        o_s = o_s + _dot(m_a, u_b)
        uv = jnp.concatenate([u_b, v_s], axis=0)
        ak = jnp.concatenate([stack(al * e_rem), stack(k * e_rem)], axis=0).astype(BF16)
        w_ref[gi] = w * jnp.exp(cum_end) + _dot_tn(uv, ak)
        yield
        o = unstack(o_s)
        mu = _dot_ones(o, seg) * (1.0 / hd)
        yield
        d = o - mu
        var = _dot_ones(d * d, seg) * (1.0 / hd)
        yield
        on = d * lax.rsqrt(var + GN_EPS) * lnw_ref[:, ls] + lnb_ref[:, ls]
        o_ref[0, sl, ls] = ((on + bonus) * g_ref[0, sl, ls]).astype(BF16)

    def chunk(ci, _):
        sl = pl.ds(pl.multiple_of(ci * c, c), c)
        _lockstep([chunk_pair(sl, gi) for gi in range(gp)])
        return 0

    lax.fori_loop(0, tl // c, chunk, 0)

    @pl.when(l == pl.num_programs(2) - 1)
    def _():
        for gi in range(gp):
            wf = w_ref[gi]
            sn_ref[0, 2 * gi] = wf[0:hd, 0:hd]
            sn_ref[0, 2 * gi + 1] = wf[hd:2 * hd, hd:2 * hd]


def _rwkv_chunk(r, lw, kr, vr, a, g, k_k, k_a, r_k, ln_w, ln_b, s0p, *, hd):
    B, L, RW = r.shape
    assert 2 * hd == LANE and RW % LANE == 0
    c = min(RWKV_CHUNK, L)
    tl = _pick_tile(L, 512, c)
    assert L % c == 0 and tl % c == 0 and c % SUBLANE == 0
    npair = RW // LANE
    gp = _pick_tile(npair, 8 if L > c else 16, 1)
    seq = pl.BlockSpec((1, tl, gp * LANE), lambda b, p, l: (b, l, p))
    par = pl.BlockSpec((1, gp * LANE), lambda b, p, l: (0, p))
    st = pl.BlockSpec((1, 2 * gp, hd, hd), lambda b, p, l: (b, p, 0, 0))
    return pl.pallas_call(
        functools.partial(_rwkv_chunk_kernel, c=c, hd=hd, gp=gp),
        grid=(B, npair // gp, L // tl),
        in_specs=[seq] * 6 + [par] * 5 + [st],
        out_specs=[seq, st],
        out_shape=[jax.ShapeDtypeStruct((B, L, RW), BF16),
                   jax.ShapeDtypeStruct((B, 2 * npair, hd, hd), F32)],
        scratch_shapes=[pltpu.VMEM((gp, LANE, LANE), F32)],
        compiler_params=_cparams(("parallel", "parallel", "arbitrary")),
        name="rwkv_chunk",
    )(r, lw, kr, vr, a, g, k_k.reshape(1, RW), k_a.reshape(1, RW), r_k.reshape(1, RW),
      ln_w.reshape(1, RW), ln_b.reshape(1, RW), s0p)


def _up_conv_kernel(x_ref, gn_ref, wg_ref, wv_ref, s_ref, cw_ref, cb_ref, act_ref, nc_ref, h_ref, c_ref,
                    *, rows, tiles_per_seq, seqs):
    i, j = pl.program_id(0), pl.program_id(1)
    tm = x_ref.shape[0]
    tl = tm // seqs

    @pl.when(j == 0)
    def _():
        _rms_rows(x_ref, gn_ref, h_ref, rows)

    h = h_ref[...]
    g = _dot(h, wg_ref[...].astype(BF16))
    v = _dot(h, wv_ref[...].astype(BF16))
    tn = g.shape[1]
    g = g.reshape(seqs, tl, tn)
    first = (i % tiles_per_seq) == 0
    prev = jnp.where(first, s_ref[...], c_ref[j])
    row = lax.broadcasted_iota(jnp.int32, g.shape, 1)
    g1 = jnp.where(row == 0, prev[:, 1:2], pltpu.roll(g, 1, 1))
    g2 = jnp.where(row == 0, prev[:, 0:1], jnp.where(row == 1, prev[:, 1:2], pltpu.roll(g, 2, 1)))
    last2 = g[:, tl - 2:tl, :]
    c_ref[j] = last2
    nc_ref[...] = last2
    conv = cb_ref[...] + g2 * cw_ref[0:1, :] + g1 * cw_ref[1:2, :] + g * cw_ref[2:3, :]
    act_ref[...] = (_gelu_tanh(conv).reshape(tm, tn) * v).astype(BF16)


def _up_conv_act(x, gain, w_up, s_conv, conv_w, conv_b, *, seq_len, tm=1024, tn=256):
    M, K = x.shape
    dff = conv_b.shape[0]
    if seq_len >= tm:
        tm = _pick_tile(seq_len, tm, SUBLANE)
        seqs, tps = 1, seq_len // tm
    else:
        assert seq_len % SUBLANE == 0
        seqs, tps = _pick_tile(M // seq_len, tm // seq_len, 1), 1
        tm = seqs * seq_len
    tn = _pick_tile(dff, tn, LANE)
    nj = dff // tn
    rows = _pick_tile(tm, 64, SUBLANE)
    assert conv_w.shape[0] == 3 and s_conv.shape[1] == 2 and tm // seqs >= 2
    act, tails = pl.pallas_call(
        functools.partial(_up_conv_kernel, rows=rows, tiles_per_seq=tps, seqs=seqs),
        grid=(M // tm, nj),
        in_specs=[pl.BlockSpec((tm, K), lambda i, j: (i, 0), pipeline_mode=ROW_RESIDENT),
                  pl.BlockSpec((1, K), lambda i, j: (0, 0)),
                  pl.BlockSpec((K, tn), lambda i, j: (0, j)),
                  pl.BlockSpec((K, tn), lambda i, j: (0, nj + j)),
                  pl.BlockSpec((seqs, 2, tn), lambda i, j: (i // tps, 0, j)),
                  pl.BlockSpec((3, tn), lambda i, j: (0, j)),
                  pl.BlockSpec((1, tn), lambda i, j: (0, j))],
        out_specs=[pl.BlockSpec((tm, tn), lambda i, j: (i, j)),
                   pl.BlockSpec((seqs, 2, tn), lambda i, j: (i, 0, j))],
        out_shape=[jax.ShapeDtypeStruct((M, dff), BF16),
                   jax.ShapeDtypeStruct((M // tm * seqs, 2, dff), F32)],
        scratch_shapes=[pltpu.VMEM((tm, K), BF16), pltpu.VMEM((nj, seqs, 2, tn), F32)],
        compiler_params=_cparams(("arbitrary", "arbitrary")),
        name="up_conv_act",
    )(x, gain.reshape(1, K), w_up, w_up, s_conv, conv_w, conv_b.reshape(1, dff))
    return act, tails[tps - 1::tps]


def _pad_cols(a, width):
    return a if a.shape[-1] == width else jnp.pad(a, [(0, 0)] * (a.ndim - 1) + [(0, width - a.shape[-1])])


def _pad_rows(a, height):
    return a if a.shape[0] == height else jnp.pad(a, [(0, height - a.shape[0])] + [(0, 0)] * (a.ndim - 1))


def _layout(D, QK, GW, R, RW, WL, AL, GL):
    src, o = {}, 0
    for name, w in (("q", QK), ("k", QK), ("v", GW), ("za", R), ("zg", GW), ("r", RW), ("xw", WL),
                    ("kr", RW), ("vr", RW), ("xa", AL), ("xg", GL), ("ga", D), ("gb", D)):
        src[name] = (o, w)
        o += w
    return src


def _prepare_params(lp):
    (w_in, w_alpha2, b_alpha, gla_norm, w_branch_a,
     mu_shift, w0, w_decay2, a0, w_iclr2, w_gate2, k_k, k_a, r_k, ln_x_w, ln_x_b, w_branch_b,
     w_out, g_pre_mix, g_post_mix, g_pre_ffn, g_post_ffn,
     w_up, conv_w, conv_b, w_down, g_pe, w_pe_gate, w_pe) = lp
    D = w_in.shape[0]
    R, QK = w_alpha2.shape
    GW = w_branch_a.shape[0]
    RW = w_branch_b.shape[0]
    WL, AL, GL = w_decay2.shape[0], w_iclr2.shape[0], w_gate2.shape[0]
    src = _layout(D, QK, GW, R, RW, WL, AL, GL)
    pad = lambda w: -(-w // LANE) * LANE
    prm = dict(
        dims=dict(D=D, R=R, QK=QK, GW=GW, RW=RW, WL=WL, AL=AL, GL=GL, DV=gla_norm.shape[0],
                  HD=r_k.shape[1], DFF=conv_b.shape[0]),
        src=src, off={n: src[n][0] for n in src},
        w_in_t=w_in.T,
        w_alpha2=_pad_rows(w_alpha2, LANE).astype(BF16), b_alpha=b_alpha, gla_norm=gla_norm,
        w_branch_a=w_branch_a.astype(BF16), w_branch_b=w_branch_b.astype(BF16),
        w0=w0, w_decay2=_pad_rows(w_decay2, LANE).astype(BF16), a0=a0,
        w_iclr2=_pad_rows(w_iclr2, LANE).astype(BF16), w_gate2=w_gate2.astype(BF16),
        k_k=k_k, k_a=k_a, r_k=r_k.reshape(-1), ln_x_w=ln_x_w, ln_x_b=ln_x_b,
        w_out=w_out.astype(BF16), g_pre_mix=g_pre_mix, g_post_mix=g_post_mix,
        g_pre_ffn=g_pre_ffn, g_post_ffn=g_post_ffn,
        w_up=w_up, conv_w=conv_w, conv_b=conv_b, w_down=w_down.astype(BF16),
        g_pe=g_pe, w_pe_gate=w_pe_gate, w_pe=w_pe.astype(BF16),
    )
    rsrc = src["r"][0]
    rnames = ("r", "xw", "kr", "vr", "xa", "xg")
    prm["rnames"] = rnames
    prm["rsl"] = {n: (src[n][0] - rsrc, src[n][1]) for n in rnames}
    prm["mu"] = {n: _pad_cols(mu_shift[None, prm["rsl"][n][0]:prm["rsl"][n][0] + prm["rsl"][n][1]],
                              pad(prm["rsl"][n][1])) for n in rnames}
    return prm


def _layer(x, p, s_gla, s_rwkv, s_shift, s_conv, prm):
    B, L, D = x.shape
    M = B * L
    dm, dst = prm["dims"], prm["off"]
    GW, RW, DV, HD, DFF, QK = dm["GW"], dm["RW"], dm["DV"], dm["HD"], dm["DFF"], dm["QK"]
    heads = GW // DV
    pad = lambda w: -(-w // LANE) * LANE
    x2 = x.reshape(M, D)

    proj = _norm_matmul_nt(x2, prm["g_pre_mix"], prm["w_in_t"])
    proj3 = proj.reshape(B, L, -1)

    o_a, s_gla_new = _gla(proj3, dst, prm["w_alpha2"], prm["b_alpha"], prm["gla_norm"], s_gla,
                          dk=QK // heads, dv=DV, heads=heads)

    shift_parts = {n: _pad_cols(s_shift[:, None, prm["rsl"][n][0]:prm["rsl"][n][0] + prm["rsl"][n][1]],
                                pad(prm["rsl"][n][1])) for n in prm["rnames"]}
    r, lw, kr, vr, a, g = _rwkv_prep(proj3, dst, shift_parts, prm["mu"], prm["w0"], prm["w_decay2"],
                                     prm["a0"], prm["w_iclr2"], prm["w_gate2"], rw=RW, gl=dm["GL"])
    o_b, s_rwkv_new = _rwkv_chunk(r, lw, kr, vr, a, g, prm["k_k"], prm["k_a"], prm["r_k"],
                                  prm["ln_x_w"], prm["ln_x_b"], s_rwkv, hd=HD)
    new_shift = proj3[:, L - 1, dst["r"]:dst["ga"]]

    mixed = _merge(o_a.reshape(M, GW), o_b.reshape(M, RW), prm["w_branch_a"], prm["w_branch_b"],
                   proj, dst["ga"], dst["gb"])
    x2 = _matmul_norm_residual(mixed, prm["w_out"], x2, prm["g_post_mix"], tm=512, tn=512,
                               single_buffer_rows=False)

    act, new_conv = _up_conv_act(x2, prm["g_pre_ffn"], prm["w_up"], s_conv, prm["conv_w"],
                                 prm["conv_b"], seq_len=L)
    x2 = _matmul_norm_residual(act.reshape(M, DFF), prm["w_down"], x2, prm["g_post_ffn"], tm=512, tn=256,
                               single_buffer_rows=True)

    x2 = _pe_layer(x2, prm["g_pe"], prm["w_pe_gate"], p.reshape(M, -1), prm["w_pe"])
    return x2.reshape(B, L, D), s_gla_new, s_rwkv_new, new_shift, new_conv


def kernel(x_prompt, x_sample, state_gla, state_rwkv, state_shift, state_ffn_conv, p_prompt, p_sample, w_in, w_alpha2, b_alpha, gla_norm, w_branch_a, mu_shift, w0, w_decay2, a0, w_iclr2, w_gate2, k_k, k_a, r_k, ln_x_w, ln_x_b, w_branch_b, w_out, g_pre_mix, g_post_mix, g_pre_ffn, g_post_ffn, w_up, conv_w, conv_b, w_down, g_pe, w_pe_gate, w_pe):
    params = (w_in, w_alpha2, b_alpha, gla_norm, w_branch_a,
              mu_shift, w0, w_decay2, a0, w_iclr2, w_gate2, k_k, k_a, r_k, ln_x_w, ln_x_b, w_branch_b,
              w_out, g_pre_mix, g_post_mix, g_pre_ffn, g_post_ffn,
              w_up, conv_w, conv_b, w_down, g_pe, w_pe_gate, w_pe)
    depth = w_in.shape[0]
    nb = x_prompt.shape[0]
    yp, ys = x_prompt, x_sample
    outs_p = [[], [], [], []]
    outs_s = [[], [], [], []]
    for i in range(depth):
        prm = _prepare_params(tuple(t[i] for t in params))
        z_gla = jnp.zeros((nb,) + state_gla.shape[2:], F32)
        z_rwkv = jnp.zeros((nb,) + state_rwkv.shape[2:], F32)
        z_shift = jnp.zeros((nb,) + state_shift.shape[2:], x_prompt.dtype)
        z_conv = jnp.zeros((nb,) + state_ffn_conv.shape[2:], x_prompt.dtype)
        yp, *st = _layer(yp, p_prompt[i], z_gla, z_rwkv, z_shift, z_conv, prm)
        for acc, s in zip(outs_p, st):
            acc.append(s)
        ys, *st = _layer(ys, p_sample[i], state_gla[i], state_rwkv[i], state_shift[i],
                         state_ffn_conv[i], prm)
        for acc, s in zip(outs_s, st):
            acc.append(s)
    return (yp, ys, *(jnp.stack(a) for a in outs_p), *(jnp.stack(a) for a in outs_s))
```

```python
import functools
import math

import jax
import jax.numpy as jnp
from jax import lax
from jax.experimental import pallas as pl
from jax.experimental.pallas import tpu as pltpu

F32 = jnp.float32
BF16 = jnp.bfloat16

LANE = 128
SUBLANE = 8
VMEM_LIMIT_BYTES = 56 * 1024 * 1024

EPS = 1e-6
GN_EPS = 64e-5
GLA_TAU = 16.0
GLA_CHUNK = 64
RWKV_CHUNK = 64
KK_EPS = 1e-12
DECAY_SCALE = math.exp(-0.5)
GELU_C = math.sqrt(2.0 / math.pi)


ROW_RESIDENT = pl.Buffered(1)


def _cparams(sem):
    return pltpu.CompilerParams(dimension_semantics=sem, vmem_limit_bytes=VMEM_LIMIT_BYTES)


def _sigmoid(x):
    return 1.0 / (1.0 + jnp.exp(-x))


def _log_sigmoid(x):
    return jnp.minimum(x, 0.0) - jnp.log(1.0 + jnp.exp(-jnp.abs(x)))


def _gelu_tanh(x):
    return 0.5 * x * (1.0 + jnp.tanh(GELU_C * (x + 0.044715 * (x * x * x))))


def _split2(x):
    hi = x.astype(BF16)
    lo = (x - hi.astype(F32)).astype(BF16)
    return hi, lo


def _dot(a, b):
    return jnp.dot(a, b, preferred_element_type=F32)


def _dot_nt(a, b):
    return lax.dot_general(a, b, (((1,), (1,)), ((), ())), preferred_element_type=F32)


def _dot_tn(a, b):
    return lax.dot_general(a, b, (((0,), (0,)), ((), ())), preferred_element_type=F32)


def _ones_dot(m01, x):
    hi, lo = _split2(x)
    return _dot(m01, hi) + _dot(m01, lo)


def _dot_ones(x, m01):
    return _dot(x.astype(BF16), m01)


def _lockstep(gens):
    gens = list(gens)
    while gens:
        alive = []
        for g in gens:
            try:
                next(g)
                alive.append(g)
            except StopIteration:
                pass
        gens = alive


def _tri_incl(n):
    r = lax.broadcasted_iota(jnp.int32, (n, n), 0)
    c = lax.broadcasted_iota(jnp.int32, (n, n), 1)
    return r >= c


def _pick_tile(n, target, mult):
    if n <= target:
        return n
    best = None
    t = mult
    while t <= target:
        if n % t == 0:
            best = t
        t += mult
    assert best is not None, (n, target, mult)
    return best


def _rms_rows(x_ref, g_ref, h_ref, rows):
    tm = x_ref.shape[0]

    def body(i, _):
        sl = pl.ds(pl.multiple_of(i * rows, rows), rows)
        x = x_ref[sl, :]
        ms = jnp.mean(x * x, axis=-1, keepdims=True)
        h_ref[sl, :] = (x * lax.rsqrt(ms + EPS) * g_ref[...]).astype(BF16)
        return 0

    lax.fori_loop(0, tm // rows, body, 0)


def _norm_mm_kernel(x_ref, g_ref, wt_ref, o_ref, h_ref, *, rows, n_valid):
    j = pl.program_id(1)

    @pl.when(j == 0)
    def _():
        _rms_rows(x_ref, g_ref, h_ref, rows)

    y = _dot_nt(h_ref[...], wt_ref[...].astype(BF16))
    tn = y.shape[1]
    if n_valid % tn:
        col = lax.broadcasted_iota(jnp.int32, y.shape, 1) + j * tn
        y = jnp.where(col < n_valid, y, 0.0)
    o_ref[...] = y


def _window(src, width):
    base = src // LANE * LANE
    shift = src - base
    wp = -(-width // LANE) * LANE
    return base, shift, (wp if shift == 0 else wp + LANE), wp


def _norm_matmul_nt(x, gain, wt, *, tm=1024, tn=512):
    M, K = x.shape
    nw = wt.shape[0]
    tm = _pick_tile(M, tm, SUBLANE)
    tn = min(tn, -(-nw // LANE) * LANE)
    N = -(-nw // tn) * tn
    rows = _pick_tile(tm, 64, SUBLANE)
    return pl.pallas_call(
        functools.partial(_norm_mm_kernel, rows=rows, n_valid=nw),
        grid=(M // tm, N // tn),
        in_specs=[pl.BlockSpec((tm, K), lambda i, j: (i, 0), pipeline_mode=ROW_RESIDENT),
                  pl.BlockSpec((1, K), lambda i, j: (0, 0)),
                  pl.BlockSpec((tn, K), lambda i, j: (j, 0))],
        out_specs=pl.BlockSpec((tm, tn), lambda i, j: (i, j)),
        out_shape=jax.ShapeDtypeStruct((M, N), F32),
        scratch_shapes=[pltpu.VMEM((tm, K), BF16)],
        compiler_params=_cparams(("parallel", "arbitrary")),
        name="norm_matmul",
    )(x, gain.reshape(1, K), wt)


def _pe_kernel(x_ref, g_ref, w_ref, p_ref, wp_ref, o_ref, h_ref, *, rows, tn):
    j = pl.program_id(1)

    @pl.when(j == 0)
    def _():
        _rms_rows(x_ref, g_ref, h_ref, rows)

    gate = _sigmoid(_dot(h_ref[...], w_ref[...].astype(BF16)))
    pe = _dot(p_ref[...].astype(BF16), wp_ref[...])
    xs = x_ref[:, pl.ds(pl.multiple_of(j * tn, tn), tn)]
    o_ref[...] = xs + gate * pe


def _pe_layer(x, gain, w_gate, p, w_pe, *, tm=1024, tn=512):
    M, K = x.shape
    N = w_gate.shape[1]
    P = p.shape[1]
    tm = _pick_tile(M, tm, SUBLANE)
    tn = _pick_tile(N, tn, LANE)
    rows = _pick_tile(tm, 64, SUBLANE)
    return pl.pallas_call(
        functools.partial(_pe_kernel, rows=rows, tn=tn),
        grid=(M // tm, N // tn),
        in_specs=[pl.BlockSpec((tm, K), lambda i, j: (i, 0), pipeline_mode=ROW_RESIDENT),
                  pl.BlockSpec((1, K), lambda i, j: (0, 0)),
                  pl.BlockSpec((K, tn), lambda i, j: (0, j)),
                  pl.BlockSpec((tm, P), lambda i, j: (i, 0)),
                  pl.BlockSpec((P, tn), lambda i, j: (0, j))],
        out_specs=pl.BlockSpec((tm, tn), lambda i, j: (i, j)),
        out_shape=jax.ShapeDtypeStruct((M, N), F32),
        scratch_shapes=[pltpu.VMEM((tm, K), BF16)],
        compiler_params=_cparams(("parallel", "arbitrary")),
        name="pe_layer",
    )(x, gain.reshape(1, K), w_gate, p, w_pe)


def _mm_norm_res_kernel(a_ref, w_ref, x_ref, g_ref, o_ref, *, tn, rows):
    j = pl.program_id(1)
    o_ref[:, pl.ds(pl.multiple_of(j * tn, tn), tn)] = _dot(a_ref[...], w_ref[...])

    @pl.when(j == pl.num_programs(1) - 1)
    def _():
        tm = o_ref.shape[0]

        def body(i, _):
            sl = pl.ds(pl.multiple_of(i * rows, rows), rows)
            y = o_ref[sl, :]
            ms = jnp.mean(y * y, axis=-1, keepdims=True)
            o_ref[sl, :] = x_ref[sl, :] + y * lax.rsqrt(ms + EPS) * g_ref[...]
            return 0

        lax.fori_loop(0, tm // rows, body, 0)


def _matmul_norm_residual(a, w, x, gain, *, tm, tn, single_buffer_rows):
    M, K = a.shape
    N = w.shape[1]
    tm = _pick_tile(M, tm, SUBLANE)
    tn = _pick_tile(N, tn, LANE)
    rows = _pick_tile(tm, 64, SUBLANE)
    mode = dict(pipeline_mode=ROW_RESIDENT) if single_buffer_rows else {}
    return pl.pallas_call(
        functools.partial(_mm_norm_res_kernel, tn=tn, rows=rows),
        grid=(M // tm, N // tn),
        in_specs=[pl.BlockSpec((tm, K), lambda i, j: (i, 0), **mode),
                  pl.BlockSpec((K, tn), lambda i, j: (0, j)),
                  pl.BlockSpec((tm, N), lambda i, j: (i, 0), **mode),
                  pl.BlockSpec((1, N), lambda i, j: (0, 0))],
        out_specs=pl.BlockSpec((tm, N), lambda i, j: (i, 0)),
        out_shape=jax.ShapeDtypeStruct((M, N), F32),
        compiler_params=_cparams(("parallel", "arbitrary")),
        name="matmul_norm_residual",
    )(a, w, x, gain.reshape(1, N))


def _merge_kernel(oa_ref, ob_ref, wa_ref, wb_ref, ga_ref, gb_ref, o_ref, *, sa, sb):
    tn = o_ref.shape[1]
    ya = _dot(oa_ref[...], wa_ref[...])
    yb = _dot(ob_ref[...], wb_ref[...])
    ga = ga_ref[:, sa:sa + tn]
    gb = gb_ref[:, sb:sb + tn]
    o_ref[...] = (_sigmoid(ga) * ya + _sigmoid(gb) * yb).astype(BF16)


def _gate_window_spec(tm, tn, off, n_cols):
    base, shift, _, _ = _window(off, tn)
    ww = tn if shift == 0 else tn + LANE
    spec = pl.BlockSpec((pl.Element(tm), pl.Element(ww)),
                        lambda i, j: (pl.multiple_of(i * tm, SUBLANE), pl.multiple_of(base + j * tn, LANE)))
    return spec, shift, base + n_cols - tn + ww


def _merge(oa, ob, wa, wb, proj, ga_off, gb_off, *, tm=512, tn=1024):
    M, KA = oa.shape
    KB = ob.shape[1]
    N = wa.shape[1]
    tm = _pick_tile(M, tm, SUBLANE)
    tn = _pick_tile(N, tn, LANE)
    ga_spec, sa, enda = _gate_window_spec(tm, tn, ga_off, N)
    gb_spec, sb, endb = _gate_window_spec(tm, tn, gb_off, N)
    assert max(enda, endb) <= proj.shape[1]
    return pl.pallas_call(
        functools.partial(_merge_kernel, sa=sa, sb=sb),
        grid=(M // tm, N // tn),
        in_specs=[pl.BlockSpec((tm, KA), lambda i, j: (i, 0)),
                  pl.BlockSpec((tm, KB), lambda i, j: (i, 0)),
                  pl.BlockSpec((KA, tn), lambda i, j: (0, j)),
                  pl.BlockSpec((KB, tn), lambda i, j: (0, j)),
                  ga_spec, gb_spec],
        out_specs=pl.BlockSpec((tm, tn), lambda i, j: (i, j)),
        out_shape=jax.ShapeDtypeStruct((M, N), BF16),
        compiler_params=_cparams(("parallel", "arbitrary")),
        name="merge",
    )(oa, ob, wa, wb, proj, proj)


def _gla_kernel(q_ref, k_ref, v_ref, za_ref, zg_ref, wa_ref, ba_ref, gn_ref, s0_ref,
                o_ref, sn_ref, s_ref, *, scale, c, heads, zshift):
    l = pl.program_id(1)
    bb = q_ref.shape[0]
    dk = q_ref.shape[2] // heads
    dv = v_ref.shape[2] // heads

    @pl.when(l == 0)
    def _():
        s_ref[...] = s0_ref[...].reshape(bb * heads, dk, dv)

    x = _dot(za_ref[...].reshape(bb * c, LANE).astype(BF16), wa_ref[...]) + ba_ref[...]
    la_seqs = _log_sigmoid(x) * (1.0 / GLA_TAU)
    tri = _tri_incl(c)
    tri01 = jnp.where(tri, 1.0, 0.0).astype(BF16)
    ones = jnp.ones((c, LANE), BF16)
    la_seq = [la_seqs[bi * c:(bi + 1) * c] for bi in range(bb)]
    b_seq = [_ones_dot(tri01, la) for la in la_seq]

    def head(bi, h):
        ks = slice(h * dk, (h + 1) * dk)
        vs = slice(h * dv, (h + 1) * dv)
        la = la_seq[bi][:, ks]
        b = b_seq[bi][:, ks]
        b_end = b[c - 1:c, :]
        q = q_ref[bi, :, ks] * scale
        k = k_ref[bi, :, ks]
        v = v_ref[bi, :, vs].astype(BF16)
        qd = (q * jnp.exp(b)).astype(BF16)
        kd = (k * jnp.exp(-b)).astype(BF16)
        s = s_ref[bi * heads + h]
        att = _dot_nt(qd, kd)
        o = _dot(qd, s.astype(BF16))
        k_end = (k * jnp.exp(b_end - b)).astype(BF16)
        la_hi, la_lo = _split2(la)
        dec = jnp.exp(_dot_tn(la_hi, ones) + _dot_tn(la_lo, ones))
        dec = jnp.concatenate([dec] * (dv // LANE), axis=1)
        s_ref[bi * heads + h] = s * dec + _dot_tn(k_end, v)
        yield
        o = o + _dot(jnp.where(tri, att, 0.0).astype(BF16), v)
        yield
        ms = jnp.mean(o * o, axis=-1, keepdims=True)
        on = o * lax.rsqrt(ms + EPS) * gn_ref[...]
        zg = zg_ref[bi, :, zshift + h * dv:zshift + (h + 1) * dv]
        o_ref[bi, :, vs] = (on * (zg * _sigmoid(zg))).astype(BF16)

    _lockstep([head(bi, h) for bi in range(bb) for h in range(heads)])

    @pl.when(l == pl.num_programs(1) - 1)
    def _():
        sn_ref[...] = s_ref[...].reshape(bb, heads, dk, dv)


def _gla(proj3, offs, w_alpha2p, b_alpha, gla_norm, s0, *, dk, dv, heads):
    B, L, _ = proj3.shape
    c = min(GLA_CHUNK, L)
    qk, gw = heads * dk, heads * dv
    assert L % c == 0 and dv % LANE == 0
    for name, wdt in (("q", qk), ("k", qk), ("v", gw), ("za", LANE)):
        assert offs[name] % wdt == 0
    jq, jk, jv, jz = offs["q"] // qk, offs["k"] // qk, offs["v"] // gw, offs["za"] // LANE
    zbase, zshift, zww, _ = _window(offs["zg"], gw)
    assert zbase + zww <= proj3.shape[2]
    bb = _pick_tile(B, 2, 1) if L == c else 1
    return pl.pallas_call(
        functools.partial(_gla_kernel, scale=dk ** -0.5, c=c, heads=heads, zshift=zshift),
        grid=(B // bb, L // c),
        in_specs=[pl.BlockSpec((bb, c, qk), lambda b, l: (b, l, jq)),
                  pl.BlockSpec((bb, c, qk), lambda b, l: (b, l, jk)),
                  pl.BlockSpec((bb, c, gw), lambda b, l: (b, l, jv)),
                  pl.BlockSpec((bb, c, LANE), lambda b, l: (b, l, jz)),
                  pl.BlockSpec((pl.Element(bb), pl.Element(c), pl.Element(zww)),
                               lambda b, l: (b * bb, pl.multiple_of(l * c, c), zbase)),
                  pl.BlockSpec((LANE, qk), lambda b, l: (0, 0)),
                  pl.BlockSpec((1, qk), lambda b, l: (0, 0)),
                  pl.BlockSpec((1, dv), lambda b, l: (0, 0)),
                  pl.BlockSpec((bb, heads, dk, dv), lambda b, l: (b, 0, 0, 0))],
        out_specs=[pl.BlockSpec((bb, c, gw), lambda b, l: (b, l, 0)),
                   pl.BlockSpec((bb, heads, dk, dv), lambda b, l: (b, 0, 0, 0))],
        out_shape=[jax.ShapeDtypeStruct((B, L, gw), BF16),
                   jax.ShapeDtypeStruct((B, heads, dk, dv), F32)],
        scratch_shapes=[pltpu.VMEM((bb * heads, dk, dv), F32)],
        compiler_params=_cparams(("parallel", "arbitrary")),
        name="gla",
    )(proj3, proj3, proj3, proj3, proj3, w_alpha2p, b_alpha.reshape(1, -1), gla_norm.reshape(1, dv), s0)


def _rwkv_prep_kernel(r_ref, xw_ref, kr_ref, vr_ref, xa_ref, xg_ref,
                      sr_ref, sxw_ref, skr_ref, svr_ref, sxa_ref, sxg_ref,
                      mr_ref, mxw_ref, mkr_ref, mvr_ref, mxa_ref, mxg_ref,
                      w0_ref, wd_ref, a0_ref, wi_ref, wg_ref,
                      ro_ref, lw_ref, ko_ref, vo_ref, ao_ref, go_ref,
                      cr_ref, cxw_ref, ckr_ref, cvr_ref, cxa_ref, cxg_ref, *, shifts):
    l = pl.program_id(1)
    bb, tl = r_ref.shape[0], r_ref.shape[1]
    sh_r, sh_xw, sh_kr, sh_vr, sh_xa, sh_xg = shifts

    def shift_mix(z_ref, s0, s_ref, m_ref, c_ref):
        cur = z_ref[:, :, s0:s0 + m_ref.shape[1]]
        first = jnp.where(l == 0, s_ref[...], c_ref[...])
        row = lax.broadcasted_iota(jnp.int32, cur.shape, 1)
        prev = jnp.where(row == 0, first, pltpu.roll(cur, 1, 1))
        c_ref[...] = cur[:, tl - 1:tl, :]
        return cur + (prev - cur) * m_ref[...]

    def flat(x):
        return x.reshape(bb * tl, x.shape[2])

    ro_ref[...] = shift_mix(r_ref, sh_r, sr_ref, mr_ref, cr_ref)
    ko_ref[...] = shift_mix(kr_ref, sh_kr, skr_ref, mkr_ref, ckr_ref)
    vo_ref[...] = shift_mix(vr_ref, sh_vr, svr_ref, mvr_ref, cvr_ref)
    xw = flat(shift_mix(xw_ref, sh_xw, sxw_ref, mxw_ref, cxw_ref))
    xa = flat(shift_mix(xa_ref, sh_xa, sxa_ref, mxa_ref, cxa_ref))
    xg = flat(shift_mix(xg_ref, sh_xg, sxg_ref, mxg_ref, cxg_ref))
    z = w0_ref[...] + _dot(jnp.tanh(xw).astype(BF16), wd_ref[...])
    rw = z.shape[1]
    lw_ref[...] = (-DECAY_SCALE * _sigmoid(z)).reshape(bb, tl, rw)
    ao_ref[...] = _sigmoid(a0_ref[...] + _dot(xa.astype(BF16), wi_ref[...])).reshape(bb, tl, rw)
    go_ref[...] = _dot(_sigmoid(xg).astype(BF16), wg_ref[...]).reshape(bb, tl, rw)


def _rwkv_prep(proj3, offs, shift_parts, mu_parts, w0, w_decay2p, a0, w_iclr2p, w_gate2, *, rw, gl):
    B, L, _ = proj3.shape
    tl = _pick_tile(L, 256, SUBLANE)
    bb = _pick_tile(B, max(1, 256 // L), 1) if tl == L else 1
    names = ("r", "xw", "kr", "vr", "xa", "xg")
    widths = {"r": rw, "xw": LANE, "kr": rw, "vr": rw, "xa": LANE, "xg": gl}
    in_specs, args, shifts = [], [], []
    for n in names:
        base, shift, ww, wp = _window(offs[n], widths[n])
        assert wp == widths[n] and base + ww <= proj3.shape[2]
        shifts.append(shift)
        in_specs.append(pl.BlockSpec(
            (pl.Element(bb), pl.Element(tl), pl.Element(ww)),
            functools.partial(lambda b, l, base: (b * bb, pl.multiple_of(l * tl, SUBLANE), base), base=base)))
        args.append(proj3)
    for n in names:
        in_specs.append(pl.BlockSpec((bb, 1, widths[n]), lambda b, l: (b, 0, 0)))
        args.append(shift_parts[n])
    for n in names:
        in_specs.append(pl.BlockSpec((1, widths[n]), lambda b, l: (0, 0)))
        args.append(mu_parts[n])
    for arr in (w0.reshape(1, rw), w_decay2p, a0.reshape(1, rw), w_iclr2p, w_gate2):
        in_specs.append(pl.BlockSpec(arr.shape, lambda b, l: (0, 0)))
        args.append(arr)
    out_spec = pl.BlockSpec((bb, tl, rw), lambda b, l: (b, l, 0))
    out_sd = jax.ShapeDtypeStruct((B, L, rw), F32)
    return pl.pallas_call(
        functools.partial(_rwkv_prep_kernel, shifts=tuple(shifts)),
        grid=(B // bb, L // tl),
        in_specs=in_specs,
        out_specs=[out_spec] * 6,
        out_shape=[out_sd] * 6,
        scratch_shapes=[pltpu.VMEM((bb, 1, widths[n]), F32) for n in names],
        compiler_params=_cparams(("parallel", "arbitrary")),
        name="rwkv_prep",
    )(*args)


def _rwkv_chunk_kernel(r_ref, lw_ref, kr_ref, v_ref, a_ref, g_ref,
                       kk_w_ref, ka_w_ref, rk_w_ref, lnw_ref, lnb_ref, s0_ref,
                       o_ref, sn_ref, w_ref, *, c, hd, gp):
    l = pl.program_id(2)
    tl = r_ref.shape[1]
    n2 = 2 * c
    lane_c = lax.broadcasted_iota(jnp.int32, (c, LANE), 1)
    head0_c = lane_c < hd
    row2 = lax.broadcasted_iota(jnp.int32, (n2, n2), 0)
    col2 = lax.broadcasted_iota(jnp.int32, (n2, n2), 1)
    same = (2 * row2 + 1 - n2) * (2 * col2 + 1 - n2) > 0
    strict = jnp.logical_and(same, row2 > col2)
    incl = jnp.logical_and(same, row2 >= col2)
    eye = jnp.where(row2 == col2, 1.0, 0.0)
    tri = jnp.where(_tri_incl(c), 1.0, 0.0).astype(BF16)
    lr = lax.broadcasted_iota(jnp.int32, (LANE, LANE), 0)
    lc = lax.broadcasted_iota(jnp.int32, (LANE, LANE), 1)
    seg = jnp.where((2 * lr + 1 - LANE) * (2 * lc + 1 - LANE) > 0, 1.0, 0.0).astype(BF16)

    @pl.when(l == 0)
    def _():
        zero = jnp.zeros((hd, hd), F32)
        for gi in range(gp):
            s_e = s0_ref[0, 2 * gi]
            s_o = s0_ref[0, 2 * gi + 1]
            w_ref[gi] = jnp.concatenate([jnp.concatenate([s_e, zero], axis=1),
                                         jnp.concatenate([zero, s_o], axis=1)], axis=0)

    def stack(x):
        return jnp.concatenate([jnp.where(head0_c, x, 0.0), jnp.where(head0_c, 0.0, x)], axis=0)

    def unstack(x):
        return x[0:c] + x[c:n2]

    def chunk_pair(sl, gi):
        ls = slice(gi * LANE, (gi + 1) * LANE)
        r = r_ref[0, sl, ls]
        lw = lw_ref[0, sl, ls]
        kr = kr_ref[0, sl, ls]
        v = v_ref[0, sl, ls]
        a = a_ref[0, sl, ls]
        kk = kr * kk_w_ref[:, ls]
        kk_ss = _dot_ones(kk * kk, seg)
        cum = _ones_dot(tri, lw)
        yield
        kk = kk / jnp.maximum(jnp.sqrt(kk_ss), KK_EPS)
        k = kr * (1.0 + (a - 1.0) * ka_w_ref[:, ls])
        cum_end = cum[c - 1:c, :]
        e_neg = jnp.exp(-cum)
        e_rem = jnp.exp(cum_end - cum)
        al = kk * a
        al_t = stack(al * e_neg).astype(BF16)
        k_t = stack(k * e_neg).astype(BF16)
        be_t = stack(-kk * jnp.exp(cum - lw)).astype(BF16)
        r_t = stack(r * jnp.exp(cum)).astype(BF16)
        v_s = stack(v).astype(BF16)
        lhs = jnp.concatenate([be_t, r_t], axis=0)
        sc_a = _dot_nt(lhs, al_t)
        sc_k = _dot_nt(lhs, k_t)
        w = w_ref[gi]
        rd = _dot_nt(lhs, w.astype(BF16))
        yield
        l_a =jnp.where(strict, sc_a[0:n2], 0.0)
        l_k = jnp.where(strict, sc_k[0:n2], 0.0).astype(BF16)
        m_a = jnp.where(incl, sc_a[n2:2 * n2], 0.0).astype(BF16)
        m_k = jnp.where(incl, sc_k[n2:2 * n2], 0.0).astype(BF16)
        t_inv = eye + l_a
        lp = l_a
        rhs_u = rd[0:n2] + _dot(l_k, v_s)
        o_s = rd[n2:2 * n2] + _dot(m_k, v_s)
        bonus = _dot_ones(r * k * rk_w_ref[:, ls], seg) * v
        span = 1
        while 2 * span < c:
            lpb = lp.astype(BF16)
            lp = _dot(lpb, lpb)
            yield
            t_inv = t_inv + _dot(t_inv.astype(BF16), lp.astype(BF16))
            yield
            span *= 2
        u_s = _dot(t_inv.astype(BF16), rhs_u.astype(BF16))
        yield
        u_b = u_s.astype(BF16)
        o_s = o_s + _dot(m_a, u_b)
        uv = jnp.concatenate([u_b, v_s], axis=0)
        ak = jnp.concatenate([stack(al * e_rem), stack(k * e_rem)], axis=0).astype(BF16)
        w_ref[gi] = w * jnp.exp(cum_end) + _dot_tn(uv, ak)
        yield
        o = unstack(o_s)
        mu = _dot_ones(o, seg) * (1.0 / hd)
        yield
        d = o - mu
        var = _dot_ones(d * d, seg) * (1.0 / hd)
        yield
        on = d * lax.rsqrt(var + GN_EPS) * lnw_ref[:, ls] + lnb_ref[:, ls]
        o_ref[0, sl, ls] = ((on + bonus) * g_ref[0, sl, ls]).astype(BF16)

    def chunk(ci, _):
        sl = pl.ds(pl.multiple_of(ci * c, c), c)
        _lockstep([chunk_pair(sl, gi) for gi in range(gp)])
        return 0

    lax.fori_loop(0, tl // c, chunk, 0)

    @pl.when(l == pl.num_programs(2) - 1)
    def _():
        for gi in range(gp):
            wf = w_ref[gi]
            sn_ref[0, 2 * gi] = wf[0:hd, 0:hd]
            sn_ref[0, 2 * gi + 1] = wf[hd:2 * hd, hd:2 * hd]


def _rwkv_chunk(r, lw, kr, vr, a, g, k_k, k_a, r_k, ln_w, ln_b, s0p, *, hd):
    B, L, RW = r.shape
    assert 2 * hd == LANE and RW % LANE == 0
    c = min(RWKV_CHUNK, L)
    tl = _pick_tile(L, 512, c)
    assert L % c == 0 and tl % c == 0 and c % SUBLANE == 0
    npair = RW // LANE
    gp = _pick_tile(npair, 8 if L > c else 16, 1)
    seq = pl.BlockSpec((1, tl, gp * LANE), lambda b, p, l: (b, l, p))
    par = pl.BlockSpec((1, gp * LANE), lambda b, p, l: (0, p))
    st = pl.BlockSpec((1, 2 * gp, hd, hd), lambda b, p, l: (b, p, 0, 0))
    return pl.pallas_call(
        functools.partial(_rwkv_chunk_kernel, c=c, hd=hd, gp=gp),
        grid=(B, npair // gp, L // tl),
        in_specs=[seq] * 6 + [par] * 5 + [st],
        out_specs=[seq, st],
        out_shape=[jax.ShapeDtypeStruct((B, L, RW), BF16),
                   jax.ShapeDtypeStruct((B, 2 * npair, hd, hd), F32)],
        scratch_shapes=[pltpu.VMEM((gp, LANE, LANE), F32)],
        compiler_params=_cparams(("parallel", "parallel", "arbitrary")),
        name="rwkv_chunk",
    )(r, lw, kr, vr, a, g, k_k.reshape(1, RW), k_a.reshape(1, RW), r_k.reshape(1, RW),
      ln_w.reshape(1, RW), ln_b.reshape(1, RW), s0p)


def _up_conv_kernel(x_ref, gn_ref, wg_ref, wv_ref, s_ref, cw_ref, cb_ref, act_ref, nc_ref, h_ref, c_ref,
                    *, rows, tiles_per_seq, seqs):
    i, j = pl.program_id(0), pl.program_id(1)
    tm = x_ref.shape[0]
    tl = tm // seqs

    @pl.when(j == 0)
    def _():
        _rms_rows(x_ref, gn_ref, h_ref, rows)

    h = h_ref[...]
    g = _dot(h, wg_ref[...].astype(BF16))
    v = _dot(h, wv_ref[...].astype(BF16))
    tn = g.shape[1]
    g = g.reshape(seqs, tl, tn)
    first = (i % tiles_per_seq) == 0
    prev = jnp.where(first, s_ref[...], c_ref[j])
    row = lax.broadcasted_iota(jnp.int32, g.shape, 1)
    g1 = jnp.where(row == 0, prev[:, 1:2], pltpu.roll(g, 1, 1))
    g2 = jnp.where(row == 0, prev[:, 0:1], jnp.where(row == 1, prev[:, 1:2], pltpu.roll(g, 2, 1)))
    last2 = g[:, tl - 2:tl, :]
    c_ref[j] = last2
    nc_ref[...] = last2
    conv = cb_ref[...] + g2 * cw_ref[0:1, :] + g1 * cw_ref[1:2, :] + g * cw_ref[2:3, :]
    act_ref[...] = (_gelu_tanh(conv).reshape(tm, tn) * v).astype(BF16)


def _up_conv_act(x, gain, w_up, s_conv, conv_w, conv_b, *, seq_len, tm=1024, tn=256):
    M, K = x.shape
    dff = conv_b.shape[0]
    if seq_len >= tm:
        tm = _pick_tile(seq_len, tm, SUBLANE)
        seqs, tps = 1, seq_len // tm
    else:
        assert seq_len % SUBLANE == 0
        seqs, tps = _pick_tile(M // seq_len, tm // seq_len, 1), 1
        tm = seqs * seq_len
    tn = _pick_tile(dff, tn, LANE)
    nj = dff // tn
    rows = _pick_tile(tm, 64, SUBLANE)
    assert conv_w.shape[0] == 3 and s_conv.shape[1] == 2 and tm // seqs >= 2
    act, tails = pl.pallas_call(
        functools.partial(_up_conv_kernel, rows=rows, tiles_per_seq=tps, seqs=seqs),
        grid=(M // tm, nj),
        in_specs=[pl.BlockSpec((tm, K), lambda i, j: (i, 0), pipeline_mode=ROW_RESIDENT),
                  pl.BlockSpec((1, K), lambda i, j: (0, 0)),
                  pl.BlockSpec((K, tn), lambda i, j: (0, j)),
                  pl.BlockSpec((K, tn), lambda i, j: (0, nj + j)),
                  pl.BlockSpec((seqs, 2, tn), lambda i, j: (i // tps, 0, j)),
                  pl.BlockSpec((3, tn), lambda i, j: (0, j)),
                  pl.BlockSpec((1, tn), lambda i, j: (0, j))],
        out_specs=[pl.BlockSpec((tm, tn), lambda i, j: (i, j)),
                   pl.BlockSpec((seqs, 2, tn), lambda i, j: (i, 0, j))],
        out_shape=[jax.ShapeDtypeStruct((M, dff), BF16),
                   jax.ShapeDtypeStruct((M // tm * seqs, 2, dff), F32)],
        scratch_shapes=[pltpu.VMEM((tm, K), BF16), pltpu.VMEM((nj, seqs, 2, tn), F32)],
        compiler_params=_cparams(("arbitrary", "arbitrary")),
        name="up_conv_act",
    )(x, gain.reshape(1, K), w_up, w_up, s_conv, conv_w, conv_b.reshape(1, dff))
    return act, tails[tps - 1::tps]


def _pad_cols(a, width):
    return a if a.shape[-1] == width else jnp.pad(a, [(0, 0)] * (a.ndim - 1) + [(0, width - a.shape[-1])])


def _pad_rows(a, height):
    return a if a.shape[0] == height else jnp.pad(a, [(0, height - a.shape[0])] + [(0, 0)] * (a.ndim - 1))


def _layout(D, QK, GW, R, RW, WL, AL, GL):
    src, o = {}, 0
    for name, w in (("q", QK), ("k", QK), ("v", GW), ("za", R), ("zg", GW), ("r", RW), ("xw", WL),
                    ("kr", RW), ("vr", RW), ("xa", AL), ("xg", GL), ("ga", D), ("gb", D)):
        src[name] = (o, w)
        o += w
    return src


def _prepare_params(lp):
    (w_in, w_alpha2, b_alpha, gla_norm, w_branch_a,
     mu_shift, w0, w_decay2, a0, w_iclr2, w_gate2, k_k, k_a, r_k, ln_x_w, ln_x_b, w_branch_b,
     w_out, g_pre_mix, g_post_mix, g_pre_ffn, g_post_ffn,
     w_up, conv_w, conv_b, w_down, g_pe, w_pe_gate, w_pe) = lp
    D = w_in.shape[0]
    R, QK = w_alpha2.shape
    GW = w_branch_a.shape[0]
    RW = w_branch_b.shape[0]
    WL, AL, GL = w_decay2.shape[0], w_iclr2.shape[0], w_gate2.shape[0]
    src = _layout(D, QK, GW, R, RW, WL, AL, GL)
    pad = lambda w: -(-w // LANE) * LANE
    prm = dict(
        dims=dict(D=D, R=R, QK=QK, GW=GW, RW=RW, WL=WL, AL=AL, GL=GL, DV=gla_norm.shape[0],
                  HD=r_k.shape[1], DFF=conv_b.shape[0]),
        src=src, off={n: src[n][0] for n in src},
        w_in_t=w_in.T,
        w_alpha2=_pad_rows(w_alpha2, LANE).astype(BF16), b_alpha=b_alpha, gla_norm=gla_norm,
        w_branch_a=w_branch_a.astype(BF16), w_branch_b=w_branch_b.astype(BF16),
        w0=w0, w_decay2=_pad_rows(w_decay2, LANE).astype(BF16), a0=a0,
        w_iclr2=_pad_rows(w_iclr2, LANE).astype(BF16), w_gate2=w_gate2.astype(BF16),
        k_k=k_k, k_a=k_a, r_k=r_k.reshape(-1), ln_x_w=ln_x_w, ln_x_b=ln_x_b,
        w_out=w_out.astype(BF16), g_pre_mix=g_pre_mix, g_post_mix=g_post_mix,
        g_pre_ffn=g_pre_ffn, g_post_ffn=g_post_ffn,
        w_up=w_up, conv_w=conv_w, conv_b=conv_b, w_down=w_down.astype(BF16),
        g_pe=g_pe, w_pe_gate=w_pe_gate, w_pe=w_pe.astype(BF16),
    )
    rsrc = src["r"][0]
    rnames = ("r", "xw", "kr", "vr", "xa", "xg")
    prm["rnames"] = rnames
    prm["rsl"] = {n: (src[n][0] - rsrc, src[n][1]) for n in rnames}
    prm["mu"] = {n: _pad_cols(mu_shift[None, prm["rsl"][n][0]:prm["rsl"][n][0] + prm["rsl"][n][1]],
                              pad(prm["rsl"][n][1])) for n in rnames}
    return prm


def _layer(x, p, s_gla, s_rwkv, s_shift, s_conv, prm):
    B, L, D = x.shape
    M = B * L
    dm, dst = prm["dims"], prm["off"]
    GW, RW, DV, HD, DFF, QK = dm["GW"], dm["RW"], dm["DV"], dm["HD"], dm["DFF"], dm["QK"]
    heads = GW // DV
    pad = lambda w: -(-w // LANE) * LANE
    x2 = x.reshape(M, D)

    proj = _norm_matmul_nt(x2, prm["g_pre_mix"], prm["w_in_t"])
    proj3 = proj.reshape(B, L, -1)

    o_a, s_gla_new = _gla(proj3, dst, prm["w_alpha2"], prm["b_alpha"], prm["gla_norm"], s_gla,
                          dk=QK // heads, dv=DV, heads=heads)

    shift_parts = {n: _pad_cols(s_shift[:, None, prm["rsl"][n][0]:prm["rsl"][n][0] + prm["rsl"][n][1]],
                                pad(prm["rsl"][n][1])) for n in prm["rnames"]}
    r, lw, kr, vr, a, g = _rwkv_prep(proj3, dst, shift_parts, prm["mu"], prm["w0"], prm["w_decay2"],
                                     prm["a0"], prm["w_iclr2"], prm["w_gate2"], rw=RW, gl=dm["GL"])
    o_b, s_rwkv_new = _rwkv_chunk(r, lw, kr, vr, a, g, prm["k_k"], prm["k_a"], prm["r_k"],
                                  prm["ln_x_w"], prm["ln_x_b"], s_rwkv, hd=HD)
    new_shift = proj3[:, L - 1, dst["r"]:dst["ga"]]

    mixed = _merge(o_a.reshape(M, GW), o_b.reshape(M, RW), prm["w_branch_a"], prm["w_branch_b"],
                   proj, dst["ga"], dst["gb"])
    x2 = _matmul_norm_residual(mixed, prm["w_out"], x2, prm["g_post_mix"], tm=512, tn=512,
                               single_buffer_rows=False)

    act, new_conv = _up_conv_act(x2, prm["g_pre_ffn"], prm["w_up"], s_conv, prm["conv_w"],
                                 prm["conv_b"], seq_len=L)
    x2 = _matmul_norm_residual(act.reshape(M, DFF), prm["w_down"], x2, prm["g_post_ffn"], tm=512, tn=256,
                               single_buffer_rows=True)

    x2 = _pe_layer(x2, prm["g_pe"], prm["w_pe_gate"], p.reshape(M, -1), prm["w_pe"])
    return x2.reshape(B, L, D), s_gla_new, s_rwkv_new, new_shift, new_conv


def kernel(x_prompt, x_sample, state_gla, state_rwkv, state_shift, state_ffn_conv, p_prompt, p_sample, w_in, w_alpha2, b_alpha, gla_norm, w_branch_a, mu_shift, w0, w_decay2, a0, w_iclr2, w_gate2, k_k, k_a, r_k, ln_x_w, ln_x_b, w_branch_b, w_out, g_pre_mix, g_post_mix, g_pre_ffn, g_post_ffn, w_up, conv_w, conv_b, w_down, g_pe, w_pe_gate, w_pe):
    params = (w_in, w_alpha2, b_alpha, gla_norm, w_branch_a,
              mu_shift, w0, w_decay2, a0, w_iclr2, w_gate2, k_k, k_a, r_k, ln_x_w, ln_x_b, w_branch_b,
              w_out, g_pre_mix, g_post_mix, g_pre_ffn, g_post_ffn,
              w_up, conv_w, conv_b, w_down, g_pe, w_pe_gate, w_pe)
    depth = w_in.shape[0]
    nb = x_prompt.shape[0]
    yp, ys = x_prompt, x_sample
    outs_p = [[], [], [], []]
    outs_s = [[], [], [], []]
    for i in range(depth):
        prm = _prepare_params(tuple(t[i] for t in params))
        z_gla = jnp.zeros((nb,) + state_gla.shape[2:], F32)
        z_rwkv = jnp.zeros((nb,) + state_rwkv.shape[2:], F32)
        z_shift = jnp.zeros((nb,) + state_shift.shape[2:], x_prompt.dtype)
        z_conv = jnp.zeros((nb,) + state_ffn_conv.shape[2:], x_prompt.dtype)
        yp, *st = _layer(yp, p_prompt[i], z_gla, z_rwkv, z_shift, z_conv, prm)
        for acc, s in zip(outs_p, st):
            acc.append(s)
        ys, *st = _layer(ys, p_sample[i], state_gla[i], state_rwkv[i], state_shift[i],
                         state_ffn_conv[i], prm)
        for acc, s in zip(outs_s, st):
            acc.append(s)
    return (yp, ys, *(jnp.stack(a) for a in outs_p), *(jnp.stack(a) for a in outs_s))
```

```python
import functools
import math

import jax
import jax.numpy as jnp
from jax import lax
from jax.experimental import pallas as pl
from jax.experimental.pallas import tpu as pltpu

F32 = jnp.float32
BF16 = jnp.bfloat16

LANE = 128
SUBLANE = 8
VMEM_LIMIT_BYTES = 56 * 1024 * 1024

EPS = 1e-6
GN_EPS = 64e-5
GLA_TAU = 16.0
GLA_CHUNK = 64
RWKV_CHUNK = 64
KK_EPS = 1e-12
DECAY_SCALE = math.exp(-0.5)
GELU_C = math.sqrt(2.0 / math.pi)


ROW_RESIDENT = pl.Buffered(1)


def _cparams(sem):
    return pltpu.CompilerParams(dimension_semantics=sem, vmem_limit_bytes=VMEM_LIMIT_BYTES)


def _sigmoid(x):
    return 1.0 / (1.0 + jnp.exp(-x))


def _log_sigmoid(x):
    return jnp.minimum(x, 0.0) - jnp.log(1.0 + jnp.exp(-jnp.abs(x)))


def _gelu_tanh(x):
    return 0.5 * x * (1.0 + jnp.tanh(GELU_C * (x + 0.044715 * (x * x * x))))


def _split2(x):
    hi = x.astype(BF16)
    lo = (x - hi.astype(F32)).astype(BF16)
    return hi, lo


def _dot(a, b):
    return jnp.dot(a, b, preferred_element_type=F32)


def _dot_nt(a, b):
    return lax.dot_general(a, b, (((1,), (1,)), ((), ())), preferred_element_type=F32)


def _dot_tn(a, b):
    return lax.dot_general(a, b, (((0,), (0,)), ((), ())), preferred_element_type=F32)


def _ones_dot(m01, x):
    hi, lo = _split2(x)
    return _dot(m01, hi) + _dot(m01, lo)


def _dot_ones(x, m01):
    return _dot(x.astype(BF16), m01)


def _lockstep(gens):
    gens = list(gens)
    while gens:
        alive = []
        for g in gens:
            try:
                next(g)
                alive.append(g)
            except StopIteration:
                pass
        gens = alive


def _tri_incl(n):
    r = lax.broadcasted_iota(jnp.int32, (n, n), 0)
    c = lax.broadcasted_iota(jnp.int32, (n, n), 1)
    return r >= c


def _pick_tile(n, target, mult):
    if n <= target:
        return n
    best = None
    t = mult
    while t <= target:
        if n % t == 0:
            best = t
        t += mult
    assert best is not None, (n, target, mult)
    return best


def _rms_rows(x_ref, g_ref, h_ref, rows):
    tm = x_ref.shape[0]

    def body(i, _):
        sl = pl.ds(pl.multiple_of(i * rows, rows), rows)
        x = x_ref[sl, :]
        ms = jnp.mean(x * x, axis=-1, keepdims=True)
        h_ref[sl, :] = (x * lax.rsqrt(ms + EPS) * g_ref[...]).astype(BF16)
        return 0

    lax.fori_loop(0, tm // rows, body, 0)


def _norm_mm_kernel(x_ref, g_ref, wt_ref, o_ref, h_ref, *, rows, n_valid):
    j = pl.program_id(1)

    @pl.when(j == 0)
    def _():
        _rms_rows(x_ref, g_ref, h_ref, rows)

    y = _dot_nt(h_ref[...], wt_ref[...].astype(BF16))
    tn = y.shape[1]
    if n_valid % tn:
        col = lax.broadcasted_iota(jnp.int32, y.shape, 1) + j * tn
        y = jnp.where(col < n_valid, y, 0.0)
    o_ref[...] = y


def _window(src, width):
    base = src // LANE * LANE
    shift = src - base
    wp = -(-width // LANE) * LANE
    return base, shift, (wp if shift == 0 else wp + LANE), wp


def _norm_matmul_nt(x, gain, wt, *, tm=1024, tn=512):
    M, K = x.shape
    nw = wt.shape[0]
    tm = _pick_tile(M, tm, SUBLANE)
    tn = min(tn, -(-nw // LANE) * LANE)
    N = -(-nw // tn) * tn
    rows = _pick_tile(tm, 64, SUBLANE)
    return pl.pallas_call(
        functools.partial(_norm_mm_kernel, rows=rows, n_valid=nw),
        grid=(M // tm, N // tn),
        in_specs=[pl.BlockSpec((tm, K), lambda i, j: (i, 0), pipeline_mode=ROW_RESIDENT),
                  pl.BlockSpec((1, K), lambda i, j: (0, 0)),
                  pl.BlockSpec((tn, K), lambda i, j: (j, 0))],
        out_specs=pl.BlockSpec((tm, tn), lambda i, j: (i, j)),
        out_shape=jax.ShapeDtypeStruct((M, N), F32),
        scratch_shapes=[pltpu.VMEM((tm, K), BF16)],
        compiler_params=_cparams(("parallel", "arbitrary")),
        name="norm_matmul",
    )(x, gain.reshape(1, K), wt)


def _pe_kernel(x_ref, g_ref, w_ref, p_ref, wp_ref, o_ref, h_ref, *, rows, tn):
    j = pl.program_id(1)

    @pl.when(j == 0)
    def _():
        _rms_rows(x_ref, g_ref, h_ref, rows)

    gate = _sigmoid(_dot(h_ref[...], w_ref[...].astype(BF16)))
    pe = _dot(p_ref[...].astype(BF16), wp_ref[...])
    xs = x_ref[:, pl.ds(pl.multiple_of(j * tn, tn), tn)]
    o_ref[...] = xs + gate * pe


def _pe_layer(x, gain, w_gate, p, w_pe, *, tm=1024, tn=512):
    M, K = x.shape
    N = w_gate.shape[1]
    P = p.shape[1]
    tm = _pick_tile(M, tm, SUBLANE)
    tn = _pick_tile(N, tn, LANE)
    rows = _pick_tile(tm, 64, SUBLANE)
    return pl.pallas_call(
        functools.partial(_pe_kernel, rows=rows, tn=tn),
        grid=(M // tm, N // tn),
        in_specs=[pl.BlockSpec((tm, K), lambda i, j: (i, 0), pipeline_mode=ROW_RESIDENT),
                  pl.BlockSpec((1, K), lambda i, j: (0, 0)),
                  pl.BlockSpec((K, tn), lambda i, j: (0, j)),
                  pl.BlockSpec((tm, P), lambda i, j: (i, 0)),
                  pl.BlockSpec((P, tn), lambda i, j: (0, j))],
        out_specs=pl.BlockSpec((tm, tn), lambda i, j: (i, j)),
        out_shape=jax.ShapeDtypeStruct((M, N), F32),
        scratch_shapes=[pltpu.VMEM((tm, K), BF16)],
        compiler_params=_cparams(("parallel", "arbitrary")),
        name="pe_layer",
    )(x, gain.reshape(1, K), w_gate, p, w_pe)


def _mm_norm_res_kernel(a_ref, w_ref, x_ref, g_ref, o_ref, *, tn, rows):
    j = pl.program_id(1)
    o_ref[:, pl.ds(pl.multiple_of(j * tn, tn), tn)] = _dot(a_ref[...], w_ref[...])

    @pl.when(j == pl.num_programs(1) - 1)
    def _():
        tm = o_ref.shape[0]

        def body(i, _):
            sl = pl.ds(pl.multiple_of(i * rows, rows), rows)
            y = o_ref[sl, :]
            ms = jnp.mean(y * y, axis=-1, keepdims=True)
            o_ref[sl, :] = x_ref[sl, :] + y * lax.rsqrt(ms + EPS) * g_ref[...]
            return 0

        lax.fori_loop(0, tm // rows, body, 0)


def _matmul_norm_residual(a, w, x, gain, *, tm, tn, single_buffer_rows):
    M, K = a.shape
    N = w.shape[1]
    tm = _pick_tile(M, tm, SUBLANE)
    tn = _pick_tile(N, tn, LANE)
    rows = _pick_tile(tm, 64, SUBLANE)
    mode = dict(pipeline_mode=ROW_RESIDENT) if single_buffer_rows else {}
    return pl.pallas_call(
        functools.partial(_mm_norm_res_kernel, tn=tn, rows=rows),
        grid=(M // tm, N // tn),
        in_specs=[pl.BlockSpec((tm, K), lambda i, j: (i, 0), **mode),
                  pl.BlockSpec((K, tn), lambda i, j: (0, j)),
                  pl.BlockSpec((tm, N), lambda i, j: (i, 0), **mode),
                  pl.BlockSpec((1, N), lambda i, j: (0, 0))],
        out_specs=pl.BlockSpec((tm, N), lambda i, j: (i, 0)),
        out_shape=jax.ShapeDtypeStruct((M, N), F32),
        compiler_params=_cparams(("parallel", "arbitrary")),
        name="matmul_norm_residual",
    )(a, w, x, gain.reshape(1, N))


def _merge_kernel(oa_ref, ob_ref, wa_ref, wb_ref, ga_ref, gb_ref, o_ref, *, sa, sb):
    tn = o_ref.shape[1]
    ya = _dot(oa_ref[...], wa_ref[...])
    yb = _dot(ob_ref[...], wb_ref[...])
    ga = ga_ref[:, sa:sa + tn]
    gb = gb_ref[:, sb:sb + tn]
    o_ref[...] = (_sigmoid(ga) * ya + _sigmoid(gb) * yb).astype(BF16)


def _gate_window_spec(tm, tn, off, n_cols):
    base, shift, _, _ = _window(off, tn)
    ww = tn if shift == 0 else tn + LANE
    spec = pl.BlockSpec((pl.Element(tm), pl.Element(ww)),
                        lambda i, j: (pl.multiple_of(i * tm, SUBLANE), pl.multiple_of(base + j * tn, LANE)))
    return spec, shift, base + n_cols - tn + ww


def _merge(oa, ob, wa, wb, proj, ga_off, gb_off, *, tm=512, tn=1024):
    M, KA = oa.shape
    KB = ob.shape[1]
    N = wa.shape[1]
    tm = _pick_tile(M, tm, SUBLANE)
    tn = _pick_tile(N, tn, LANE)
    ga_spec, sa, enda = _gate_window_spec(tm, tn, ga_off, N)
    gb_spec, sb, endb = _gate_window_spec(tm, tn, gb_off, N)
    assert max(enda, endb) <= proj.shape[1]
    return pl.pallas_call(
        functools.partial(_merge_kernel, sa=sa, sb=sb),
        grid=(M // tm, N // tn),
        in_specs=[pl.BlockSpec((tm, KA), lambda i, j: (i, 0)),
                  pl.BlockSpec((tm, KB), lambda i, j: (i, 0)),
                  pl.BlockSpec((KA, tn), lambda i, j: (0, j)),
                  pl.BlockSpec((KB, tn), lambda i, j: (0, j)),
                  ga_spec, gb_spec],
        out_specs=pl.BlockSpec((tm, tn), lambda i, j: (i, j)),
        out_shape=jax.ShapeDtypeStruct((M, N), BF16),
        compiler_params=_cparams(("parallel", "arbitrary")),
        name="merge",
    )(oa, ob, wa, wb, proj, proj)


def _gla_kernel(q_ref, k_ref, v_ref, za_ref, zg_ref, wa_ref, ba_ref, gn_ref, s0_ref,
                o_ref, sn_ref, s_ref, *, scale, c, heads, zshift, single_chunk):
    l = pl.program_id(1)
    bb = q_ref.shape[0]
    dk = q_ref.shape[2] // heads
    dv = v_ref.shape[2] // heads

    if not single_chunk:
        @pl.when(l == 0)
        def _():
            s_ref[...] = s0_ref[...].reshape(bb * heads, dk, dv)

    x = _dot(za_ref[...].reshape(bb * c, LANE).astype(BF16), wa_ref[...]) + ba_ref[...]
    la_seqs = _log_sigmoid(x) * (1.0 / GLA_TAU)
    tri = _tri_incl(c)
    tri01 = jnp.where(tri, 1.0, 0.0).astype(BF16)
    ones = jnp.ones((c, LANE), BF16)
    la_seq = [la_seqs[bi * c:(bi + 1) * c] for bi in range(bb)]
    b_seq = [_ones_dot(tri01, la) for la in la_seq]

    def head(bi, h):
        ks = slice(h * dk, (h + 1) * dk)
        vs = slice(h * dv, (h + 1) * dv)
        la = la_seq[bi][:, ks]
        b = b_seq[bi][:, ks]
        b_end = b[c - 1:c, :]
        q = q_ref[bi, :, ks] * scale
        k = k_ref[bi, :, ks]
        v = v_ref[bi, :, vs].astype(BF16)
        qd = (q * jnp.exp(b)).astype(BF16)
        kd = (k * jnp.exp(-b)).astype(BF16)
        s = s0_ref[bi, h] if single_chunk else s_ref[bi * heads + h]
        att = _dot_nt(qd, kd)
        o = _dot(qd, s.astype(BF16))
        k_end = (k * jnp.exp(b_end - b)).astype(BF16)
        la_hi, la_lo = _split2(la)
        dec = jnp.exp(_dot_tn(la_hi, ones) + _dot_tn(la_lo, ones))
        dec = jnp.concatenate([dec] * (dv // LANE), axis=1)
        s_new = s * dec + _dot_tn(k_end, v)
        if single_chunk:
            sn_ref[bi, h] = s_new
        else:
            s_ref[bi * heads + h] = s_new
        yield
        o = o + _dot(jnp.where(tri, att, 0.0).astype(BF16), v)
        yield
        ms = jnp.mean(o * o, axis=-1, keepdims=True)
        on = o * lax.rsqrt(ms + EPS) * gn_ref[...]
        zg = zg_ref[bi, :, zshift + h * dv:zshift + (h + 1) * dv]
        o_ref[bi, :, vs] = (on * (zg * _sigmoid(zg))).astype(BF16)

    _lockstep([head(bi, h) for bi in range(bb) for h in range(heads)])

    if not single_chunk:
        @pl.when(l == pl.num_programs(1) - 1)
        def _():
            sn_ref[...] = s_ref[...].reshape(bb, heads, dk, dv)


def _gla(proj3, offs, w_alpha2p, b_alpha, gla_norm, s0, *, dk, dv, heads):
    B, L, _ = proj3.shape
    c = min(GLA_CHUNK, L)
    qk, gw = heads * dk, heads * dv
    assert L % c == 0 and dv % LANE == 0
    for name, wdt in (("q", qk), ("k", qk), ("v", gw), ("za", LANE)):
        assert offs[name] % wdt == 0
    jq, jk, jv, jz = offs["q"] // qk, offs["k"] // qk, offs["v"] // gw, offs["za"] // LANE
    zbase, zshift, zww, _ = _window(offs["zg"], gw)
    assert zbase + zww <= proj3.shape[2]
    single_chunk = L == c
    bb = _pick_tile(B, 2, 1) if single_chunk else 1
    return pl.pallas_call(
        functools.partial(_gla_kernel, scale=dk ** -0.5, c=c, heads=heads, zshift=zshift,
                          single_chunk=single_chunk),
        grid=(B // bb, L // c),
        in_specs=[pl.BlockSpec((bb, c, qk), lambda b, l: (b, l, jq)),
                  pl.BlockSpec((bb, c, qk), lambda b, l: (b, l, jk)),
                  pl.BlockSpec((bb, c, gw), lambda b, l: (b, l, jv)),
                  pl.BlockSpec((bb, c, LANE), lambda b, l: (b, l, jz)),
                  pl.BlockSpec((pl.Element(bb), pl.Element(c), pl.Element(zww)),
                               lambda b, l: (b * bb, pl.multiple_of(l * c, c), zbase)),
                  pl.BlockSpec((LANE, qk), lambda b, l: (0, 0)),
                  pl.BlockSpec((1, qk), lambda b, l: (0, 0)),
                  pl.BlockSpec((1, dv), lambda b, l: (0, 0)),
                  pl.BlockSpec((bb, heads, dk, dv), lambda b, l: (b, 0, 0, 0))],
        out_specs=[pl.BlockSpec((bb, c, gw), lambda b, l: (b, l, 0)),
                   pl.BlockSpec((bb, heads, dk, dv), lambda b, l: (b, 0, 0, 0))],
        out_shape=[jax.ShapeDtypeStruct((B, L, gw), BF16),
                   jax.ShapeDtypeStruct((B, heads, dk, dv), F32)],
        scratch_shapes=[pltpu.VMEM((SUBLANE, LANE) if single_chunk else (bb * heads, dk, dv), F32)],
        compiler_params=_cparams(("parallel", "arbitrary")),
        name="gla",
    )(proj3, proj3, proj3, proj3, proj3, w_alpha2p, b_alpha.reshape(1, -1), gla_norm.reshape(1, dv), s0)


def _rwkv_prep_kernel(r_ref, xw_ref, kr_ref, vr_ref, xa_ref, xg_ref,
                      sr_ref, sxw_ref, skr_ref, svr_ref, sxa_ref, sxg_ref,
                      mr_ref, mxw_ref, mkr_ref, mvr_ref, mxa_ref, mxg_ref,
                      w0_ref, wd_ref, a0_ref, wi_ref, wg_ref,
                      ro_ref, lw_ref, ko_ref, vo_ref, ao_ref, go_ref,
                      cr_ref, cxw_ref, ckr_ref, cvr_ref, cxa_ref, cxg_ref, *, shifts):
    l = pl.program_id(1)
    bb, tl = r_ref.shape[0], r_ref.shape[1]
    sh_r, sh_xw, sh_kr, sh_vr, sh_xa, sh_xg = shifts

    def shift_mix(z_ref, s0, s_ref, m_ref, c_ref):
        cur = z_ref[:, :, s0:s0 + m_ref.shape[1]]
        first = jnp.where(l == 0, s_ref[...], c_ref[...])
        row = lax.broadcasted_iota(jnp.int32, cur.shape, 1)
        prev = jnp.where(row == 0, first, pltpu.roll(cur, 1, 1))
        c_ref[...] = cur[:, tl - 1:tl, :]
        return cur + (prev - cur) * m_ref[...]

    def flat(x):
        return x.reshape(bb * tl, x.shape[2])

    ro_ref[...] = shift_mix(r_ref, sh_r, sr_ref, mr_ref, cr_ref)
    ko_ref[...] = shift_mix(kr_ref, sh_kr, skr_ref, mkr_ref, ckr_ref)
    vo_ref[...] = shift_mix(vr_ref, sh_vr, svr_ref, mvr_ref, cvr_ref)
    xw = flat(shift_mix(xw_ref, sh_xw, sxw_ref, mxw_ref, cxw_ref))
    xa = flat(shift_mix(xa_ref, sh_xa, sxa_ref, mxa_ref, cxa_ref))
    xg = flat(shift_mix(xg_ref, sh_xg, sxg_ref, mxg_ref, cxg_ref))
    z = w0_ref[...] + _dot(jnp.tanh(xw).astype(BF16), wd_ref[...])
    rw = z.shape[1]
    lw_ref[...] = (-DECAY_SCALE * _sigmoid(z)).reshape(bb, tl, rw)
    ao_ref[...] = _sigmoid(a0_ref[...] + _dot(xa.astype(BF16), wi_ref[...])).reshape(bb, tl, rw)
    go_ref[...] = _dot(_sigmoid(xg).astype(BF16), wg_ref[...]).reshape(bb, tl, rw)


def _rwkv_prep(proj3, offs, shift_parts, mu_parts, w0, w_decay2p, a0, w_iclr2p, w_gate2, *, rw, gl):
    B, L, _ = proj3.shape
    tl = _pick_tile(L, 256, SUBLANE)
    bb = _pick_tile(B, max(1, 256 // L), 1) if tl == L else 1
    names = ("r", "xw", "kr", "vr", "xa", "xg")
    widths = {"r": rw, "xw": LANE, "kr": rw, "vr": rw, "xa": LANE, "xg": gl}
    in_specs, args, shifts = [], [], []
    for n in names:
        base, shift, ww, wp = _window(offs[n], widths[n])
        assert wp == widths[n] and base + ww <= proj3.shape[2]
        shifts.append(shift)
        in_specs.append(pl.BlockSpec(
            (pl.Element(bb), pl.Element(tl), pl.Element(ww)),
            functools.partial(lambda b, l, base: (b * bb, pl.multiple_of(l * tl, SUBLANE), base), base=base)))
        args.append(proj3)
    for n in names:
        in_specs.append(pl.BlockSpec((bb, 1, widths[n]), lambda b, l: (b, 0, 0)))
        args.append(shift_parts[n])
    for n in names:
        in_specs.append(pl.BlockSpec((1, widths[n]), lambda b, l: (0, 0)))
        args.append(mu_parts[n])
    for arr in (w0.reshape(1, rw), w_decay2p, a0.reshape(1, rw), w_iclr2p, w_gate2):
        in_specs.append(pl.BlockSpec(arr.shape, lambda b, l: (0, 0)))
        args.append(arr)
    out_spec = pl.BlockSpec((bb, tl, rw), lambda b, l: (b, l, 0))
    out_sd = jax.ShapeDtypeStruct((B, L, rw), F32)
    return pl.pallas_call(
        functools.partial(_rwkv_prep_kernel, shifts=tuple(shifts)),
        grid=(B // bb, L // tl),
        in_specs=in_specs,
        out_specs=[out_spec] * 6,
        out_shape=[out_sd] * 6,
        scratch_shapes=[pltpu.VMEM((bb, 1, widths[n]), F32) for n in names],
        compiler_params=_cparams(("parallel", "arbitrary")),
        name="rwkv_prep",
    )(*args)


def _rwkv_chunk_kernel(r_ref, lw_ref, kr_ref, v_ref, a_ref, g_ref,
                       kk_w_ref, ka_w_ref, rk_w_ref, lnw_ref, lnb_ref, s0_ref,
                       o_ref, sn_ref, w_ref, *, c, hd, gp):
    l = pl.program_id(2)
    tl = r_ref.shape[1]
    n2 = 2 * c
    lane_c = lax.broadcasted_iota(jnp.int32, (c, LANE), 1)
    head0_c = lane_c < hd
    row2 = lax.broadcasted_iota(jnp.int32, (n2, n2), 0)
    col2 = lax.broadcasted_iota(jnp.int32, (n2, n2), 1)
    same = (2 * row2 + 1 - n2) * (2 * col2 + 1 - n2) > 0
    strict = jnp.logical_and(same, row2 > col2)
    incl = jnp.logical_and(same, row2 >= col2)
    eye = jnp.where(row2 == col2, 1.0, 0.0)
    tri = jnp.where(_tri_incl(c), 1.0, 0.0).astype(BF16)
    lr = lax.broadcasted_iota(jnp.int32, (LANE, LANE), 0)
    lc = lax.broadcasted_iota(jnp.int32, (LANE, LANE), 1)
    seg = jnp.where((2 * lr + 1 - LANE) * (2 * lc + 1 - LANE) > 0, 1.0, 0.0).astype(BF16)

    @pl.when(l == 0)
    def _():
        zero = jnp.zeros((hd, hd), F32)
        for gi in range(gp):
            s_e = s0_ref[0, 2 * gi]
            s_o = s0_ref[0, 2 * gi + 1]
            w_ref[gi] = jnp.concatenate([jnp.concatenate([s_e, zero], axis=1),
                                         jnp.concatenate([zero, s_o], axis=1)], axis=0)

    def stack(x):
        return jnp.concatenate([jnp.where(head0_c, x, 0.0), jnp.where(head0_c, 0.0, x)], axis=0)

    def unstack(x):
        return x[0:c] + x[c:n2]

    def chunk_pair(sl, gi):
        ls = slice(gi * LANE, (gi + 1) * LANE)
        r = r_ref[0, sl, ls]
        lw = lw_ref[0, sl, ls]
        kr = kr_ref[0, sl, ls]
        v = v_ref[0, sl, ls]
        a = a_ref[0, sl, ls]
        kk = kr * kk_w_ref[:, ls]
        kk_ss = _dot_ones(kk * kk, seg)
        cum = _ones_dot(tri, lw)
        yield
        kk = kk / jnp.maximum(jnp.sqrt(kk_ss), KK_EPS)
        k = kr * (1.0 + (a - 1.0) * ka_w_ref[:, ls])
        cum_end = cum[c - 1:c, :]
        e_neg = jnp.exp(-cum)
        e_rem = jnp.exp(cum_end - cum)
        al = kk * a
        al_t = stack(al * e_neg).astype(BF16)
        k_t = stack(k * e_neg).astype(BF16)
        be_t = stack(-kk * jnp.exp(cum - lw)).astype(BF16)
        r_t = stack(r * jnp.exp(cum)).astype(BF16)
        v_s = stack(v).astype(BF16)
        lhs = jnp.concatenate([be_t, r_t], axis=0)
        sc_a = _dot_nt(lhs, al_t)
        sc_k = _dot_nt(lhs, k_t)
        w = w_ref[gi]
        rd = _dot_nt(lhs, w.astype(BF16))
        yield
        l_a =jnp.where(strict, sc_a[0:n2], 0.0)
        l_k = jnp.where(strict, sc_k[0:n2], 0.0).astype(BF16)
        m_a = jnp.where(incl, sc_a[n2:2 * n2], 0.0).astype(BF16)
        m_k = jnp.where(incl, sc_k[n2:2 * n2], 0.0).astype(BF16)
        t_inv = eye + l_a
        lp = l_a
        rhs_u = rd[0:n2] + _dot(l_k, v_s)
        o_s = rd[n2:2 * n2] + _dot(m_k, v_s)
        bonus = _dot_ones(r * k * rk_w_ref[:, ls], seg) * v
        span = 1
        while 2 * span < c:
            lpb = lp.astype(BF16)
            lp = _dot(lpb, lpb)
            yield
            t_inv = t_inv + _dot(t_inv.astype(BF16), lp.astype(BF16))
            yield
            span *= 2
        u_s = _dot(t_inv.astype(BF16), rhs_u.astype(BF16))
        yield
        u_b = u_s.astype(BF16)
        o_s = o_s + _dot(m_a, u_b)
        uv = jnp.concatenate([u_b, v_s], axis=0)
        ak = jnp.concatenate([stack(al * e_rem), stack(k * e_rem)], axis=0).astype(BF16)
        w_ref[gi] = w * jnp.exp(cum_end) + _dot_tn(uv, ak)
        yield
        o = unstack(o_s)
        mu = _dot_ones(o, seg) * (1.0 / hd)
        yield
        d = o - mu
        var = _dot_ones(d * d, seg) * (1.0 / hd)
        yield
        on = d * lax.rsqrt(var + GN_EPS) * lnw_ref[:, ls] + lnb_ref[:, ls]
        o_ref[0, sl, ls] = ((on + bonus) * g_ref[0, sl, ls]).astype(BF16)

    def chunk(ci, _):
        sl = pl.ds(pl.multiple_of(ci * c, c), c)
        _lockstep([chunk_pair(sl, gi) for gi in range(gp)])
        return 0

    lax.fori_loop(0, tl // c, chunk, 0)

    @pl.when(l == pl.num_programs(2) - 1)
    def _():
        for gi in range(gp):
            wf = w_ref[gi]
            sn_ref[0, 2 * gi] = wf[0:hd, 0:hd]
            sn_ref[0, 2 * gi + 1] = wf[hd:2 * hd, hd:2 * hd]


def _rwkv_chunk(r, lw, kr, vr, a, g, k_k, k_a, r_k, ln_w, ln_b, s0p, *, hd):
    B, L, RW = r.shape
    assert 2 * hd == LANE and RW % LANE == 0
    c = min(RWKV_CHUNK, L)
    tl = _pick_tile(L, 256, c)
    assert L % c == 0 and tl % c == 0 and c % SUBLANE == 0
    npair = RW // LANE
    gp = _pick_tile(npair, 16, 1)
    seq = pl.BlockSpec((1, tl, gp * LANE), lambda b, p, l: (b, l, p))
    par = pl.BlockSpec((1, gp * LANE), lambda b, p, l: (0, p))
    st = pl.BlockSpec((1, 2 * gp, hd, hd), lambda b, p, l: (b, p, 0, 0))
    return pl.pallas_call(
        functools.partial(_rwkv_chunk_kernel, c=c, hd=hd, gp=gp),
        grid=(B, npair // gp, L // tl),
        in_specs=[seq] * 6 + [par] * 5 + [st],
        out_specs=[seq, st],
        out_shape=[jax.ShapeDtypeStruct((B, L, RW), BF16),
                   jax.ShapeDtypeStruct((B, 2 * npair, hd, hd), F32)],
        scratch_shapes=[pltpu.VMEM((gp, LANE, LANE), F32)],
        compiler_params=_cparams(("parallel", "parallel", "arbitrary")),
        name="rwkv_chunk",
    )(r, lw, kr, vr, a, g, k_k.reshape(1, RW), k_a.reshape(1, RW), r_k.reshape(1, RW),
      ln_w.reshape(1, RW), ln_b.reshape(1, RW), s0p)


def _up_conv_kernel(x_ref, gn_ref, wg_ref, wv_ref, s_ref, cw_ref, cb_ref, act_ref, nc_ref, h_ref, c_ref,
                    *, rows, tiles_per_seq, seqs):
    i, j = pl.program_id(0), pl.program_id(1)
    tm = x_ref.shape[0]
    tl = tm // seqs

    @pl.when(j == 0)
    def _():
        _rms_rows(x_ref, gn_ref, h_ref, rows)

    h = h_ref[...]
    g = _dot(h, wg_ref[...].astype(BF16))
    v = _dot(h, wv_ref[...].astype(BF16))
    tn = g.shape[1]
    g = g.reshape(seqs, tl, tn)
    first = (i % tiles_per_seq) == 0
    prev = jnp.where(first, s_ref[...], c_ref[j])
    row = lax.broadcasted_iota(jnp.int32, g.shape, 1)
    g1 = jnp.where(row == 0, prev[:, 1:2], pltpu.roll(g, 1, 1))
    g2 = jnp.where(row == 0, prev[:, 0:1], jnp.where(row == 1, prev[:, 1:2], pltpu.roll(g, 2, 1)))
    last2 = g[:, tl - 2:tl, :]
    c_ref[j] = last2
    nc_ref[...] = last2
    conv = cb_ref[...] + g2 * cw_ref[0:1, :] + g1 * cw_ref[1:2, :] + g * cw_ref[2:3, :]
    act_ref[...] = (_gelu_tanh(conv).reshape(tm, tn) * v).astype(BF16)


def _up_conv_act(x, gain, w_up, s_conv, conv_w, conv_b, *, seq_len, tm=1024, tn=256):
    M, K = x.shape
    dff = conv_b.shape[0]
    if seq_len >= tm:
        tm = _pick_tile(seq_len, tm, SUBLANE)
        seqs, tps = 1, seq_len // tm
    else:
        assert seq_len % SUBLANE == 0
        seqs, tps = _pick_tile(M // seq_len, tm // seq_len, 1), 1
        tm = seqs * seq_len
    tn = _pick_tile(dff, tn, LANE)
    nj = dff // tn
    rows = _pick_tile(tm, 64, SUBLANE)
    assert conv_w.shape[0] == 3 and s_conv.shape[1] == 2 and tm // seqs >= 2
    act, tails = pl.pallas_call(
        functools.partial(_up_conv_kernel, rows=rows, tiles_per_seq=tps, seqs=seqs),
        grid=(M // tm, nj),
        in_specs=[pl.BlockSpec((tm, K), lambda i, j: (i, 0), pipeline_mode=ROW_RESIDENT),
                  pl.BlockSpec((1, K), lambda i, j: (0, 0)),
                  pl.BlockSpec((K, tn), lambda i, j: (0, j)),
                  pl.BlockSpec((K, tn), lambda i, j: (0, nj + j)),
                  pl.BlockSpec((seqs, 2, tn), lambda i, j: (i // tps, 0, j)),
                  pl.BlockSpec((3, tn), lambda i, j: (0, j)),
                  pl.BlockSpec((1, tn), lambda i, j: (0, j))],
        out_specs=[pl.BlockSpec((tm, tn), lambda i, j: (i, j)),
                   pl.BlockSpec((seqs, 2, tn), lambda i, j: (i, 0, j))],
        out_shape=[jax.ShapeDtypeStruct((M, dff), BF16),
                   jax.ShapeDtypeStruct((M // tm * seqs, 2, dff), F32)],
        scratch_shapes=[pltpu.VMEM((tm, K), BF16), pltpu.VMEM((nj, seqs, 2, tn), F32)],
        compiler_params=_cparams(("arbitrary", "arbitrary")),
        name="up_conv_act",
    )(x, gain.reshape(1, K), w_up, w_up, s_conv, conv_w, conv_b.reshape(1, dff))
    return act, tails[tps - 1::tps]


def _pad_cols(a, width):
    return a if a.shape[-1] == width else jnp.pad(a, [(0, 0)] * (a.ndim - 1) + [(0, width - a.shape[-1])])


def _pad_rows(a, height):
    return a if a.shape[0] == height else jnp.pad(a, [(0, height - a.shape[0])] + [(0, 0)] * (a.ndim - 1))


def _layout(D, QK, GW, R, RW, WL, AL, GL):
    src, o = {}, 0
    for name, w in (("q", QK), ("k", QK), ("v", GW), ("za", R), ("zg", GW), ("r", RW), ("xw", WL),
                    ("kr", RW), ("vr", RW), ("xa", AL), ("xg", GL), ("ga", D), ("gb", D)):
        src[name] = (o, w)
        o += w
    return src


def _prepare_params(lp):
    (w_in, w_alpha2, b_alpha, gla_norm, w_branch_a,
     mu_shift, w0, w_decay2, a0, w_iclr2, w_gate2, k_k, k_a, r_k, ln_x_w, ln_x_b, w_branch_b,
     w_out, g_pre_mix, g_post_mix, g_pre_ffn, g_post_ffn,
     w_up, conv_w, conv_b, w_down, g_pe, w_pe_gate, w_pe) = lp
    D = w_in.shape[0]
    R, QK = w_alpha2.shape
    GW = w_branch_a.shape[0]
    RW = w_branch_b.shape[0]
    WL, AL, GL = w_decay2.shape[0], w_iclr2.shape[0], w_gate2.shape[0]
    src = _layout(D, QK, GW, R, RW, WL, AL, GL)
    pad = lambda w: -(-w // LANE) * LANE
    prm = dict(
        dims=dict(D=D, R=R, QK=QK, GW=GW, RW=RW, WL=WL, AL=AL, GL=GL, DV=gla_norm.shape[0],
                  HD=r_k.shape[1], DFF=conv_b.shape[0]),
        src=src, off={n: src[n][0] for n in src},
        w_in_t=w_in.T,
        w_alpha2=_pad_rows(w_alpha2, LANE).astype(BF16), b_alpha=b_alpha, gla_norm=gla_norm,
        w_branch_a=w_branch_a.astype(BF16), w_branch_b=w_branch_b.astype(BF16),
        w0=w0, w_decay2=_pad_rows(w_decay2, LANE).astype(BF16), a0=a0,
        w_iclr2=_pad_rows(w_iclr2, LANE).astype(BF16), w_gate2=w_gate2.astype(BF16),
        k_k=k_k, k_a=k_a, r_k=r_k.reshape(-1), ln_x_w=ln_x_w, ln_x_b=ln_x_b,
        w_out=w_out.astype(BF16), g_pre_mix=g_pre_mix, g_post_mix=g_post_mix,
        g_pre_ffn=g_pre_ffn, g_post_ffn=g_post_ffn,
        w_up=w_up, conv_w=conv_w, conv_b=conv_b, w_down=w_down.astype(BF16),
        g_pe=g_pe, w_pe_gate=w_pe_gate, w_pe=w_pe.astype(BF16),
    )
    rsrc = src["r"][0]
    rnames = ("r", "xw", "kr", "vr", "xa", "xg")
    prm["rnames"] = rnames
    prm["rsl"] = {n: (src[n][0] - rsrc, src[n][1]) for n in rnames}
    prm["mu"] = {n: _pad_cols(mu_shift[None, prm["rsl"][n][0]:prm["rsl"][n][0] + prm["rsl"][n][1]],
                              pad(prm["rsl"][n][1])) for n in rnames}
    return prm


def _layer(x, p, s_gla, s_rwkv, s_shift, s_conv, prm):
    B, L, D = x.shape
    M = B * L
    dm, dst = prm["dims"], prm["off"]
    GW, RW, DV, HD, DFF, QK = dm["GW"], dm["RW"], dm["DV"], dm["HD"], dm["DFF"], dm["QK"]
    heads = GW // DV
    pad = lambda w: -(-w // LANE) * LANE
    x2 = x.reshape(M, D)

    proj = _norm_matmul_nt(x2, prm["g_pre_mix"], prm["w_in_t"])
    proj3 = proj.reshape(B, L, -1)

    o_a, s_gla_new = _gla(proj3, dst, prm["w_alpha2"], prm["b_alpha"], prm["gla_norm"], s_gla,
                          dk=QK // heads, dv=DV, heads=heads)

    shift_parts = {n: _pad_cols(s_shift[:, None, prm["rsl"][n][0]:prm["rsl"][n][0] + prm["rsl"][n][1]],
                                pad(prm["rsl"][n][1])) for n in prm["rnames"]}
    r, lw, kr, vr, a, g = _rwkv_prep(proj3, dst, shift_parts, prm["mu"], prm["w0"], prm["w_decay2"],
                                     prm["a0"], prm["w_iclr2"], prm["w_gate2"], rw=RW, gl=dm["GL"])
    o_b, s_rwkv_new = _rwkv_chunk(r, lw, kr, vr, a, g, prm["k_k"], prm["k_a"], prm["r_k"],
                                  prm["ln_x_w"], prm["ln_x_b"], s_rwkv, hd=HD)
    new_shift = proj3[:, L - 1, dst["r"]:dst["ga"]]

    mixed = _merge(o_a.reshape(M, GW), o_b.reshape(M, RW), prm["w_branch_a"], prm["w_branch_b"],
                   proj, dst["ga"], dst["gb"])
    x2 = _matmul_norm_residual(mixed, prm["w_out"], x2, prm["g_post_mix"], tm=512, tn=512,
                               single_buffer_rows=False)

    act, new_conv = _up_conv_act(x2, prm["g_pre_ffn"], prm["w_up"], s_conv, prm["conv_w"],
                                 prm["conv_b"], seq_len=L)
    x2 = _matmul_norm_residual(act.reshape(M, DFF), prm["w_down"], x2, prm["g_post_ffn"], tm=512, tn=256,
                               single_buffer_rows=True)

    x2 = _pe_layer(x2, prm["g_pe"], prm["w_pe_gate"], p.reshape(M, -1), prm["w_pe"])
    return x2.reshape(B, L, D), s_gla_new, s_rwkv_new, new_shift, new_conv


def kernel(x_prompt, x_sample, state_gla, state_rwkv, state_shift, state_ffn_conv, p_prompt, p_sample, w_in, w_alpha2, b_alpha, gla_norm, w_branch_a, mu_shift, w0, w_decay2, a0, w_iclr2, w_gate2, k_k, k_a, r_k, ln_x_w, ln_x_b, w_branch_b, w_out, g_pre_mix, g_post_mix, g_pre_ffn, g_post_ffn, w_up, conv_w, conv_b, w_down, g_pe, w_pe_gate, w_pe):
    params = (w_in, w_alpha2, b_alpha, gla_norm, w_branch_a,
              mu_shift, w0, w_decay2, a0, w_iclr2, w_gate2, k_k, k_a, r_k, ln_x_w, ln_x_b, w_branch_b,
              w_out, g_pre_mix, g_post_mix, g_pre_ffn, g_post_ffn,
              w_up, conv_w, conv_b, w_down, g_pe, w_pe_gate, w_pe)
    depth = w_in.shape[0]
    nb = x_prompt.shape[0]
    yp, ys = x_prompt, x_sample
    outs_p = [[], [], [], []]
    outs_s = [[], [], [], []]
    for i in range(depth):
        prm = _prepare_params(tuple(t[i] for t in params))
        z_gla = jnp.zeros((nb,) + state_gla.shape[2:], F32)
        z_rwkv = jnp.zeros((nb,) + state_rwkv.shape[2:], F32)
        z_shift = jnp.zeros((nb,) + state_shift.shape[2:], x_prompt.dtype)
        z_conv = jnp.zeros((nb,) + state_ffn_conv.shape[2:], x_prompt.dtype)
        yp, *st = _layer(yp, p_prompt[i], z_gla, z_rwkv, z_shift, z_conv, prm)
        for acc, s in zip(outs_p, st):
            acc.append(s)
        ys, *st = _layer(ys, p_sample[i], state_gla[i], state_rwkv[i], state_shift[i],
                         state_ffn_conv[i], prm)
        for acc, s in zip(outs_s, st):
            acc.append(s)
    return (yp, ys, *(jnp.stack(a) for a in outs_p), *(jnp.stack(a) for a in outs_s))
```

```python
import functools
import math

import jax
import jax.numpy as jnp
from jax import lax
from jax.experimental import pallas as pl
from jax.experimental.pallas import tpu as pltpu

F32 = jnp.float32
BF16 = jnp.bfloat16

LANE = 128
SUBLANE = 8
VMEM_LIMIT_BYTES = 56 * 1024 * 1024

EPS = 1e-6
GN_EPS = 64e-5
GLA_TAU = 16.0
GLA_CHUNK = 64
RWKV_CHUNK = 64
KK_EPS = 1e-12
DECAY_SCALE = math.exp(-0.5)
GELU_C = math.sqrt(2.0 / math.pi)


ROW_RESIDENT = pl.Buffered(1)


def _cparams(sem):
    return pltpu.CompilerParams(dimension_semantics=sem, vmem_limit_bytes=VMEM_LIMIT_BYTES)


def _sigmoid(x):
    return 1.0 / (1.0 + jnp.exp(-x))


def _log_sigmoid(x):
    return jnp.minimum(x, 0.0) - jnp.log(1.0 + jnp.exp(-jnp.abs(x)))


def _gelu_tanh(x):
    return 0.5 * x * (1.0 + jnp.tanh(GELU_C * (x + 0.044715 * (x * x * x))))


def _split2(x):
    hi = x.astype(BF16)
    lo = (x - hi.astype(F32)).astype(BF16)
    return hi, lo


def _dot(a, b):
    return jnp.dot(a, b, preferred_element_type=F32)


def _dot_nt(a, b):
    return lax.dot_general(a, b, (((1,), (1,)), ((), ())), preferred_element_type=F32)


def _dot_tn(a, b):
    return lax.dot_general(a, b, (((0,), (0,)), ((), ())), preferred_element_type=F32)


def _ones_dot(m01, x):
    hi, lo = _split2(x)
    return _dot(m01, hi) + _dot(m01, lo)


def _dot_ones(x, m01):
    return _dot(x.astype(BF16), m01)


def _lockstep(gens):
    gens = list(gens)
    while gens:
        alive = []
        for g in gens:
            try:
                next(g)
                alive.append(g)
            except StopIteration:
                pass
        gens = alive


def _tri_incl(n):
    r = lax.broadcasted_iota(jnp.int32, (n, n), 0)
    c = lax.broadcasted_iota(jnp.int32, (n, n), 1)
    return r >= c


def _pick_tile(n, target, mult):
    if n <= target:
        return n
    best = None
    t = mult
    while t <= target:
        if n % t == 0:
            best = t
        t += mult
    assert best is not None, (n, target, mult)
    return best


def _rms_rows(x_ref, g_ref, h_ref, rows):
    tm = x_ref.shape[0]

    def body(i, _):
        sl = pl.ds(pl.multiple_of(i * rows, rows), rows)
        x = x_ref[sl, :]
        ms = jnp.mean(x * x, axis=-1, keepdims=True)
        h_ref[sl, :] = (x * lax.rsqrt(ms + EPS) * g_ref[...]).astype(BF16)
        return 0

    lax.fori_loop(0, tm // rows, body, 0)


def _norm_mm_kernel(x_ref, g_ref, wt_ref, o_ref, h_ref, *, rows, n_valid):
    j = pl.program_id(1)

    @pl.when(j == 0)
    def _():
        _rms_rows(x_ref, g_ref, h_ref, rows)

    y = _dot_nt(h_ref[...], wt_ref[...].astype(BF16))
    tn = y.shape[1]
    if n_valid % tn:
        col = lax.broadcasted_iota(jnp.int32, y.shape, 1) + j * tn
        y = jnp.where(col < n_valid, y, 0.0)
    o_ref[...] = y


def _window(src, width):
    base = src // LANE * LANE
    shift = src - base
    wp = -(-width // LANE) * LANE
    return base, shift, (wp if shift == 0 else wp + LANE), wp


def _norm_matmul_nt(x, gain, wt, *, tm=1024, tn=512):
    M, K = x.shape
    nw = wt.shape[0]
    tm = _pick_tile(M, tm, SUBLANE)
    tn = min(tn, -(-nw // LANE) * LANE)
    N = -(-nw // tn) * tn
    rows = _pick_tile(tm, 64, SUBLANE)
    return pl.pallas_call(
        functools.partial(_norm_mm_kernel, rows=rows, n_valid=nw),
        grid=(M // tm, N // tn),
        in_specs=[pl.BlockSpec((tm, K), lambda i, j: (i, 0), pipeline_mode=ROW_RESIDENT),
                  pl.BlockSpec((1, K), lambda i, j: (0, 0)),
                  pl.BlockSpec((tn, K), lambda i, j: (j, 0))],
        out_specs=pl.BlockSpec((tm, tn), lambda i, j: (i, j)),
        out_shape=jax.ShapeDtypeStruct((M, N), F32),
        scratch_shapes=[pltpu.VMEM((tm, K), BF16)],
        compiler_params=_cparams(("parallel", "arbitrary")),
        name="norm_matmul",
    )(x, gain.reshape(1, K), wt)


def _pe_kernel(x_ref, g_ref, w_ref, p_ref, wp_ref, o_ref, h_ref, *, rows, tn):
    j = pl.program_id(1)

    @pl.when(j == 0)
    def _():
        _rms_rows(x_ref, g_ref, h_ref, rows)

    gate = _sigmoid(_dot(h_ref[...], w_ref[...].astype(BF16)))
    pe = _dot(p_ref[...].astype(BF16), wp_ref[...])
    xs = x_ref[:, pl.ds(pl.multiple_of(j * tn, tn), tn)]
    o_ref[...] = xs + gate * pe


def _pe_layer(x, gain, w_gate, p, w_pe, *, tm=1024, tn=512):
    M, K = x.shape
    N = w_gate.shape[1]
    P = p.shape[1]
    tm = _pick_tile(M, tm, SUBLANE)
    tn = _pick_tile(N, tn, LANE)
    rows = _pick_tile(tm, 64, SUBLANE)
    return pl.pallas_call(
        functools.partial(_pe_kernel, rows=rows, tn=tn),
        grid=(M // tm, N // tn),
        in_specs=[pl.BlockSpec((tm, K), lambda i, j: (i, 0), pipeline_mode=ROW_RESIDENT),
                  pl.BlockSpec((1, K), lambda i, j: (0, 0)),
                  pl.BlockSpec((K, tn), lambda i, j: (0, j)),
                  pl.BlockSpec((tm, P), lambda i, j: (i, 0)),
                  pl.BlockSpec((P, tn), lambda i, j: (0, j))],
        out_specs=pl.BlockSpec((tm, tn), lambda i, j: (i, j)),
        out_shape=jax.ShapeDtypeStruct((M, N), F32),
        scratch_shapes=[pltpu.VMEM((tm, K), BF16)],
        compiler_params=_cparams(("parallel", "arbitrary")),
        name="pe_layer",
    )(x, gain.reshape(1, K), w_gate, p, w_pe)


def _mm_norm_res_kernel(a_ref, w_ref, x_ref, g_ref, o_ref, *, tn, rows):
    j = pl.program_id(1)
    o_ref[:, pl.ds(pl.multiple_of(j * tn, tn), tn)] = _dot(a_ref[...], w_ref[...])

    @pl.when(j == pl.num_programs(1) - 1)
    def _():
        tm = o_ref.shape[0]

        def body(i, _):
            sl = pl.ds(pl.multiple_of(i * rows, rows), rows)
            y = o_ref[sl, :]
            ms = jnp.mean(y * y, axis=-1, keepdims=True)
            o_ref[sl, :] = x_ref[sl, :] + y * lax.rsqrt(ms + EPS) * g_ref[...]
            return 0

        lax.fori_loop(0, tm // rows, body, 0)


def _matmul_norm_residual(a, w, x, gain, *, tm, tn, single_buffer_rows):
    M, K = a.shape
    N = w.shape[1]
    tm = _pick_tile(M, tm, SUBLANE)
    tn = _pick_tile(N, tn, LANE)
    rows = _pick_tile(tm, 64, SUBLANE)
    mode = dict(pipeline_mode=ROW_RESIDENT) if single_buffer_rows else {}
    return pl.pallas_call(
        functools.partial(_mm_norm_res_kernel, tn=tn, rows=rows),
        grid=(M // tm, N // tn),
        in_specs=[pl.BlockSpec((tm, K), lambda i, j: (i, 0), **mode),
                  pl.BlockSpec((K, tn), lambda i, j: (0, j)),
                  pl.BlockSpec((tm, N), lambda i, j: (i, 0), **mode),
                  pl.BlockSpec((1, N), lambda i, j: (0, 0))],
        out_specs=pl.BlockSpec((tm, N), lambda i, j: (i, 0)),
        out_shape=jax.ShapeDtypeStruct((M, N), F32),
        compiler_params=_cparams(("parallel", "arbitrary")),
        name="matmul_norm_residual",
    )(a, w, x, gain.reshape(1, N))


def _merge_kernel(oa_ref, ob_ref, wa_ref, wb_ref, ga_ref, gb_ref, o_ref, *, sa, sb):
    tn = o_ref.shape[1]
    ya = _dot(oa_ref[...], wa_ref[...])
    yb = _dot(ob_ref[...], wb_ref[...])
    ga = ga_ref[:, sa:sa + tn]
    gb = gb_ref[:, sb:sb + tn]
    o_ref[...] = (_sigmoid(ga) * ya + _sigmoid(gb) * yb).astype(BF16)


def _gate_window_spec(tm, tn, off, n_cols):
    base, shift, _, _ = _window(off, tn)
    ww = tn if shift == 0 else tn + LANE
    spec = pl.BlockSpec((pl.Element(tm), pl.Element(ww)),
                        lambda i, j: (pl.multiple_of(i * tm, SUBLANE), pl.multiple_of(base + j * tn, LANE)))
    return spec, shift, base + n_cols - tn + ww


def _merge(oa, ob, wa, wb, proj, ga_off, gb_off, *, tm=512, tn=1024):
    M, KA = oa.shape
    KB = ob.shape[1]
    N = wa.shape[1]
    tm = _pick_tile(M, tm, SUBLANE)
    tn = _pick_tile(N, tn, LANE)
    ga_spec, sa, enda = _gate_window_spec(tm, tn, ga_off, N)
    gb_spec, sb, endb = _gate_window_spec(tm, tn, gb_off, N)
    assert max(enda, endb) <= proj.shape[1]
    return pl.pallas_call(
        functools.partial(_merge_kernel, sa=sa, sb=sb),
        grid=(M // tm, N // tn),
        in_specs=[pl.BlockSpec((tm, KA), lambda i, j: (i, 0)),
                  pl.BlockSpec((tm, KB), lambda i, j: (i, 0)),
                  pl.BlockSpec((KA, tn), lambda i, j: (0, j)),
                  pl.BlockSpec((KB, tn), lambda i, j: (0, j)),
                  ga_spec, gb_spec],
        out_specs=pl.BlockSpec((tm, tn), lambda i, j: (i, j)),
        out_shape=jax.ShapeDtypeStruct((M, N), BF16),
        compiler_params=_cparams(("parallel", "arbitrary")),
        name="merge",
    )(oa, ob, wa, wb, proj, proj)


def _gla_kernel(q_ref, k_ref, v_ref, za_ref, zg_ref, wa_ref, ba_ref, gn_ref, s0_ref,
                o_ref, sn_ref, s_ref, *, scale, c, heads, zshift, single_chunk):
    l = pl.program_id(1)
    bb = q_ref.shape[0]
    dk = q_ref.shape[2] // heads
    dv = v_ref.shape[2] // heads

    if not single_chunk:
        @pl.when(l == 0)
        def _():
            s_ref[...] = s0_ref[...].reshape(bb * heads, dk, dv)

    x = _dot(za_ref[...].reshape(bb * c, LANE).astype(BF16), wa_ref[...]) + ba_ref[...]
    la_seqs = _log_sigmoid(x) * (1.0 / GLA_TAU)
    tri = _tri_incl(c)
    tri01 = jnp.where(tri, 1.0, 0.0).astype(BF16)
    ones = jnp.ones((c, LANE), BF16)
    la_seq = [la_seqs[bi * c:(bi + 1) * c] for bi in range(bb)]
    b_seq = [_ones_dot(tri01, la) for la in la_seq]

    def head(bi, h):
        ks = slice(h * dk, (h + 1) * dk)
        vs = slice(h * dv, (h + 1) * dv)
        la = la_seq[bi][:, ks]
        b = b_seq[bi][:, ks]
        b_end = b[c - 1:c, :]
        q = q_ref[bi, :, ks] * scale
        k = k_ref[bi, :, ks]
        v = v_ref[bi, :, vs].astype(BF16)
        qd = (q * jnp.exp(b)).astype(BF16)
        kd = (k * jnp.exp(-b)).astype(BF16)
        s = s0_ref[bi, h] if single_chunk else s_ref[bi * heads + h]
        att = _dot_nt(qd, kd)
        o = _dot(qd, s.astype(BF16))
        k_end = (k * jnp.exp(b_end - b)).astype(BF16)
        la_hi, la_lo = _split2(la)
        dec = jnp.exp(_dot_tn(la_hi, ones) + _dot_tn(la_lo, ones))
        dec = jnp.concatenate([dec] * (dv // LANE), axis=1)
        s_new = s * dec + _dot_tn(k_end, v)
        if single_chunk:
            sn_ref[bi, h] = s_new
        else:
            s_ref[bi * heads + h] = s_new
        yield
        o = o + _dot(jnp.where(tri, att, 0.0).astype(BF16), v)
        yield
        ms = jnp.mean(o * o, axis=-1, keepdims=True)
        on = o * lax.rsqrt(ms + EPS) * gn_ref[...]
        zg = zg_ref[bi, :, zshift + h * dv:zshift + (h + 1) * dv]
        o_ref[bi, :, vs] = (on * (zg * _sigmoid(zg))).astype(BF16)

    _lockstep([head(bi, h) for bi in range(bb) for h in range(heads)])

    if not single_chunk:
        @pl.when(l == pl.num_programs(1) - 1)
        def _():
            sn_ref[...] = s_ref[...].reshape(bb, heads, dk, dv)


def _gla(proj3, offs, w_alpha2p, b_alpha, gla_norm, s0, *, dk, dv, heads):
    B, L, _ = proj3.shape
    c = min(GLA_CHUNK, L)
    qk, gw = heads * dk, heads * dv
    assert L % c == 0 and dv % LANE == 0
    for name, wdt in (("q", qk), ("k", qk), ("v", gw), ("za", LANE)):
        assert offs[name] % wdt == 0
    jq, jk, jv, jz = offs["q"] // qk, offs["k"] // qk, offs["v"] // gw, offs["za"] // LANE
    zbase, zshift, zww, _ = _window(offs["zg"], gw)
    assert zbase + zww <= proj3.shape[2]
    single_chunk = L == c
    bb = _pick_tile(B, 2, 1)
    return pl.pallas_call(
        functools.partial(_gla_kernel, scale=dk ** -0.5, c=c, heads=heads, zshift=zshift,
                          single_chunk=single_chunk),
        grid=(B // bb, L // c),
        in_specs=[pl.BlockSpec((bb, c, qk), lambda b, l: (b, l, jq)),
                  pl.BlockSpec((bb, c, qk), lambda b, l: (b, l, jk)),
                  pl.BlockSpec((bb, c, gw), lambda b, l: (b, l, jv)),
                  pl.BlockSpec((bb, c, LANE), lambda b, l: (b, l, jz)),
                  pl.BlockSpec((pl.Element(bb), pl.Element(c), pl.Element(zww)),
                               lambda b, l: (b * bb, pl.multiple_of(l * c, c), zbase)),
                  pl.BlockSpec((LANE, qk), lambda b, l: (0, 0)),
                  pl.BlockSpec((1, qk), lambda b, l: (0, 0)),
                  pl.BlockSpec((1, dv), lambda b, l: (0, 0)),
                  pl.BlockSpec((bb, heads, dk, dv), lambda b, l: (b, 0, 0, 0))],
        out_specs=[pl.BlockSpec((bb, c, gw), lambda b, l: (b, l, 0)),
                   pl.BlockSpec((bb, heads, dk, dv), lambda b, l: (b, 0, 0, 0))],
        out_shape=[jax.ShapeDtypeStruct((B, L, gw), BF16),
                   jax.ShapeDtypeStruct((B, heads, dk, dv), F32)],
        scratch_shapes=[pltpu.VMEM((SUBLANE, LANE) if single_chunk else (bb * heads, dk, dv), F32)],
        compiler_params=_cparams(("parallel", "arbitrary")),
        name="gla",
    )(proj3, proj3, proj3, proj3, proj3, w_alpha2p, b_alpha.reshape(1, -1), gla_norm.reshape(1, dv), s0)


def _rwkv_prep_kernel(r_ref, xw_ref, kr_ref, vr_ref, xa_ref, xg_ref,
                      sr_ref, sxw_ref, skr_ref, svr_ref, sxa_ref, sxg_ref,
                      mr_ref, mxw_ref, mkr_ref, mvr_ref, mxa_ref, mxg_ref,
                      w0_ref, wd_ref, a0_ref, wi_ref, wg_ref,
                      ro_ref, lw_ref, ko_ref, vo_ref, ao_ref, go_ref,
                      cr_ref, cxw_ref, ckr_ref, cvr_ref, cxa_ref, cxg_ref, *, shifts):
    l = pl.program_id(1)
    bb, tl = r_ref.shape[0], r_ref.shape[1]
    sh_r, sh_xw, sh_kr, sh_vr, sh_xa, sh_xg = shifts

    def shift_mix(z_ref, s0, s_ref, m_ref, c_ref):
        cur = z_ref[:, :, s0:s0 + m_ref.shape[1]]
        first = jnp.where(l == 0, s_ref[...], c_ref[...])
        row = lax.broadcasted_iota(jnp.int32, cur.shape, 1)
        prev = jnp.where(row == 0, first, pltpu.roll(cur, 1, 1))
        c_ref[...] = cur[:, tl - 1:tl, :]
        return cur + (prev - cur) * m_ref[...]

    def flat(x):
        return x.reshape(bb * tl, x.shape[2])

    ro_ref[...] = shift_mix(r_ref, sh_r, sr_ref, mr_ref, cr_ref)
    ko_ref[...] = shift_mix(kr_ref, sh_kr, skr_ref, mkr_ref, ckr_ref)
    vo_ref[...] = shift_mix(vr_ref, sh_vr, svr_ref, mvr_ref, cvr_ref)
    xw = flat(shift_mix(xw_ref, sh_xw, sxw_ref, mxw_ref, cxw_ref))
    xa = flat(shift_mix(xa_ref, sh_xa, sxa_ref, mxa_ref, cxa_ref))
    xg = flat(shift_mix(xg_ref, sh_xg, sxg_ref, mxg_ref, cxg_ref))
    z = w0_ref[...] + _dot(jnp.tanh(xw).astype(BF16), wd_ref[...])
    rw = z.shape[1]
    lw_ref[...] = (-DECAY_SCALE * _sigmoid(z)).reshape(bb, tl, rw)
    ao_ref[...] = _sigmoid(a0_ref[...] + _dot(xa.astype(BF16), wi_ref[...])).reshape(bb, tl, rw)
    go_ref[...] = _dot(_sigmoid(xg).astype(BF16), wg_ref[...]).reshape(bb, tl, rw)


def _rwkv_prep(proj3, offs, shift_parts, mu_parts, w0, w_decay2p, a0, w_iclr2p, w_gate2, *, rw, gl):
    B, L, _ = proj3.shape
    tl = _pick_tile(L, 256, SUBLANE)
    bb = _pick_tile(B, max(1, 256 // L), 1) if tl == L else 1
    names = ("r", "xw", "kr", "vr", "xa", "xg")
    widths = {"r": rw, "xw": LANE, "kr": rw, "vr": rw, "xa": LANE, "xg": gl}
    in_specs, args, shifts = [], [], []
    for n in names:
        base, shift, ww, wp = _window(offs[n], widths[n])
        assert wp == widths[n] and base + ww <= proj3.shape[2]
        shifts.append(shift)
        in_specs.append(pl.BlockSpec(
            (pl.Element(bb), pl.Element(tl), pl.Element(ww)),
            functools.partial(lambda b, l, base: (b * bb, pl.multiple_of(l * tl, SUBLANE), base), base=base)))
        args.append(proj3)
    for n in names:
        in_specs.append(pl.BlockSpec((bb, 1, widths[n]), lambda b, l: (b, 0, 0)))
        args.append(shift_parts[n])
    for n in names:
        in_specs.append(pl.BlockSpec((1, widths[n]), lambda b, l: (0, 0)))
        args.append(mu_parts[n])
    for arr in (w0.reshape(1, rw), w_decay2p, a0.reshape(1, rw), w_iclr2p, w_gate2):
        in_specs.append(pl.BlockSpec(arr.shape, lambda b, l: (0, 0)))
        args.append(arr)
    out_spec = pl.BlockSpec((bb, tl, rw), lambda b, l: (b, l, 0))
    out_sd = jax.ShapeDtypeStruct((B, L, rw), F32)
    return pl.pallas_call(
        functools.partial(_rwkv_prep_kernel, shifts=tuple(shifts)),
        grid=(B // bb, L // tl),
        in_specs=in_specs,
        out_specs=[out_spec] * 6,
        out_shape=[out_sd] * 6,
        scratch_shapes=[pltpu.VMEM((bb, 1, widths[n]), F32) for n in names],
        compiler_params=_cparams(("parallel", "arbitrary")),
        name="rwkv_prep",
    )(*args)


def _rwkv_chunk_kernel(r_ref, lw_ref, kr_ref, v_ref, a_ref, g_ref,
                       kk_w_ref, ka_w_ref, rk_w_ref, lnw_ref, lnb_ref, s0_ref,
                       o_ref, sn_ref, w_ref, *, c, hd, gp):
    l = pl.program_id(2)
    bb, tl = r_ref.shape[0], r_ref.shape[1]
    n2 = 2 * c
    lane_c = lax.broadcasted_iota(jnp.int32, (c, LANE), 1)
    head0_c = lane_c < hd
    row2 = lax.broadcasted_iota(jnp.int32, (n2, n2), 0)
    col2 = lax.broadcasted_iota(jnp.int32, (n2, n2), 1)
    same = (2 * row2 + 1 - n2) * (2 * col2 + 1 - n2) > 0
    strict = jnp.logical_and(same, row2 > col2)
    incl = jnp.logical_and(same, row2 >= col2)
    eye = jnp.where(row2 == col2, 1.0, 0.0)
    tri = jnp.where(_tri_incl(c), 1.0, 0.0).astype(BF16)
    lr = lax.broadcasted_iota(jnp.int32, (LANE, LANE), 0)
    lc = lax.broadcasted_iota(jnp.int32, (LANE, LANE), 1)
    seg = jnp.where((2 * lr + 1 - LANE) * (2 * lc + 1 - LANE) > 0, 1.0, 0.0).astype(BF16)

    @pl.when(l == 0)
    def _():
        zero = jnp.zeros((hd, hd), F32)
        for bi in range(bb):
            for gi in range(gp):
                s_e = s0_ref[bi, 2 * gi]
                s_o = s0_ref[bi, 2 * gi + 1]
                w_ref[bi * gp + gi] = jnp.concatenate([jnp.concatenate([s_e, zero], axis=1),
                                                       jnp.concatenate([zero, s_o], axis=1)], axis=0)

    def stack(x):
        return jnp.concatenate([jnp.where(head0_c, x, 0.0), jnp.where(head0_c, 0.0, x)], axis=0)

    def unstack(x):
        return x[0:c] + x[c:n2]

    def chunk_pair(sl, bi, gi):
        ls = slice(gi * LANE, (gi + 1) * LANE)
        wi = bi * gp + gi
        r = r_ref[bi, sl, ls]
        lw = lw_ref[bi, sl, ls]
        kr = kr_ref[bi, sl, ls]
        v = v_ref[bi, sl, ls]
        a = a_ref[bi, sl, ls]
        kk = kr * kk_w_ref[:, ls]
        kk_ss = _dot_ones(kk * kk, seg)
        cum = _ones_dot(tri, lw)
        yield
        kk = kk / jnp.maximum(jnp.sqrt(kk_ss), KK_EPS)
        k = kr * (1.0 + (a - 1.0) * ka_w_ref[:, ls])
        cum_end = cum[c - 1:c, :]
        e_neg = jnp.exp(-cum)
        e_rem = jnp.exp(cum_end - cum)
        al = kk * a
        al_t = stack(al * e_neg).astype(BF16)
        k_t = stack(k * e_neg).astype(BF16)
        be_t = stack(-kk * jnp.exp(cum - lw)).astype(BF16)
        r_t = stack(r * jnp.exp(cum)).astype(BF16)
        v_s = stack(v).astype(BF16)
        lhs = jnp.concatenate([be_t, r_t], axis=0)
        sc_a = _dot_nt(lhs, al_t)
        sc_k = _dot_nt(lhs, k_t)
        w = w_ref[wi]
        rd = _dot_nt(lhs, w.astype(BF16))
        yield
        l_a =jnp.where(strict, sc_a[0:n2], 0.0)
        l_k = jnp.where(strict, sc_k[0:n2], 0.0).astype(BF16)
        m_a = jnp.where(incl, sc_a[n2:2 * n2], 0.0).astype(BF16)
        m_k = jnp.where(incl, sc_k[n2:2 * n2], 0.0).astype(BF16)
        t_inv = eye + l_a
        lp = l_a
        rhs_u = rd[0:n2] + _dot(l_k, v_s)
        o_s = rd[n2:2 * n2] + _dot(m_k, v_s)
        bonus = _dot_ones(r * k * rk_w_ref[:, ls], seg) * v
        span = 1
        while 2 * span < c:
            lpb = lp.astype(BF16)
            lp = _dot(lpb, lpb)
            yield
            t_inv = t_inv + _dot(t_inv.astype(BF16), lp.astype(BF16))
            yield
            span *= 2
        u_s = _dot(t_inv.astype(BF16), rhs_u.astype(BF16))
        yield
        u_b = u_s.astype(BF16)
        o_s = o_s + _dot(m_a, u_b)
        uv = jnp.concatenate([u_b, v_s], axis=0)
        ak = jnp.concatenate([stack(al * e_rem), stack(k * e_rem)], axis=0).astype(BF16)
        w_ref[wi] = w * jnp.exp(cum_end) + _dot_tn(uv, ak)
        yield
        o = unstack(o_s)
        mu = _dot_ones(o, seg) * (1.0 / hd)
        yield
        d = o - mu
        var = _dot_ones(d * d, seg) * (1.0 / hd)
        yield
        on = d * lax.rsqrt(var + GN_EPS) * lnw_ref[:, ls] + lnb_ref[:, ls]
        o_ref[bi, sl, ls] = ((on + bonus) * g_ref[bi, sl, ls]).astype(BF16)

    def chunk(ci, _):
        sl = pl.ds(pl.multiple_of(ci * c, c), c)
        _lockstep([chunk_pair(sl, bi, gi) for bi in range(bb) for gi in range(gp)])
        return 0

    lax.fori_loop(0, tl // c, chunk, 0)

    @pl.when(l == pl.num_programs(2) - 1)
    def _():
        for bi in range(bb):
            for gi in range(gp):
                wf = w_ref[bi * gp + gi]
                sn_ref[bi, 2 * gi] = wf[0:hd, 0:hd]
                sn_ref[bi, 2 * gi + 1] = wf[hd:2 * hd, hd:2 * hd]


def _rwkv_chunk(r, lw, kr, vr, a, g, k_k, k_a, r_k, ln_w, ln_b, s0p, *, hd):
    B, L, RW = r.shape
    assert 2 * hd == LANE and RW % LANE == 0
    c = min(RWKV_CHUNK, L)
    tl = _pick_tile(L, 256, c)
    assert L % c == 0 and tl % c == 0 and c % SUBLANE == 0
    npair = RW // LANE
    gp = _pick_tile(npair, 16, 1)
    bb = _pick_tile(B, 4, 1) if L == c else 1
    seq = pl.BlockSpec((bb, tl, gp * LANE), lambda b, p, l: (b, l, p))
    par = pl.BlockSpec((1, gp * LANE), lambda b, p, l: (0, p))
    st = pl.BlockSpec((bb, 2 * gp, hd, hd), lambda b, p, l: (b, p, 0, 0))
    return pl.pallas_call(
        functools.partial(_rwkv_chunk_kernel, c=c, hd=hd, gp=gp),
        grid=(B // bb, npair // gp, L // tl),
        in_specs=[seq] * 6 + [par] * 5 + [st],
        out_specs=[seq, st],
        out_shape=[jax.ShapeDtypeStruct((B, L, RW), BF16),
                   jax.ShapeDtypeStruct((B, 2 * npair, hd, hd), F32)],
        scratch_shapes=[pltpu.VMEM((bb * gp, LANE, LANE), F32)],
        compiler_params=_cparams(("parallel", "parallel", "arbitrary")),
        name="rwkv_chunk",
    )(r, lw, kr, vr, a, g, k_k.reshape(1, RW), k_a.reshape(1, RW), r_k.reshape(1, RW),
      ln_w.reshape(1, RW), ln_b.reshape(1, RW), s0p)


def _up_conv_kernel(x_ref, gn_ref, wg_ref, wv_ref, s_ref, cw_ref, cb_ref, act_ref, nc_ref, h_ref, c_ref,
                    *, rows, tiles_per_seq, seqs):
    i, j = pl.program_id(0), pl.program_id(1)
    tm = x_ref.shape[0]
    tl = tm // seqs

    @pl.when(j == 0)
    def _():
        _rms_rows(x_ref, gn_ref, h_ref, rows)

    h = h_ref[...]
    g = _dot(h, wg_ref[...].astype(BF16))
    v = _dot(h, wv_ref[...].astype(BF16))
    tn = g.shape[1]
    g = g.reshape(seqs, tl, tn)
    first = (i % tiles_per_seq) == 0
    prev = jnp.where(first, s_ref[...], c_ref[j])
    row = lax.broadcasted_iota(jnp.int32, g.shape, 1)
    g1 = jnp.where(row == 0, prev[:, 1:2], pltpu.roll(g, 1, 1))
    g2 = jnp.where(row == 0, prev[:, 0:1], jnp.where(row == 1, prev[:, 1:2], pltpu.roll(g, 2, 1)))
    last2 = g[:, tl - 2:tl, :]
    c_ref[j] = last2
    nc_ref[...] = last2
    conv = cb_ref[...] + g2 * cw_ref[0:1, :] + g1 * cw_ref[1:2, :] + g * cw_ref[2:3, :]
    act_ref[...] = (_gelu_tanh(conv).reshape(tm, tn) * v).astype(BF16)


def _up_conv_act(x, gain, w_up, s_conv, conv_w, conv_b, *, seq_len, tm=1024, tn=256):
    M, K = x.shape
    dff = conv_b.shape[0]
    if seq_len >= tm:
        tm = _pick_tile(seq_len, tm, SUBLANE)
        seqs, tps = 1, seq_len // tm
    else:
        assert seq_len % SUBLANE == 0
        seqs, tps = _pick_tile(M // seq_len, tm // seq_len, 1), 1
        tm = seqs * seq_len
    tn = _pick_tile(dff, tn, LANE)
    nj = dff // tn
    rows = _pick_tile(tm, 64, SUBLANE)
    assert conv_w.shape[0] == 3 and s_conv.shape[1] == 2 and tm // seqs >= 2
    act, tails = pl.pallas_call(
        functools.partial(_up_conv_kernel, rows=rows, tiles_per_seq=tps, seqs=seqs),
        grid=(M // tm, nj),
        in_specs=[pl.BlockSpec((tm, K), lambda i, j: (i, 0), pipeline_mode=ROW_RESIDENT),
                  pl.BlockSpec((1, K), lambda i, j: (0, 0)),
                  pl.BlockSpec((K, tn), lambda i, j: (0, j)),
                  pl.BlockSpec((K, tn), lambda i, j: (0, nj + j)),
                  pl.BlockSpec((seqs, 2, tn), lambda i, j: (i // tps, 0, j)),
                  pl.BlockSpec((3, tn), lambda i, j: (0, j)),
                  pl.BlockSpec((1, tn), lambda i, j: (0, j))],
        out_specs=[pl.BlockSpec((tm, tn), lambda i, j: (i, j)),
                   pl.BlockSpec((seqs, 2, tn), lambda i, j: (i, 0, j))],
        out_shape=[jax.ShapeDtypeStruct((M, dff), BF16),
                   jax.ShapeDtypeStruct((M // tm * seqs, 2, dff), F32)],
        scratch_shapes=[pltpu.VMEM((tm, K), BF16), pltpu.VMEM((nj, seqs, 2, tn), F32)],
        compiler_params=_cparams(("arbitrary", "arbitrary")),
        name="up_conv_act",
    )(x, gain.reshape(1, K), w_up, w_up, s_conv, conv_w, conv_b.reshape(1, dff))
    return act, tails[tps - 1::tps]


def _pad_cols(a, width):
    return a if a.shape[-1] == width else jnp.pad(a, [(0, 0)] * (a.ndim - 1) + [(0, width - a.shape[-1])])


def _pad_rows(a, height):
    return a if a.shape[0] == height else jnp.pad(a, [(0, height - a.shape[0])] + [(0, 0)] * (a.ndim - 1))


def _layout(D, QK, GW, R, RW, WL, AL, GL):
    src, o = {}, 0
    for name, w in (("q", QK), ("k", QK), ("v", GW), ("za", R), ("zg", GW), ("r", RW), ("xw", WL),
                    ("kr", RW), ("vr", RW), ("xa", AL), ("xg", GL), ("ga", D), ("gb", D)):
        src[name] = (o, w)
        o += w
    return src


def _prepare_params(lp):
    (w_in, w_alpha2, b_alpha, gla_norm, w_branch_a,
     mu_shift, w0, w_decay2, a0, w_iclr2, w_gate2, k_k, k_a, r_k, ln_x_w, ln_x_b, w_branch_b,
     w_out, g_pre_mix, g_post_mix, g_pre_ffn, g_post_ffn,
     w_up, conv_w, conv_b, w_down, g_pe, w_pe_gate, w_pe) = lp
    D = w_in.shape[0]
    R, QK = w_alpha2.shape
    GW = w_branch_a.shape[0]
    RW = w_branch_b.shape[0]
    WL, AL, GL = w_decay2.shape[0], w_iclr2.shape[0], w_gate2.shape[0]
    src = _layout(D, QK, GW, R, RW, WL, AL, GL)
    pad = lambda w: -(-w // LANE) * LANE
    prm = dict(
        dims=dict(D=D, R=R, QK=QK, GW=GW, RW=RW, WL=WL, AL=AL, GL=GL, DV=gla_norm.shape[0],
                  HD=r_k.shape[1], DFF=conv_b.shape[0]),
        src=src, off={n: src[n][0] for n in src},
        w_in_t=w_in.T,
        w_alpha2=_pad_rows(w_alpha2, LANE).astype(BF16), b_alpha=b_alpha, gla_norm=gla_norm,
        w_branch_a=w_branch_a.astype(BF16), w_branch_b=w_branch_b.astype(BF16),
        w0=w0, w_decay2=_pad_rows(w_decay2, LANE).astype(BF16), a0=a0,
        w_iclr2=_pad_rows(w_iclr2, LANE).astype(BF16), w_gate2=w_gate2.astype(BF16),
        k_k=k_k, k_a=k_a, r_k=r_k.reshape(-1), ln_x_w=ln_x_w, ln_x_b=ln_x_b,
        w_out=w_out.astype(BF16), g_pre_mix=g_pre_mix, g_post_mix=g_post_mix,
        g_pre_ffn=g_pre_ffn, g_post_ffn=g_post_ffn,
        w_up=w_up, conv_w=conv_w, conv_b=conv_b, w_down=w_down.astype(BF16),
        g_pe=g_pe, w_pe_gate=w_pe_gate, w_pe=w_pe.astype(BF16),
    )
    rsrc = src["r"][0]
    rnames = ("r", "xw", "kr", "vr", "xa", "xg")
    prm["rnames"] = rnames
    prm["rsl"] = {n: (src[n][0] - rsrc, src[n][1]) for n in rnames}
    prm["mu"] = {n: _pad_cols(mu_shift[None, prm["rsl"][n][0]:prm["rsl"][n][0] + prm["rsl"][n][1]],
                              pad(prm["rsl"][n][1])) for n in rnames}
    return prm


def _layer(x, p, s_gla, s_rwkv, s_shift, s_conv, prm):
    B, L, D = x.shape
    M = B * L
    dm, dst = prm["dims"], prm["off"]
    GW, RW, DV, HD, DFF, QK = dm["GW"], dm["RW"], dm["DV"], dm["HD"], dm["DFF"], dm["QK"]
    heads = GW // DV
    pad = lambda w: -(-w // LANE) * LANE
    x2 = x.reshape(M, D)

    proj = _norm_matmul_nt(x2, prm["g_pre_mix"], prm["w_in_t"])
    proj3 = proj.reshape(B, L, -1)

    o_a, s_gla_new = _gla(proj3, dst, prm["w_alpha2"], prm["b_alpha"], prm["gla_norm"], s_gla,
                          dk=QK // heads, dv=DV, heads=heads)

    shift_parts = {n: _pad_cols(s_shift[:, None, prm["rsl"][n][0]:prm["rsl"][n][0] + prm["rsl"][n][1]],
                                pad(prm["rsl"][n][1])) for n in prm["rnames"]}
    r, lw, kr, vr, a, g = _rwkv_prep(proj3, dst, shift_parts, prm["mu"], prm["w0"], prm["w_decay2"],
                                     prm["a0"], prm["w_iclr2"], prm["w_gate2"], rw=RW, gl=dm["GL"])
    o_b, s_rwkv_new = _rwkv_chunk(r, lw, kr, vr, a, g, prm["k_k"], prm["k_a"], prm["r_k"],
                                  prm["ln_x_w"], prm["ln_x_b"], s_rwkv, hd=HD)
    new_shift = proj3[:, L - 1, dst["r"]:dst["ga"]]

    mixed = _merge(o_a.reshape(M, GW), o_b.reshape(M, RW), prm["w_branch_a"], prm["w_branch_b"],
                   proj, dst["ga"], dst["gb"])
    x2 = _matmul_norm_residual(mixed, prm["w_out"], x2, prm["g_post_mix"], tm=512, tn=512,
                               single_buffer_rows=False)

    act, new_conv = _up_conv_act(x2, prm["g_pre_ffn"], prm["w_up"], s_conv, prm["conv_w"],
                                 prm["conv_b"], seq_len=L)
    x2 = _matmul_norm_residual(act.reshape(M, DFF), prm["w_down"], x2, prm["g_post_ffn"], tm=512, tn=256,
                               single_buffer_rows=True)

    x2 = _pe_layer(x2, prm["g_pe"], prm["w_pe_gate"], p.reshape(M, -1), prm["w_pe"])
    return x2.reshape(B, L, D), s_gla_new, s_rwkv_new, new_shift, new_conv


def kernel(x_prompt, x_sample, state_gla, state_rwkv, state_shift, state_ffn_conv, p_prompt, p_sample, w_in, w_alpha2, b_alpha, gla_norm, w_branch_a, mu_shift, w0, w_decay2, a0, w_iclr2, w_gate2, k_k, k_a, r_k, ln_x_w, ln_x_b, w_branch_b, w_out, g_pre_mix, g_post_mix, g_pre_ffn, g_post_ffn, w_up, conv_w, conv_b, w_down, g_pe, w_pe_gate, w_pe):
    params = (w_in, w_alpha2, b_alpha, gla_norm, w_branch_a,
              mu_shift, w0, w_decay2, a0, w_iclr2, w_gate2, k_k, k_a, r_k, ln_x_w, ln_x_b, w_branch_b,
              w_out, g_pre_mix, g_post_mix, g_pre_ffn, g_post_ffn,
              w_up, conv_w, conv_b, w_down, g_pe, w_pe_gate, w_pe)
    depth = w_in.shape[0]
    nb = x_prompt.shape[0]
    yp, ys = x_prompt, x_sample
    outs_p = [[], [], [], []]
    outs_s = [[], [], [], []]
    for i in range(depth):
        prm = _prepare_params(tuple(t[i] for t in params))
        z_gla = jnp.zeros((nb,) + state_gla.shape[2:], F32)
        z_rwkv = jnp.zeros((nb,) + state_rwkv.shape[2:], F32)
        z_shift = jnp.zeros((nb,) + state_shift.shape[2:], x_prompt.dtype)
        z_conv = jnp.zeros((nb,) + state_ffn_conv.shape[2:], x_prompt.dtype)
        yp, *st = _layer(yp, p_prompt[i], z_gla, z_rwkv, z_shift, z_conv, prm)
        for acc, s in zip(outs_p, st):
            acc.append(s)
        ys, *st = _layer(ys, p_sample[i], state_gla[i], state_rwkv[i], state_shift[i],
                         state_ffn_conv[i], prm)
        for acc, s in zip(outs_s, st):
            acc.append(s)
    return (yp, ys, *(jnp.stack(a) for a in outs_p), *(jnp.stack(a) for a in outs_s))
```

```python
import functools
import math

import jax
import jax.numpy as jnp
from jax import lax
from jax.experimental import pallas as pl
from jax.experimental.pallas import tpu as pltpu

F32 = jnp.float32
BF16 = jnp.bfloat16

LANE = 128
SUBLANE = 8
VMEM_LIMIT_BYTES = 56 * 1024 * 1024

EPS = 1e-6
GN_EPS = 64e-5
GLA_TAU = 16.0
GLA_CHUNK = 64
RWKV_CHUNK = 64
KK_EPS = 1e-12
DECAY_SCALE = math.exp(-0.5)
GELU_C = math.sqrt(2.0 / math.pi)


ROW_RESIDENT = pl.Buffered(1)


def _cparams(sem):
    return pltpu.CompilerParams(dimension_semantics=sem, vmem_limit_bytes=VMEM_LIMIT_BYTES)


def _sigmoid(x):
    return 1.0 / (1.0 + jnp.exp(-x))


def _log_sigmoid(x):
    return jnp.minimum(x, 0.0) - jnp.log(1.0 + jnp.exp(-jnp.abs(x)))


def _gelu_tanh(x):
    return 0.5 * x * (1.0 + jnp.tanh(GELU_C * (x + 0.044715 * (x * x * x))))


def _split2(x):
    hi = x.astype(BF16)
    lo = (x - hi.astype(F32)).astype(BF16)
    return hi, lo


def _dot(a, b):
    return jnp.dot(a, b, preferred_element_type=F32)


def _dot_nt(a, b):
    return lax.dot_general(a, b, (((1,), (1,)), ((), ())), preferred_element_type=F32)


def _dot_tn(a, b):
    return lax.dot_general(a, b, (((0,), (0,)), ((), ())), preferred_element_type=F32)


def _ones_dot(m01, x):
    hi, lo = _split2(x)
    return _dot(m01, hi) + _dot(m01, lo)


def _dot_ones(x, m01):
    return _dot(x.astype(BF16), m01)


def _lockstep(gens):
    gens = list(gens)
    while gens:
        alive = []
        for g in gens:
            try:
                next(g)
                alive.append(g)
            except StopIteration:
                pass
        gens = alive


def _tri_incl(n):
    r = lax.broadcasted_iota(jnp.int32, (n, n), 0)
    c = lax.broadcasted_iota(jnp.int32, (n, n), 1)
    return r >= c


def _pick_tile(n, target, mult):
    if n <= target:
        return n
    best = None
    t = mult
    while t <= target:
        if n % t == 0:
            best = t
        t += mult
    assert best is not None, (n, target, mult)
    return best


def _rms_rows(x_ref, g_ref, h_ref, rows):
    tm = x_ref.shape[0]

    def body(i, _):
        sl = pl.ds(pl.multiple_of(i * rows, rows), rows)
        x = x_ref[sl, :]
        ms = jnp.mean(x * x, axis=-1, keepdims=True)
        h_ref[sl, :] = (x * lax.rsqrt(ms + EPS) * g_ref[...]).astype(BF16)
        return 0

    lax.fori_loop(0, tm // rows, body, 0)


def _norm_mm_kernel(x_ref, g_ref, wt_ref, o_ref, h_ref, *, rows, n_valid):
    j = pl.program_id(1)

    @pl.when(j == 0)
    def _():
        _rms_rows(x_ref, g_ref, h_ref, rows)

    y = _dot_nt(h_ref[...], wt_ref[...].astype(BF16))
    tn = y.shape[1]
    if n_valid % tn:
        col = lax.broadcasted_iota(jnp.int32, y.shape, 1) + j * tn
        y = jnp.where(col < n_valid, y, 0.0)
    o_ref[...] = y


def _window(src, width):
    base = src // LANE * LANE
    shift = src - base
    wp = -(-width // LANE) * LANE
    return base, shift, (wp if shift == 0 else wp + LANE), wp


def _norm_matmul_nt(x, gain, wt, *, tm=1024, tn=512):
    M, K = x.shape
    nw = wt.shape[0]
    tm = _pick_tile(M, tm, SUBLANE)
    tn = min(tn, -(-nw // LANE) * LANE)
    N = -(-nw // tn) * tn
    rows = _pick_tile(tm, 64, SUBLANE)
    return pl.pallas_call(
        functools.partial(_norm_mm_kernel, rows=rows, n_valid=nw),
        grid=(M // tm, N // tn),
        in_specs=[pl.BlockSpec((tm, K), lambda i, j: (i, 0), pipeline_mode=ROW_RESIDENT),
                  pl.BlockSpec((1, K), lambda i, j: (0, 0)),
                  pl.BlockSpec((tn, K), lambda i, j: (j, 0))],
        out_specs=pl.BlockSpec((tm, tn), lambda i, j: (i, j)),
        out_shape=jax.ShapeDtypeStruct((M, N), F32),
        scratch_shapes=[pltpu.VMEM((tm, K), BF16)],
        compiler_params=_cparams(("parallel", "arbitrary")),
        name="norm_matmul",
    )(x, gain.reshape(1, K), wt)


def _pe_kernel(x_ref, g_ref, w_ref, p_ref, wp_ref, o_ref, h_ref, *, rows, tn):
    j = pl.program_id(1)

    @pl.when(j == 0)
    def _():
        _rms_rows(x_ref, g_ref, h_ref, rows)

    gate = _sigmoid(_dot(h_ref[...], w_ref[...].astype(BF16)))
    pe = _dot(p_ref[...].astype(BF16), wp_ref[...])
    xs = x_ref[:, pl.ds(pl.multiple_of(j * tn, tn), tn)]
    o_ref[...] = xs + gate * pe


def _pe_layer(x, gain, w_gate, p, w_pe, *, tm=1024, tn=512):
    M, K = x.shape
    N = w_gate.shape[1]
    P = p.shape[1]
    tm = _pick_tile(M, tm, SUBLANE)
    tn = _pick_tile(N, tn, LANE)
    rows = _pick_tile(tm, 64, SUBLANE)
    return pl.pallas_call(
        functools.partial(_pe_kernel, rows=rows, tn=tn),
        grid=(M // tm, N // tn),
        in_specs=[pl.BlockSpec((tm, K), lambda i, j: (i, 0), pipeline_mode=ROW_RESIDENT),
                  pl.BlockSpec((1, K), lambda i, j: (0, 0)),
                  pl.BlockSpec((K, tn), lambda i, j: (0, j)),
                  pl.BlockSpec((tm, P), lambda i, j: (i, 0)),
                  pl.BlockSpec((P, tn), lambda i, j: (0, j))],
        out_specs=pl.BlockSpec((tm, tn), lambda i, j: (i, j)),
        out_shape=jax.ShapeDtypeStruct((M, N), F32),
        scratch_shapes=[pltpu.VMEM((tm, K), BF16)],
        compiler_params=_cparams(("parallel", "arbitrary")),
        name="pe_layer",
    )(x, gain.reshape(1, K), w_gate, p, w_pe)


def _mm_norm_res_kernel(a_ref, w_ref, x_ref, g_ref, o_ref, *, tn, rows):
    j = pl.program_id(1)
    o_ref[:, pl.ds(pl.multiple_of(j * tn, tn), tn)] = _dot(a_ref[...], w_ref[...])

    @pl.when(j == pl.num_programs(1) - 1)
    def _():
        tm = o_ref.shape[0]

        def body(i, _):
            sl = pl.ds(pl.multiple_of(i * rows, rows), rows)
            y = o_ref[sl, :]
            ms = jnp.mean(y * y, axis=-1, keepdims=True)
            o_ref[sl, :] = x_ref[sl, :] + y * lax.rsqrt(ms + EPS) * g_ref[...]
            return 0

        lax.fori_loop(0, tm // rows, body, 0)


def _matmul_norm_residual(a, w, x, gain, *, tm, tn, single_buffer_rows):
    M, K = a.shape
    N = w.shape[1]
    tm = _pick_tile(M, tm, SUBLANE)
    tn = _pick_tile(N, tn, LANE)
    rows = _pick_tile(tm, 64, SUBLANE)
    mode = dict(pipeline_mode=ROW_RESIDENT) if single_buffer_rows else {}
    return pl.pallas_call(
        functools.partial(_mm_norm_res_kernel, tn=tn, rows=rows),
        grid=(M // tm, N // tn),
        in_specs=[pl.BlockSpec((tm, K), lambda i, j: (i, 0), **mode),
                  pl.BlockSpec((K, tn), lambda i, j: (0, j)),
                  pl.BlockSpec((tm, N), lambda i, j: (i, 0), **mode),
                  pl.BlockSpec((1, N), lambda i, j: (0, 0))],
        out_specs=pl.BlockSpec((tm, N), lambda i, j: (i, 0)),
        out_shape=jax.ShapeDtypeStruct((M, N), F32),
        compiler_params=_cparams(("parallel", "arbitrary")),
        name="matmul_norm_residual",
    )(a, w, x, gain.reshape(1, N))


def _merge_kernel(oa_ref, ob_ref, wa_ref, wb_ref, ga_ref, gb_ref, o_ref, *, sa, sb):
    tn = o_ref.shape[1]
    ya = _dot(oa_ref[...], wa_ref[...])
    yb = _dot(ob_ref[...], wb_ref[...])
    ga = ga_ref[:, sa:sa + tn]
    gb = gb_ref[:, sb:sb + tn]
    o_ref[...] = (_sigmoid(ga) * ya + _sigmoid(gb) * yb).astype(BF16)


def _gate_window_spec(tm, tn, off, n_cols):
    base, shift, _, _ = _window(off, tn)
    ww = tn if shift == 0 else tn + LANE
    spec = pl.BlockSpec((pl.Element(tm), pl.Element(ww)),
                        lambda i, j: (pl.multiple_of(i * tm, SUBLANE), pl.multiple_of(base + j * tn, LANE)))
    return spec, shift, base + n_cols - tn + ww


def _merge(oa, ob, wa, wb, proj, ga_off, gb_off, *, tm=512, tn=1024):
    M, KA = oa.shape
    KB = ob.shape[1]
    N = wa.shape[1]
    tm = _pick_tile(M, tm, SUBLANE)
    tn = _pick_tile(N, tn, LANE)
    ga_spec, sa, enda = _gate_window_spec(tm, tn, ga_off, N)
    gb_spec, sb, endb = _gate_window_spec(tm, tn, gb_off, N)
    assert max(enda, endb) <= proj.shape[1]
    return pl.pallas_call(
        functools.partial(_merge_kernel, sa=sa, sb=sb),
        grid=(M // tm, N // tn),
        in_specs=[pl.BlockSpec((tm, KA), lambda i, j: (i, 0)),
                  pl.BlockSpec((tm, KB), lambda i, j: (i, 0)),
                  pl.BlockSpec((KA, tn), lambda i, j: (0, j)),
                  pl.BlockSpec((KB, tn), lambda i, j: (0, j)),
                  ga_spec, gb_spec],
        out_specs=pl.BlockSpec((tm, tn), lambda i, j: (i, j)),
        out_shape=jax.ShapeDtypeStruct((M, N), BF16),
        compiler_params=_cparams(("parallel", "arbitrary")),
        name="merge",
    )(oa, ob, wa, wb, proj, proj)


def _gla_kernel(q_ref, k_ref, v_ref, za_ref, zg_ref, wa_ref, ba_ref, gn_ref, s0_ref,
                o_ref, sn_ref, s_ref, *, scale, c, heads, zshift, single_chunk):
    l = pl.program_id(1)
    bb = q_ref.shape[0]
    dk = q_ref.shape[2] // heads
    dv = v_ref.shape[2] // heads

    if not single_chunk:
        @pl.when(l == 0)
        def _():
            s_ref[...] = s0_ref[...].reshape(bb * heads, dk, dv)

    x = _dot(za_ref[...].reshape(bb * c, LANE).astype(BF16), wa_ref[...]) + ba_ref[...]
    la_seqs = _log_sigmoid(x) * (1.0 / GLA_TAU)
    tri = _tri_incl(c)
    tri01 = jnp.where(tri, 1.0, 0.0).astype(BF16)
    ones = jnp.ones((c, LANE), BF16)
    la_seq = [la_seqs[bi * c:(bi + 1) * c] for bi in range(bb)]
    b_seq = [_ones_dot(tri01, la) for la in la_seq]

    def head(bi, h):
        ks = slice(h * dk, (h + 1) * dk)
        vs = slice(h * dv, (h + 1) * dv)
        la = la_seq[bi][:, ks]
        b = b_seq[bi][:, ks]
        b_end = b[c - 1:c, :]
        q = q_ref[bi, :, ks] * scale
        k = k_ref[bi, :, ks]
        v = v_ref[bi, :, vs].astype(BF16)
        qd = (q * jnp.exp(b)).astype(BF16)
        kd = (k * jnp.exp(-b)).astype(BF16)
        s = s0_ref[bi, h] if single_chunk else s_ref[bi * heads + h]
        att = _dot_nt(qd, kd)
        o = _dot(qd, s.astype(BF16))
        k_end = (k * jnp.exp(b_end - b)).astype(BF16)
        la_hi, la_lo = _split2(la)
        dec = jnp.exp(_dot_tn(la_hi, ones) + _dot_tn(la_lo, ones))
        dec = jnp.concatenate([dec] * (dv // LANE), axis=1)
        s_new = s * dec + _dot_tn(k_end, v)
        if single_chunk:
            sn_ref[bi, h] = s_new
        else:
            s_ref[bi * heads + h] = s_new
        yield
        o = o + _dot(jnp.where(tri, att, 0.0).astype(BF16), v)
        yield
        ms = jnp.mean(o * o, axis=-1, keepdims=True)
        on = o * lax.rsqrt(ms + EPS) * gn_ref[...]
        zg = zg_ref[bi, :, zshift + h * dv:zshift + (h + 1) * dv]
        o_ref[bi, :, vs] = (on * (zg * _sigmoid(zg))).astype(BF16)

    _lockstep([head(bi, h) for bi in range(bb) for h in range(heads)])

    if not single_chunk:
        @pl.when(l == pl.num_programs(1) - 1)
        def _():
            sn_ref[...] = s_ref[...].reshape(bb, heads, dk, dv)


def _gla(proj3, offs, w_alpha2p, b_alpha, gla_norm, s0, *, dk, dv, heads):
    B, L, _ = proj3.shape
    c = min(GLA_CHUNK, L)
    qk, gw = heads * dk, heads * dv
    assert L % c == 0 and dv % LANE == 0
    for name, wdt in (("q", qk), ("k", qk), ("v", gw), ("za", LANE)):
        assert offs[name] % wdt == 0
    jq, jk, jv, jz = offs["q"] // qk, offs["k"] // qk, offs["v"] // gw, offs["za"] // LANE
    zbase, zshift, zww, _ = _window(offs["zg"], gw)
    assert zbase + zww <= proj3.shape[2]
    single_chunk = L == c
    bb = _pick_tile(B, 4 if single_chunk else 2, 1)
    return pl.pallas_call(
        functools.partial(_gla_kernel, scale=dk ** -0.5, c=c, heads=heads, zshift=zshift,
                          single_chunk=single_chunk),
        grid=(B // bb, L // c),
        in_specs=[pl.BlockSpec((bb, c, qk), lambda b, l: (b, l, jq)),
                  pl.BlockSpec((bb, c, qk), lambda b, l: (b, l, jk)),
                  pl.BlockSpec((bb, c, gw), lambda b, l: (b, l, jv)),
                  pl.BlockSpec((bb, c, LANE), lambda b, l: (b, l, jz)),
                  pl.BlockSpec((pl.Element(bb), pl.Element(c), pl.Element(zww)),
                               lambda b, l: (b * bb, pl.multiple_of(l * c, c), zbase)),
                  pl.BlockSpec((LANE, qk), lambda b, l: (0, 0)),
                  pl.BlockSpec((1, qk), lambda b, l: (0, 0)),
                  pl.BlockSpec((1, dv), lambda b, l: (0, 0)),
                  pl.BlockSpec((bb, heads, dk, dv), lambda b, l: (b, 0, 0, 0))],
        out_specs=[pl.BlockSpec((bb, c, gw), lambda b, l: (b, l, 0)),
                   pl.BlockSpec((bb, heads, dk, dv), lambda b, l: (b, 0, 0, 0))],
        out_shape=[jax.ShapeDtypeStruct((B, L, gw), BF16),
                   jax.ShapeDtypeStruct((B, heads, dk, dv), F32)],
        scratch_shapes=[pltpu.VMEM((SUBLANE, LANE) if single_chunk else (bb * heads, dk, dv), F32)],
        compiler_params=_cparams(("parallel", "arbitrary")),
        name="gla",
    )(proj3, proj3, proj3, proj3, proj3, w_alpha2p, b_alpha.reshape(1, -1), gla_norm.reshape(1, dv), s0)


def _rwkv_prep_kernel(r_ref, xw_ref, kr_ref, vr_ref, xa_ref, xg_ref,
                      sr_ref, sxw_ref, skr_ref, svr_ref, sxa_ref, sxg_ref,
                      mr_ref, mxw_ref, mkr_ref, mvr_ref, mxa_ref, mxg_ref,
                      w0_ref, wd_ref, a0_ref, wi_ref, wg_ref,
                      ro_ref, lw_ref, ko_ref, vo_ref, ao_ref, go_ref,
                      cr_ref, cxw_ref, ckr_ref, cvr_ref, cxa_ref, cxg_ref, *, shifts):
    l = pl.program_id(1)
    bb, tl = r_ref.shape[0], r_ref.shape[1]
    sh_r, sh_xw, sh_kr, sh_vr, sh_xa, sh_xg = shifts

    def shift_mix(z_ref, s0, s_ref, m_ref, c_ref):
        cur = z_ref[:, :, s0:s0 + m_ref.shape[1]]
        first = jnp.where(l == 0, s_ref[...], c_ref[...])
        row = lax.broadcasted_iota(jnp.int32, cur.shape, 1)
        prev = jnp.where(row == 0, first, pltpu.roll(cur, 1, 1))
        c_ref[...] = cur[:, tl - 1:tl, :]
        return cur + (prev - cur) * m_ref[...]

    def flat(x):
        return x.reshape(bb * tl, x.shape[2])

    ro_ref[...] = shift_mix(r_ref, sh_r, sr_ref, mr_ref, cr_ref)
    ko_ref[...] = shift_mix(kr_ref, sh_kr, skr_ref, mkr_ref, ckr_ref)
    vo_ref[...] = shift_mix(vr_ref, sh_vr, svr_ref, mvr_ref, cvr_ref)
    xw = flat(shift_mix(xw_ref, sh_xw, sxw_ref, mxw_ref, cxw_ref))
    xa = flat(shift_mix(xa_ref, sh_xa, sxa_ref, mxa_ref, cxa_ref))
    xg = flat(shift_mix(xg_ref, sh_xg, sxg_ref, mxg_ref, cxg_ref))
    z = w0_ref[...] + _dot(jnp.tanh(xw).astype(BF16), wd_ref[...])
    rw = z.shape[1]
    lw_ref[...] = (-DECAY_SCALE * _sigmoid(z)).reshape(bb, tl, rw)
    ao_ref[...] = _sigmoid(a0_ref[...] + _dot(xa.astype(BF16), wi_ref[...])).reshape(bb, tl, rw)
    go_ref[...] = _dot(_sigmoid(xg).astype(BF16), wg_ref[...]).reshape(bb, tl, rw)


def _rwkv_prep(proj3, offs, shift_parts, mu_parts, w0, w_decay2p, a0, w_iclr2p, w_gate2, *, rw, gl):
    B, L, _ = proj3.shape
    tl = _pick_tile(L, 256, SUBLANE)
    bb = _pick_tile(B, max(1, 256 // L), 1) if tl == L else 1
    names = ("r", "xw", "kr", "vr", "xa", "xg")
    widths = {"r": rw, "xw": LANE, "kr": rw, "vr": rw, "xa": LANE, "xg": gl}
    in_specs, args, shifts = [], [], []
    for n in names:
        base, shift, ww, wp = _window(offs[n], widths[n])
        assert wp == widths[n] and base + ww <= proj3.shape[2]
        shifts.append(shift)
        in_specs.append(pl.BlockSpec(
            (pl.Element(bb), pl.Element(tl), pl.Element(ww)),
            functools.partial(lambda b, l, base: (b * bb, pl.multiple_of(l * tl, SUBLANE), base), base=base)))
        args.append(proj3)
    for n in names:
        in_specs.append(pl.BlockSpec((bb, 1, widths[n]), lambda b, l: (b, 0, 0)))
        args.append(shift_parts[n])
    for n in names:
        in_specs.append(pl.BlockSpec((1, widths[n]), lambda b, l: (0, 0)))
        args.append(mu_parts[n])
    for arr in (w0.reshape(1, rw), w_decay2p, a0.reshape(1, rw), w_iclr2p, w_gate2):
        in_specs.append(pl.BlockSpec(arr.shape, lambda b, l: (0, 0)))
        args.append(arr)
    out_spec = pl.BlockSpec((bb, tl, rw), lambda b, l: (b, l, 0))
    out_sd = jax.ShapeDtypeStruct((B, L, rw), F32)
    return pl.pallas_call(
        functools.partial(_rwkv_prep_kernel, shifts=tuple(shifts)),
        grid=(B // bb, L // tl),
        in_specs=in_specs,
        out_specs=[out_spec] * 6,
        out_shape=[out_sd] * 6,
        scratch_shapes=[pltpu.VMEM((bb, 1, widths[n]), F32) for n in names],
        compiler_params=_cparams(("parallel", "arbitrary")),
        name="rwkv_prep",
    )(*args)


def _rwkv_chunk_kernel(r_ref, lw_ref, kr_ref, v_ref, a_ref, g_ref,
                       kk_w_ref, ka_w_ref, rk_w_ref, lnw_ref, lnb_ref, s0_ref,
                       o_ref, sn_ref, w_ref, *, c, hd, gp):
    l = pl.program_id(2)
    bb, tl = r_ref.shape[0], r_ref.shape[1]
    n2 = 2 * c
    lane_c = lax.broadcasted_iota(jnp.int32, (c, LANE), 1)
    head0_c = lane_c < hd
    row2 = lax.broadcasted_iota(jnp.int32, (n2, n2), 0)
    col2 = lax.broadcasted_iota(jnp.int32, (n2, n2), 1)
    same = (2 * row2 + 1 - n2) * (2 * col2 + 1 - n2) > 0
    strict = jnp.logical_and(same, row2 > col2)
    incl = jnp.logical_and(same, row2 >= col2)
    eye = jnp.where(row2 == col2, 1.0, 0.0)
    tri = jnp.where(_tri_incl(c), 1.0, 0.0).astype(BF16)
    lr = lax.broadcasted_iota(jnp.int32, (LANE, LANE), 0)
    lc = lax.broadcasted_iota(jnp.int32, (LANE, LANE), 1)
    seg = jnp.where((2 * lr + 1 - LANE) * (2 * lc + 1 - LANE) > 0, 1.0, 0.0).astype(BF16)

    @pl.when(l == 0)
    def _():
        zero = jnp.zeros((hd, hd), F32)
        for bi in range(bb):
            for gi in range(gp):
                s_e = s0_ref[bi, 2 * gi]
                s_o = s0_ref[bi, 2 * gi + 1]
                w_ref[bi * gp + gi] = jnp.concatenate([jnp.concatenate([s_e, zero], axis=1),
                                                       jnp.concatenate([zero, s_o], axis=1)], axis=0)

    def stack(x):
        return jnp.concatenate([jnp.where(head0_c, x, 0.0), jnp.where(head0_c, 0.0, x)], axis=0)

    def unstack(x):
        return x[0:c] + x[c:n2]

    def chunk_pair(sl, bi, gi):
        ls = slice(gi * LANE, (gi + 1) * LANE)
        wi = bi * gp + gi
        r = r_ref[bi, sl, ls]
        lw = lw_ref[bi, sl, ls]
        kr = kr_ref[bi, sl, ls]
        v = v_ref[bi, sl, ls]
        a = a_ref[bi, sl, ls]
        kk = kr * kk_w_ref[:, ls]
        kk_ss = _dot_ones(kk * kk, seg)
        cum = _ones_dot(tri, lw)
        yield
        kk = kk / jnp.maximum(jnp.sqrt(kk_ss), KK_EPS)
        k = kr * (1.0 + (a - 1.0) * ka_w_ref[:, ls])
        cum_end = cum[c - 1:c, :]
        e_neg = jnp.exp(-cum)
        e_rem = jnp.exp(cum_end - cum)
        al = kk * a
        al_t = stack(al * e_neg).astype(BF16)
        k_t = stack(k * e_neg).astype(BF16)
        be_t = stack(-kk * jnp.exp(cum - lw)).astype(BF16)
        r_t = stack(r * jnp.exp(cum)).astype(BF16)
        v_s = stack(v).astype(BF16)
        lhs = jnp.concatenate([be_t, r_t], axis=0)
        sc_a = _dot_nt(lhs, al_t)
        sc_k = _dot_nt(lhs, k_t)
        w = w_ref[wi]
        rd = _dot_nt(lhs, w.astype(BF16))
        yield
        l_a =jnp.where(strict, sc_a[0:n2], 0.0)
        l_k = jnp.where(strict, sc_k[0:n2], 0.0).astype(BF16)
        m_a = jnp.where(incl, sc_a[n2:2 * n2], 0.0).astype(BF16)
        m_k = jnp.where(incl, sc_k[n2:2 * n2], 0.0).astype(BF16)
        t_inv = eye + l_a
        lp = l_a
        rhs_u = rd[0:n2] + _dot(l_k, v_s)
        o_s = rd[n2:2 * n2] + _dot(m_k, v_s)
        bonus = _dot_ones(r * k * rk_w_ref[:, ls], seg) * v
        span = 1
        while 2 * span < c:
            lpb = lp.astype(BF16)
            lp = _dot(lpb, lpb)
            yield
            t_inv = t_inv + _dot(t_inv.astype(BF16), lp.astype(BF16))
            yield
            span *= 2
        u_s = _dot(t_inv.astype(BF16), rhs_u.astype(BF16))
        yield
        u_b = u_s.astype(BF16)
        o_s = o_s + _dot(m_a, u_b)
        uv = jnp.concatenate([u_b, v_s], axis=0)
        ak = jnp.concatenate([stack(al * e_rem), stack(k * e_rem)], axis=0).astype(BF16)
        w_ref[wi] = w * jnp.exp(cum_end) + _dot_tn(uv, ak)
        yield
        o = unstack(o_s)
        mu = _dot_ones(o, seg) * (1.0 / hd)
        yield
        d = o - mu
        var = _dot_ones(d * d, seg) * (1.0 / hd)
        yield
        on = d * lax.rsqrt(var + GN_EPS) * lnw_ref[:, ls] + lnb_ref[:, ls]
        o_ref[bi, sl, ls] = ((on + bonus) * g_ref[bi, sl, ls]).astype(BF16)

    def chunk(ci, _):
        sl = pl.ds(pl.multiple_of(ci * c, c), c)
        _lockstep([chunk_pair(sl, bi, gi) for bi in range(bb) for gi in range(gp)])
        return 0

    lax.fori_loop(0, tl // c, chunk, 0)

    @pl.when(l == pl.num_programs(2) - 1)
    def _():
        for bi in range(bb):
            for gi in range(gp):
                wf = w_ref[bi * gp + gi]
                sn_ref[bi, 2 * gi] = wf[0:hd, 0:hd]
                sn_ref[bi, 2 * gi + 1] = wf[hd:2 * hd, hd:2 * hd]


def _rwkv_chunk(r, lw, kr, vr, a, g, k_k, k_a, r_k, ln_w, ln_b, s0p, *, hd):
    B, L, RW = r.shape
    assert 2 * hd == LANE and RW % LANE == 0
    c = min(RWKV_CHUNK, L)
    tl = _pick_tile(L, 256, c)
    assert L % c == 0 and tl % c == 0 and c % SUBLANE == 0
    npair = RW // LANE
    gp = _pick_tile(npair, 16, 1)
    bb = _pick_tile(B, 4, 1) if L == c else 1
    seq = pl.BlockSpec((bb, tl, gp * LANE), lambda b, p, l: (b, l, p))
    par = pl.BlockSpec((1, gp * LANE), lambda b, p, l: (0, p))
    st = pl.BlockSpec((bb, 2 * gp, hd, hd), lambda b, p, l: (b, p, 0, 0))
    return pl.pallas_call(
        functools.partial(_rwkv_chunk_kernel, c=c, hd=hd, gp=gp),
        grid=(B // bb, npair // gp, L // tl),
        in_specs=[seq] * 6 + [par] * 5 + [st],
        out_specs=[seq, st],
        out_shape=[jax.ShapeDtypeStruct((B, L, RW), BF16),
                   jax.ShapeDtypeStruct((B, 2 * npair, hd, hd), F32)],
        scratch_shapes=[pltpu.VMEM((bb * gp, LANE, LANE), F32)],
        compiler_params=_cparams(("parallel", "parallel", "arbitrary")),
        name="rwkv_chunk",
    )(r, lw, kr, vr, a, g, k_k.reshape(1, RW), k_a.reshape(1, RW), r_k.reshape(1, RW),
      ln_w.reshape(1, RW), ln_b.reshape(1, RW), s0p)


def _up_conv_kernel(x_ref, gn_ref, wg_ref, wv_ref, s_ref, cw_ref, cb_ref, act_ref, nc_ref, h_ref, c_ref,
                    *, rows, tiles_per_seq, seqs):
    i, j = pl.program_id(0), pl.program_id(1)
    tm = x_ref.shape[0]
    tl = tm // seqs

    @pl.when(j == 0)
    def _():
        _rms_rows(x_ref, gn_ref, h_ref, rows)

    h = h_ref[...]
    g = _dot(h, wg_ref[...].astype(BF16))
    v = _dot(h, wv_ref[...].astype(BF16))
    tn = g.shape[1]
    g = g.reshape(seqs, tl, tn)
    first = (i % tiles_per_seq) == 0
    ci = j if tiles_per_seq > 1 else 0
    prev = jnp.where(first, s_ref[...], c_ref[ci])
    row = lax.broadcasted_iota(jnp.int32, g.shape, 1)
    g1 = jnp.where(row == 0, prev[:, 1:2], pltpu.roll(g, 1, 1))
    g2 = jnp.where(row == 0, prev[:, 0:1], jnp.where(row == 1, prev[:, 1:2], pltpu.roll(g, 2, 1)))
    last2 = g[:, tl - 2:tl, :]
    c_ref[ci] = last2
    nc_ref[...] = last2
    conv = cb_ref[...] + g2 * cw_ref[0:1, :] + g1 * cw_ref[1:2, :] + g * cw_ref[2:3, :]
    act_ref[...] = (_gelu_tanh(conv).reshape(tm, tn) * v).astype(BF16)


def _up_conv_act(x, gain, w_up, s_conv, conv_w, conv_b, *, seq_len, tm=1024, tn=256):
    M, K = x.shape
    dff = conv_b.shape[0]
    if seq_len >= tm:
        tm = _pick_tile(seq_len, tm, SUBLANE)
        seqs, tps = 1, seq_len // tm
    else:
        assert seq_len % SUBLANE == 0
        seqs, tps = _pick_tile(M // seq_len, tm // seq_len, 1), 1
        tm = seqs * seq_len
    tn = _pick_tile(dff, tn, LANE)
    nj = dff // tn
    rows = _pick_tile(tm, 64, SUBLANE)
    assert conv_w.shape[0] == 3 and s_conv.shape[1] == 2 and tm // seqs >= 2
    act, tails = pl.pallas_call(
        functools.partial(_up_conv_kernel, rows=rows, tiles_per_seq=tps, seqs=seqs),
        grid=(M // tm, nj),
        in_specs=[pl.BlockSpec((tm, K), lambda i, j: (i, 0), pipeline_mode=ROW_RESIDENT),
                  pl.BlockSpec((1, K), lambda i, j: (0, 0)),
                  pl.BlockSpec((K, tn), lambda i, j: (0, j)),
                  pl.BlockSpec((K, tn), lambda i, j: (0, nj + j)),
                  pl.BlockSpec((seqs, 2, tn), lambda i, j: (i // tps, 0, j)),
                  pl.BlockSpec((3, tn), lambda i, j: (0, j)),
                  pl.BlockSpec((1, tn), lambda i, j: (0, j))],
        out_specs=[pl.BlockSpec((tm, tn), lambda i, j: (i, j)),
                   pl.BlockSpec((seqs, 2, tn), lambda i, j: (i, 0, j))],
        out_shape=[jax.ShapeDtypeStruct((M, dff), BF16),
                   jax.ShapeDtypeStruct((M // tm * seqs, 2, dff), F32)],
        scratch_shapes=[pltpu.VMEM((tm, K), BF16), pltpu.VMEM((nj if tps > 1 else 1, seqs, 2, tn), F32)],
        compiler_params=_cparams(("arbitrary", "arbitrary")),
        name="up_conv_act",
    )(x, gain.reshape(1, K), w_up, w_up, s_conv, conv_w, conv_b.reshape(1, dff))
    return act, tails[tps - 1::tps]


def _pad_cols(a, width):
    return a if a.shape[-1] == width else jnp.pad(a, [(0, 0)] * (a.ndim - 1) + [(0, width - a.shape[-1])])


def _pad_rows(a, height):
    return a if a.shape[0] == height else jnp.pad(a, [(0, height - a.shape[0])] + [(0, 0)] * (a.ndim - 1))


def _layout(D, QK, GW, R, RW, WL, AL, GL):
    src, o = {}, 0
    for name, w in (("q", QK), ("k", QK), ("v", GW), ("za", R), ("zg", GW), ("r", RW), ("xw", WL),
                    ("kr", RW), ("vr", RW), ("xa", AL), ("xg", GL), ("ga", D), ("gb", D)):
        src[name] = (o, w)
        o += w
    return src


def _prepare_params(lp):
    (w_in, w_alpha2, b_alpha, gla_norm, w_branch_a,
     mu_shift, w0, w_decay2, a0, w_iclr2, w_gate2, k_k, k_a, r_k, ln_x_w, ln_x_b, w_branch_b,
     w_out, g_pre_mix, g_post_mix, g_pre_ffn, g_post_ffn,
     w_up, conv_w, conv_b, w_down, g_pe, w_pe_gate, w_pe) = lp
    D = w_in.shape[0]
    R, QK = w_alpha2.shape
    GW = w_branch_a.shape[0]
    RW = w_branch_b.shape[0]
    WL, AL, GL = w_decay2.shape[0], w_iclr2.shape[0], w_gate2.shape[0]
    src = _layout(D, QK, GW, R, RW, WL, AL, GL)
    pad = lambda w: -(-w // LANE) * LANE
    prm = dict(
        dims=dict(D=D, R=R, QK=QK, GW=GW, RW=RW, WL=WL, AL=AL, GL=GL, DV=gla_norm.shape[0],
                  HD=r_k.shape[1], DFF=conv_b.shape[0]),
        src=src, off={n: src[n][0] for n in src},
        w_in_t=w_in.T,
        w_alpha2=_pad_rows(w_alpha2, LANE).astype(BF16), b_alpha=b_alpha, gla_norm=gla_norm,
        w_branch_a=w_branch_a.astype(BF16), w_branch_b=w_branch_b.astype(BF16),
        w0=w0, w_decay2=_pad_rows(w_decay2, LANE).astype(BF16), a0=a0,
        w_iclr2=_pad_rows(w_iclr2, LANE).astype(BF16), w_gate2=w_gate2.astype(BF16),
        k_k=k_k, k_a=k_a, r_k=r_k.reshape(-1), ln_x_w=ln_x_w, ln_x_b=ln_x_b,
        w_out=w_out.astype(BF16), g_pre_mix=g_pre_mix, g_post_mix=g_post_mix,
        g_pre_ffn=g_pre_ffn, g_post_ffn=g_post_ffn,
        w_up=w_up, conv_w=conv_w, conv_b=conv_b, w_down=w_down.astype(BF16),
        g_pe=g_pe, w_pe_gate=w_pe_gate, w_pe=w_pe.astype(BF16),
    )
    rsrc = src["r"][0]
    rnames = ("r", "xw", "kr", "vr", "xa", "xg")
    prm["rnames"] = rnames
    prm["rsl"] = {n: (src[n][0] - rsrc, src[n][1]) for n in rnames}
    prm["mu"] = {n: _pad_cols(mu_shift[None, prm["rsl"][n][0]:prm["rsl"][n][0] + prm["rsl"][n][1]],
                              pad(prm["rsl"][n][1])) for n in rnames}
    return prm


def _layer(x, p, s_gla, s_rwkv, s_shift, s_conv, prm):
    B, L, D = x.shape
    M = B * L
    dm, dst = prm["dims"], prm["off"]
    GW, RW, DV, HD, DFF, QK = dm["GW"], dm["RW"], dm["DV"], dm["HD"], dm["DFF"], dm["QK"]
    heads = GW // DV
    pad = lambda w: -(-w // LANE) * LANE
    x2 = x.reshape(M, D)

    proj = _norm_matmul_nt(x2, prm["g_pre_mix"], prm["w_in_t"])
    proj3 = proj.reshape(B, L, -1)

    o_a, s_gla_new = _gla(proj3, dst, prm["w_alpha2"], prm["b_alpha"], prm["gla_norm"], s_gla,
                          dk=QK // heads, dv=DV, heads=heads)

    shift_parts = {n: _pad_cols(s_shift[:, None, prm["rsl"][n][0]:prm["rsl"][n][0] + prm["rsl"][n][1]],
                                pad(prm["rsl"][n][1])) for n in prm["rnames"]}
    r, lw, kr, vr, a, g = _rwkv_prep(proj3, dst, shift_parts, prm["mu"], prm["w0"], prm["w_decay2"],
                                     prm["a0"], prm["w_iclr2"], prm["w_gate2"], rw=RW, gl=dm["GL"])
    o_b, s_rwkv_new = _rwkv_chunk(r, lw, kr, vr, a, g, prm["k_k"], prm["k_a"], prm["r_k"],
                                  prm["ln_x_w"], prm["ln_x_b"], s_rwkv, hd=HD)
    new_shift = proj3[:, L - 1, dst["r"]:dst["ga"]]

    mixed = _merge(o_a.reshape(M, GW), o_b.reshape(M, RW), prm["w_branch_a"], prm["w_branch_b"],
                   proj, dst["ga"], dst["gb"])
    x2 = _matmul_norm_residual(mixed, prm["w_out"], x2, prm["g_post_mix"], tm=512, tn=512,
                               single_buffer_rows=False)

    act, new_conv = _up_conv_act(x2, prm["g_pre_ffn"], prm["w_up"], s_conv, prm["conv_w"],
                                 prm["conv_b"], seq_len=L)
    x2 = _matmul_norm_residual(act.reshape(M, DFF), prm["w_down"], x2, prm["g_post_ffn"], tm=512, tn=256,
                               single_buffer_rows=True)

    x2 = _pe_layer(x2, prm["g_pe"], prm["w_pe_gate"], p.reshape(M, -1), prm["w_pe"])
    return x2.reshape(B, L, D), s_gla_new, s_rwkv_new, new_shift, new_conv


def kernel(x_prompt, x_sample, state_gla, state_rwkv, state_shift, state_ffn_conv, p_prompt, p_sample, w_in, w_alpha2, b_alpha, gla_norm, w_branch_a, mu_shift, w0, w_decay2, a0, w_iclr2, w_gate2, k_k, k_a, r_k, ln_x_w, ln_x_b, w_branch_b, w_out, g_pre_mix, g_post_mix, g_pre_ffn, g_post_ffn, w_up, conv_w, conv_b, w_down, g_pe, w_pe_gate, w_pe):
    params = (w_in, w_alpha2, b_alpha, gla_norm, w_branch_a,
              mu_shift, w0, w_decay2, a0, w_iclr2, w_gate2, k_k, k_a, r_k, ln_x_w, ln_x_b, w_branch_b,
              w_out, g_pre_mix, g_post_mix, g_pre_ffn, g_post_ffn,
              w_up, conv_w, conv_b, w_down, g_pe, w_pe_gate, w_pe)
    depth = w_in.shape[0]
    nb = x_prompt.shape[0]
    yp, ys = x_prompt, x_sample
    outs_p = [[], [], [], []]
    outs_s = [[], [], [], []]
    for i in range(depth):
        prm = _prepare_params(tuple(t[i] for t in params))
        z_gla = jnp.zeros((nb,) + state_gla.shape[2:], F32)
        z_rwkv = jnp.zeros((nb,) + state_rwkv.shape[2:], F32)
        z_shift = jnp.zeros((nb,) + state_shift.shape[2:], x_prompt.dtype)
        z_conv = jnp.zeros((nb,) + state_ffn_conv.shape[2:], x_prompt.dtype)
        yp, *st = _layer(yp, p_prompt[i], z_gla, z_rwkv, z_shift, z_conv, prm)
        for acc, s in zip(outs_p, st):
            acc.append(s)
        ys, *st = _layer(ys, p_sample[i], state_gla[i], state_rwkv[i], state_shift[i],
                         state_ffn_conv[i], prm)
        for acc, s in zip(outs_s, st):
            acc.append(s)
    return (yp, ys, *(jnp.stack(a) for a in outs_p), *(jnp.stack(a) for a in outs_s))
```

```python
import functools
import math

import jax
import jax.numpy as jnp
from jax import lax
from jax.experimental import pallas as pl
from jax.experimental.pallas import tpu as pltpu

F32 = jnp.float32
BF16 = jnp.bfloat16

LANE = 128
SUBLANE = 8
VMEM_LIMIT_BYTES = 56 * 1024 * 1024

EPS = 1e-6
GN_EPS = 64e-5
GLA_TAU = 16.0
GLA_CHUNK = 64
RWKV_CHUNK = 64
KK_EPS = 1e-12
DECAY_SCALE = math.exp(-0.5)
GELU_C = math.sqrt(2.0 / math.pi)


ROW_RESIDENT = pl.Buffered(1)


def _cparams(sem):
    return pltpu.CompilerParams(dimension_semantics=sem, vmem_limit_bytes=VMEM_LIMIT_BYTES)


def _sigmoid(x):
    return 1.0 / (1.0 + jnp.exp(-x))


def _log_sigmoid(x):
    return jnp.minimum(x, 0.0) - jnp.log(1.0 + jnp.exp(-jnp.abs(x)))


def _gelu_tanh(x):
    return 0.5 * x * (1.0 + jnp.tanh(GELU_C * (x + 0.044715 * (x * x * x))))


def _split2(x):
    hi = x.astype(BF16)
    lo = (x - hi.astype(F32)).astype(BF16)
    return hi, lo


def _dot(a, b):
    return jnp.dot(a, b, preferred_element_type=F32)


def _dot_nt(a, b):
    return lax.dot_general(a, b, (((1,), (1,)), ((), ())), preferred_element_type=F32)


def _dot_tn(a, b):
    return lax.dot_general(a, b, (((0,), (0,)), ((), ())), preferred_element_type=F32)


def _ones_dot(m01, x):
    hi, lo = _split2(x)
    return _dot(m01, hi) + _dot(m01, lo)


def _dot_ones(x, m01):
    return _dot(x.astype(BF16), m01)


def _lockstep(gens):
    gens = list(gens)
    while gens:
        alive = []
        for g in gens:
            try:
                next(g)
                alive.append(g)
            except StopIteration:
                pass
        gens = alive


def _tri_incl(n):
    r = lax.broadcasted_iota(jnp.int32, (n, n), 0)
    c = lax.broadcasted_iota(jnp.int32, (n, n), 1)
    return r >= c


def _pick_tile(n, target, mult):
    if n <= target:
        return n
    best = None
    t = mult
    while t <= target:
        if n % t == 0:
            best = t
        t += mult
    assert best is not None, (n, target, mult)
    return best


def _rms_rows(x_ref, g_ref, h_ref, rows):
    tm = x_ref.shape[0]

    def body(i, _):
        sl = pl.ds(pl.multiple_of(i * rows, rows), rows)
        x = x_ref[sl, :]
        ms = jnp.mean(x * x, axis=-1, keepdims=True)
        h_ref[sl, :] = (x * lax.rsqrt(ms + EPS) * g_ref[...]).astype(BF16)
        return 0

    lax.fori_loop(0, tm // rows, body, 0)


def _norm_mm_kernel(x_ref, g_ref, wt_ref, o_ref, h_ref, *, rows, n_valid):
    j = pl.program_id(1)

    @pl.when(j == 0)
    def _():
        _rms_rows(x_ref, g_ref, h_ref, rows)

    y = _dot_nt(h_ref[...], wt_ref[...].astype(BF16))
    tn = y.shape[1]
    if n_valid % tn:
        col = lax.broadcasted_iota(jnp.int32, y.shape, 1) + j * tn
        y = jnp.where(col < n_valid, y, 0.0)
    o_ref[...] = y


def _window(src, width):
    base = src // LANE * LANE
    shift = src - base
    wp = -(-width // LANE) * LANE
    return base, shift, (wp if shift == 0 else wp + LANE), wp


def _norm_matmul_nt(x, gain, wt, *, tm=1024, tn=512):
    M, K = x.shape
    nw = wt.shape[0]
    tm = _pick_tile(M, tm, SUBLANE)
    tn = min(tn, -(-nw // LANE) * LANE)
    N = -(-nw // tn) * tn
    rows = _pick_tile(tm, 64, SUBLANE)
    return pl.pallas_call(
        functools.partial(_norm_mm_kernel, rows=rows, n_valid=nw),
        grid=(M // tm, N // tn),
        in_specs=[pl.BlockSpec((tm, K), lambda i, j: (i, 0), pipeline_mode=ROW_RESIDENT),
                  pl.BlockSpec((1, K), lambda i, j: (0, 0)),
                  pl.BlockSpec((tn, K), lambda i, j: (j, 0))],
        out_specs=pl.BlockSpec((tm, tn), lambda i, j: (i, j)),
        out_shape=jax.ShapeDtypeStruct((M, N), F32),
        scratch_shapes=[pltpu.VMEM((tm, K), BF16)],
        compiler_params=_cparams(("parallel", "arbitrary")),
        name="norm_matmul",
    )(x, gain.reshape(1, K), wt)


def _pe_kernel(x_ref, g_ref, w_ref, p_ref, wp_ref, o_ref, h_ref, *, rows, tn):
    j = pl.program_id(1)

    @pl.when(j == 0)
    def _():
        _rms_rows(x_ref, g_ref, h_ref, rows)

    gate = _sigmoid(_dot(h_ref[...], w_ref[...].astype(BF16)))
    pe = _dot(p_ref[...].astype(BF16), wp_ref[...])
    xs = x_ref[:, pl.ds(pl.multiple_of(j * tn, tn), tn)]
    o_ref[...] = xs + gate * pe


def _pe_layer(x, gain, w_gate, p, w_pe, *, tm=1024, tn=512):
    M, K = x.shape
    N = w_gate.shape[1]
    P = p.shape[1]
    tm = _pick_tile(M, tm, SUBLANE)
    tn = _pick_tile(N, tn, LANE)
    rows = _pick_tile(tm, 64, SUBLANE)
    return pl.pallas_call(
        functools.partial(_pe_kernel, rows=rows, tn=tn),
        grid=(M // tm, N // tn),
        in_specs=[pl.BlockSpec((tm, K), lambda i, j: (i, 0), pipeline_mode=ROW_RESIDENT),
                  pl.BlockSpec((1, K), lambda i, j: (0, 0)),
                  pl.BlockSpec((K, tn), lambda i, j: (0, j)),
                  pl.BlockSpec((tm, P), lambda i, j: (i, 0)),
                  pl.BlockSpec((P, tn), lambda i, j: (0, j))],
        out_specs=pl.BlockSpec((tm, tn), lambda i, j: (i, j)),
        out_shape=jax.ShapeDtypeStruct((M, N), F32),
        scratch_shapes=[pltpu.VMEM((tm, K), BF16)],
        compiler_params=_cparams(("parallel", "arbitrary")),
        name="pe_layer",
    )(x, gain.reshape(1, K), w_gate, p, w_pe)


def _mm_norm_res_kernel(a_ref, w_ref, x_ref, g_ref, o_ref, *, tn, rows):
    j = pl.program_id(1)
    o_ref[:, pl.ds(pl.multiple_of(j * tn, tn), tn)] = _dot(a_ref[...], w_ref[...])

    @pl.when(j == pl.num_programs(1) - 1)
    def _():
        tm = o_ref.shape[0]

        def body(i, _):
            sl = pl.ds(pl.multiple_of(i * rows, rows), rows)
            y = o_ref[sl, :]
            ms = jnp.mean(y * y, axis=-1, keepdims=True)
            o_ref[sl, :] = x_ref[sl, :] + y * lax.rsqrt(ms + EPS) * g_ref[...]
            return 0

        lax.fori_loop(0, tm // rows, body, 0)


def _matmul_norm_residual(a, w, x, gain, *, tm, tn, single_buffer_rows):
    M, K = a.shape
    N = w.shape[1]
    tm = _pick_tile(M, tm, SUBLANE)
    tn = _pick_tile(N, tn, LANE)
    rows = _pick_tile(tm, 64, SUBLANE)
    mode = dict(pipeline_mode=ROW_RESIDENT) if single_buffer_rows else {}
    return pl.pallas_call(
        functools.partial(_mm_norm_res_kernel, tn=tn, rows=rows),
        grid=(M // tm, N // tn),
        in_specs=[pl.BlockSpec((tm, K), lambda i, j: (i, 0), **mode),
                  pl.BlockSpec((K, tn), lambda i, j: (0, j)),
                  pl.BlockSpec((tm, N), lambda i, j: (i, 0), **mode),
                  pl.BlockSpec((1, N), lambda i, j: (0, 0))],
        out_specs=pl.BlockSpec((tm, N), lambda i, j: (i, 0)),
        out_shape=jax.ShapeDtypeStruct((M, N), F32),
        compiler_params=_cparams(("parallel", "arbitrary")),
        name="matmul_norm_residual",
    )(a, w, x, gain.reshape(1, N))


def _merge_kernel(oa_ref, ob_ref, wa_ref, wb_ref, ga_ref, gb_ref, o_ref, *, sa, sb):
    tn = o_ref.shape[1]
    ya = _dot(oa_ref[...], wa_ref[...].astype(BF16))
    yb = _dot(ob_ref[...], wb_ref[...].astype(BF16))
    ga = ga_ref[:, sa:sa + tn]
    gb = gb_ref[:, sb:sb + tn]
    o_ref[...] = (_sigmoid(ga) * ya + _sigmoid(gb) * yb).astype(BF16)


def _gate_window_spec(tm, tn, off, n_cols):
    base, shift, _, _ = _window(off, tn)
    ww = tn if shift == 0 else tn + LANE
    spec = pl.BlockSpec((pl.Element(tm), pl.Element(ww)),
                        lambda i, j: (pl.multiple_of(i * tm, SUBLANE), pl.multiple_of(base + j * tn, LANE)))
    return spec, shift, base + n_cols - tn + ww


def _merge(oa, ob, wa, wb, proj, ga_off, gb_off, *, tm=1024, tn=512):
    M, KA = oa.shape
    KB = ob.shape[1]
    N = wa.shape[1]
    tm = _pick_tile(M, tm, SUBLANE)
    tn = _pick_tile(N, tn, LANE)
    ga_spec, sa, enda = _gate_window_spec(tm, tn, ga_off, N)
    gb_spec, sb, endb = _gate_window_spec(tm, tn, gb_off, N)
    assert max(enda, endb) <= proj.shape[1]
    return pl.pallas_call(
        functools.partial(_merge_kernel, sa=sa, sb=sb),
        grid=(M // tm, N // tn),
        in_specs=[pl.BlockSpec((tm, KA), lambda i, j: (i, 0)),
                  pl.BlockSpec((tm, KB), lambda i, j: (i, 0)),
                  pl.BlockSpec((KA, tn), lambda i, j: (0, j)),
                  pl.BlockSpec((KB, tn), lambda i, j: (0, j)),
                  ga_spec, gb_spec],
        out_specs=pl.BlockSpec((tm, tn), lambda i, j: (i, j)),
        out_shape=jax.ShapeDtypeStruct((M, N), BF16),
        compiler_params=_cparams(("parallel", "arbitrary")),
        name="merge",
    )(oa, ob, wa, wb, proj, proj)


def _gla_kernel(q_ref, k_ref, v_ref, za_ref, zg_ref, wa_ref, ba_ref, gn_ref, s0_ref,
                o_ref, sn_ref, s_ref, *, scale, c, heads, zshift, single_chunk):
    l = pl.program_id(1)
    bb = q_ref.shape[0]
    dk = q_ref.shape[2] // heads
    dv = v_ref.shape[2] // heads

    if not single_chunk:
        @pl.when(l == 0)
        def _():
            s_ref[...] = s0_ref[...].reshape(bb * heads, dk, dv)

    x = _dot(za_ref[...].reshape(bb * c, LANE).astype(BF16), wa_ref[...]) + ba_ref[...]
    la_seqs = _log_sigmoid(x) * (1.0 / GLA_TAU)
    tri = _tri_incl(c)
    tri01 = jnp.where(tri, 1.0, 0.0).astype(BF16)
    ones = jnp.ones((c, LANE), BF16)
    la_seq = [la_seqs[bi * c:(bi + 1) * c] for bi in range(bb)]
    b_seq = [_ones_dot(tri01, la) for la in la_seq]

    def head(bi, h):
        ks = slice(h * dk, (h + 1) * dk)
        vs = slice(h * dv, (h + 1) * dv)
        la = la_seq[bi][:, ks]
        b = b_seq[bi][:, ks]
        b_end = b[c - 1:c, :]
        q = q_ref[bi, :, ks] * scale
        k = k_ref[bi, :, ks]
        v = v_ref[bi, :, vs].astype(BF16)
        qd = (q * jnp.exp(b)).astype(BF16)
        kd = (k * jnp.exp(-b)).astype(BF16)
        s = s0_ref[bi, h] if single_chunk else s_ref[bi * heads + h]
        att = _dot_nt(qd, kd)
        o = _dot(qd, s.astype(BF16))
        k_end = (k * jnp.exp(b_end - b)).astype(BF16)
        la_hi, la_lo = _split2(la)
        dec = jnp.exp(_dot_tn(la_hi, ones) + _dot_tn(la_lo, ones))
        dec = jnp.concatenate([dec] * (dv // LANE), axis=1)
        s_new = s * dec + _dot_tn(k_end, v)
        if single_chunk:
            sn_ref[bi, h] = s_new
        else:
            s_ref[bi * heads + h] = s_new
        yield
        o = o + _dot(jnp.where(tri, att, 0.0).astype(BF16), v)
        yield
        ms = jnp.mean(o * o, axis=-1, keepdims=True)
        on = o * lax.rsqrt(ms + EPS) * gn_ref[...]
        zg = zg_ref[bi, :, zshift + h * dv:zshift + (h + 1) * dv]
        o_ref[bi, :, vs] = (on * (zg * _sigmoid(zg))).astype(BF16)

    _lockstep([head(bi, h) for bi in range(bb) for h in range(heads)])

    if not single_chunk:
        @pl.when(l == pl.num_programs(1) - 1)
        def _():
            sn_ref[...] = s_ref[...].reshape(bb, heads, dk, dv)


def _gla(proj3, offs, w_alpha2p, b_alpha, gla_norm, s0, *, dk, dv, heads):
    B, L, _ = proj3.shape
    c = min(GLA_CHUNK, L)
    qk, gw = heads * dk, heads * dv
    assert L % c == 0 and dv % LANE == 0
    for name, wdt in (("q", qk), ("k", qk), ("v", gw), ("za", LANE)):
        assert offs[name] % wdt == 0
    jq, jk, jv, jz = offs["q"] // qk, offs["k"] // qk, offs["v"] // gw, offs["za"] // LANE
    zbase, zshift, zww, _ = _window(offs["zg"], gw)
    assert zbase + zww <= proj3.shape[2]
    single_chunk = L == c
    bb = _pick_tile(B, 4 if single_chunk else 2, 1)
    return pl.pallas_call(
        functools.partial(_gla_kernel, scale=dk ** -0.5, c=c, heads=heads, zshift=zshift,
                          single_chunk=single_chunk),
        grid=(B // bb, L // c),
        in_specs=[pl.BlockSpec((bb, c, qk), lambda b, l: (b, l, jq)),
                  pl.BlockSpec((bb, c, qk), lambda b, l: (b, l, jk)),
                  pl.BlockSpec((bb, c, gw), lambda b, l: (b, l, jv)),
                  pl.BlockSpec((bb, c, LANE), lambda b, l: (b, l, jz)),
                  pl.BlockSpec((pl.Element(bb), pl.Element(c), pl.Element(zww)),
                               lambda b, l: (b * bb, pl.multiple_of(l * c, c), zbase)),
                  pl.BlockSpec((LANE, qk), lambda b, l: (0, 0)),
                  pl.BlockSpec((1, qk), lambda b, l: (0, 0)),
                  pl.BlockSpec((1, dv), lambda b, l: (0, 0)),
                  pl.BlockSpec((bb, heads, dk, dv), lambda b, l: (b, 0, 0, 0))],
        out_specs=[pl.BlockSpec((bb, c, gw), lambda b, l: (b, l, 0)),
                   pl.BlockSpec((bb, heads, dk, dv), lambda b, l: (b, 0, 0, 0))],
        out_shape=[jax.ShapeDtypeStruct((B, L, gw), BF16),
                   jax.ShapeDtypeStruct((B, heads, dk, dv), F32)],
        scratch_shapes=[pltpu.VMEM((SUBLANE, LANE) if single_chunk else (bb * heads, dk, dv), F32)],
        compiler_params=_cparams(("parallel", "arbitrary")),
        name="gla",
    )(proj3, proj3, proj3, proj3, proj3, w_alpha2p, b_alpha.reshape(1, -1), gla_norm.reshape(1, dv), s0)


def _rwkv_prep_kernel(r_ref, xw_ref, kr_ref, vr_ref, xa_ref, xg_ref,
                      sr_ref, sxw_ref, skr_ref, svr_ref, sxa_ref, sxg_ref,
                      mr_ref, mxw_ref, mkr_ref, mvr_ref, mxa_ref, mxg_ref,
                      w0_ref, wd_ref, a0_ref, wi_ref, wg_ref,
                      ro_ref, lw_ref, ko_ref, vo_ref, ao_ref, go_ref,
                      cr_ref, cxw_ref, ckr_ref, cvr_ref, cxa_ref, cxg_ref, *, shifts):
    l = pl.program_id(1)
    bb, tl = r_ref.shape[0], r_ref.shape[1]
    sh_r, sh_xw, sh_kr, sh_vr, sh_xa, sh_xg = shifts

    def shift_mix(z_ref, s0, s_ref, m_ref, c_ref):
        cur = z_ref[:, :, s0:s0 + m_ref.shape[1]]
        first = jnp.where(l == 0, s_ref[...], c_ref[...])
        row = lax.broadcasted_iota(jnp.int32, cur.shape, 1)
        prev = jnp.where(row == 0, first, pltpu.roll(cur, 1, 1))
        c_ref[...] = cur[:, tl - 1:tl, :]
        return cur + (prev - cur) * m_ref[...]

    def flat(x):
        return x.reshape(bb * tl, x.shape[2])

    ro_ref[...] = shift_mix(r_ref, sh_r, sr_ref, mr_ref, cr_ref)
    ko_ref[...] = shift_mix(kr_ref, sh_kr, skr_ref, mkr_ref, ckr_ref)
    vo_ref[...] = shift_mix(vr_ref, sh_vr, svr_ref, mvr_ref, cvr_ref)
    xw = flat(shift_mix(xw_ref, sh_xw, sxw_ref, mxw_ref, cxw_ref))
    xa = flat(shift_mix(xa_ref, sh_xa, sxa_ref, mxa_ref, cxa_ref))
    xg = flat(shift_mix(xg_ref, sh_xg, sxg_ref, mxg_ref, cxg_ref))
    z = w0_ref[...] + _dot(jnp.tanh(xw).astype(BF16), wd_ref[...])
    rw = z.shape[1]
    lw_ref[...] = (-DECAY_SCALE * _sigmoid(z)).reshape(bb, tl, rw)
    ao_ref[...] = _sigmoid(a0_ref[...] + _dot(xa.astype(BF16), wi_ref[...])).reshape(bb, tl, rw)
    go_ref[...] = _dot(_sigmoid(xg).astype(BF16), wg_ref[...]).reshape(bb, tl, rw)


def _rwkv_prep(proj3, offs, shift_parts, mu_parts, w0, w_decay2p, a0, w_iclr2p, w_gate2, *, rw, gl):
    B, L, _ = proj3.shape
    tl = _pick_tile(L, 256, SUBLANE)
    bb = _pick_tile(B, max(1, 256 // L), 1) if tl == L else 1
    names = ("r", "xw", "kr", "vr", "xa", "xg")
    widths = {"r": rw, "xw": LANE, "kr": rw, "vr": rw, "xa": LANE, "xg": gl}
    in_specs, args, shifts = [], [], []
    for n in names:
        base, shift, ww, wp = _window(offs[n], widths[n])
        assert wp == widths[n] and base + ww <= proj3.shape[2]
        shifts.append(shift)
        in_specs.append(pl.BlockSpec(
            (pl.Element(bb), pl.Element(tl), pl.Element(ww)),
            functools.partial(lambda b, l, base: (b * bb, pl.multiple_of(l * tl, SUBLANE), base), base=base)))
        args.append(proj3)
    for n in names:
        in_specs.append(pl.BlockSpec((bb, 1, widths[n]), lambda b, l: (b, 0, 0)))
        args.append(shift_parts[n])
    for n in names:
        in_specs.append(pl.BlockSpec((1, widths[n]), lambda b, l: (0, 0)))
        args.append(mu_parts[n])
    for arr in (w0.reshape(1, rw), w_decay2p, a0.reshape(1, rw), w_iclr2p, w_gate2):
        in_specs.append(pl.BlockSpec(arr.shape, lambda b, l: (0, 0)))
        args.append(arr)
    out_spec = pl.BlockSpec((bb, tl, rw), lambda b, l: (b, l, 0))
    out_sd = jax.ShapeDtypeStruct((B, L, rw), F32)
    return pl.pallas_call(
        functools.partial(_rwkv_prep_kernel, shifts=tuple(shifts)),
        grid=(B // bb, L // tl),
        in_specs=in_specs,
        out_specs=[out_spec] * 6,
        out_shape=[out_sd] * 6,
        scratch_shapes=[pltpu.VMEM((bb, 1, widths[n]), F32) for n in names],
        compiler_params=_cparams(("parallel", "arbitrary")),
        name="rwkv_prep",
    )(*args)


def _rwkv_chunk_kernel(r_ref, lw_ref, kr_ref, v_ref, a_ref, g_ref,
                       kk_w_ref, ka_w_ref, rk_w_ref, lnw_ref, lnb_ref, s0_ref,
                       o_ref, sn_ref, w_ref, *, c, hd, gp):
    l = pl.program_id(2)
    bb, tl = r_ref.shape[0], r_ref.shape[1]
    n2 = 2 * c
    lane_c = lax.broadcasted_iota(jnp.int32, (c, LANE), 1)
    head0_c = lane_c < hd
    row2 = lax.broadcasted_iota(jnp.int32, (n2, n2), 0)
    col2 = lax.broadcasted_iota(jnp.int32, (n2, n2), 1)
    same = (2 * row2 + 1 - n2) * (2 * col2 + 1 - n2) > 0
    strict = jnp.logical_and(same, row2 > col2)
    incl = jnp.logical_and(same, row2 >= col2)
    eye = jnp.where(row2 == col2, 1.0, 0.0)
    tri = jnp.where(_tri_incl(c), 1.0, 0.0).astype(BF16)
    lr = lax.broadcasted_iota(jnp.int32, (LANE, LANE), 0)
    lc = lax.broadcasted_iota(jnp.int32, (LANE, LANE), 1)
    seg = jnp.where((2 * lr + 1 - LANE) * (2 * lc + 1 - LANE) > 0, 1.0, 0.0).astype(BF16)

    @pl.when(l == 0)
    def _():
        zero = jnp.zeros((hd, hd), F32)
        for bi in range(bb):
            for gi in range(gp):
                s_e = s0_ref[bi, 2 * gi]
                s_o = s0_ref[bi, 2 * gi + 1]
                w_ref[bi * gp + gi] = jnp.concatenate([jnp.concatenate([s_e, zero], axis=1),
                                                       jnp.concatenate([zero, s_o], axis=1)], axis=0)

    def stack(x):
        return jnp.concatenate([jnp.where(head0_c, x, 0.0), jnp.where(head0_c, 0.0, x)], axis=0)

    def unstack(x):
        return x[0:c] + x[c:n2]

    def chunk_pair(sl, bi, gi):
        ls = slice(gi * LANE, (gi + 1) * LANE)
        wi = bi * gp + gi
        r = r_ref[bi, sl, ls]
        lw = lw_ref[bi, sl, ls]
        kr = kr_ref[bi, sl, ls]
        v = v_ref[bi, sl, ls]
        a = a_ref[bi, sl, ls]
        kk = kr * kk_w_ref[:, ls]
        kk_ss = _dot_ones(kk * kk, seg)
        cum = _ones_dot(tri, lw)
        yield
        kk = kk / jnp.maximum(jnp.sqrt(kk_ss), KK_EPS)
        k = kr * (1.0 + (a - 1.0) * ka_w_ref[:, ls])
        cum_end = cum[c - 1:c, :]
        e_neg = jnp.exp(-cum)
        e_rem = jnp.exp(cum_end - cum)
        al = kk * a
        al_t = stack(al * e_neg).astype(BF16)
        k_t = stack(k * e_neg).astype(BF16)
        be_t = stack(-kk * jnp.exp(cum - lw)).astype(BF16)
        r_t = stack(r * jnp.exp(cum)).astype(BF16)
        v_s = stack(v).astype(BF16)
        lhs = jnp.concatenate([be_t, r_t], axis=0)
        sc_a = _dot_nt(lhs, al_t)
        sc_k = _dot_nt(lhs, k_t)
        w = w_ref[wi]
        rd = _dot_nt(lhs, w.astype(BF16))
        yield
        l_a =jnp.where(strict, sc_a[0:n2], 0.0)
        l_k = jnp.where(strict, sc_k[0:n2], 0.0).astype(BF16)
        m_a = jnp.where(incl, sc_a[n2:2 * n2], 0.0).astype(BF16)
        m_k = jnp.where(incl, sc_k[n2:2 * n2], 0.0).astype(BF16)
        t_inv = eye + l_a
        lp = l_a
        rhs_u = rd[0:n2] + _dot(l_k, v_s)
        o_s = rd[n2:2 * n2] + _dot(m_k, v_s)
        bonus = _dot_ones(r * k * rk_w_ref[:, ls], seg) * v
        span = 1
        while 2 * span < c:
            lpb = lp.astype(BF16)
            lp = _dot(lpb, lpb)
            yield
            t_inv = t_inv + _dot(t_inv.astype(BF16), lp.astype(BF16))
            yield
            span *= 2
        u_s = _dot(t_inv.astype(BF16), rhs_u.astype(BF16))
        yield
        u_b = u_s.astype(BF16)
        o_s = o_s + _dot(m_a, u_b)
        uv = jnp.concatenate([u_b, v_s], axis=0)
        ak = jnp.concatenate([stack(al * e_rem), stack(k * e_rem)], axis=0).astype(BF16)
        w_ref[wi] = w * jnp.exp(cum_end) + _dot_tn(uv, ak)
        yield
        o = unstack(o_s)
        mu = _dot_ones(o, seg) * (1.0 / hd)
        yield
        d = o - mu
        var = _dot_ones(d * d, seg) * (1.0 / hd)
        yield
        on = d * lax.rsqrt(var + GN_EPS) * lnw_ref[:, ls] + lnb_ref[:, ls]
        o_ref[bi, sl, ls] = ((on + bonus) * g_ref[bi, sl, ls]).astype(BF16)

    def chunk(ci, _):
        sl = pl.ds(pl.multiple_of(ci * c, c), c)
        _lockstep([chunk_pair(sl, bi, gi) for bi in range(bb) for gi in range(gp)])
        return 0

    lax.fori_loop(0, tl // c, chunk, 0)

    @pl.when(l == pl.num_programs(2) - 1)
    def _():
        for bi in range(bb):
            for gi in range(gp):
                wf = w_ref[bi * gp + gi]
                sn_ref[bi, 2 * gi] = wf[0:hd, 0:hd]
                sn_ref[bi, 2 * gi + 1] = wf[hd:2 * hd, hd:2 * hd]


def _rwkv_chunk(r, lw, kr, vr, a, g, k_k, k_a, r_k, ln_w, ln_b, s0p, *, hd):
    B, L, RW = r.shape
    assert 2 * hd == LANE and RW % LANE == 0
    c = min(RWKV_CHUNK, L)
    tl = _pick_tile(L, 256, c)
    assert L % c == 0 and tl % c == 0 and c % SUBLANE == 0
    npair = RW // LANE
    gp = _pick_tile(npair, 16, 1)
    bb = _pick_tile(B, 4, 1) if L == c else 1
    seq = pl.BlockSpec((bb, tl, gp * LANE), lambda b, p, l: (b, l, p))
    par = pl.BlockSpec((1, gp * LANE), lambda b, p, l: (0, p))
    st = pl.BlockSpec((bb, 2 * gp, hd, hd), lambda b, p, l: (b, p, 0, 0))
    return pl.pallas_call(
        functools.partial(_rwkv_chunk_kernel, c=c, hd=hd, gp=gp),
        grid=(B // bb, npair // gp, L // tl),
        in_specs=[seq] * 6 + [par] * 5 + [st],
        out_specs=[seq, st],
        out_shape=[jax.ShapeDtypeStruct((B, L, RW), BF16),
                   jax.ShapeDtypeStruct((B, 2 * npair, hd, hd), F32)],
        scratch_shapes=[pltpu.VMEM((bb * gp, LANE, LANE), F32)],
        compiler_params=_cparams(("parallel", "parallel", "arbitrary")),
        name="rwkv_chunk",
    )(r, lw, kr, vr, a, g, k_k.reshape(1, RW), k_a.reshape(1, RW), r_k.reshape(1, RW),
      ln_w.reshape(1, RW), ln_b.reshape(1, RW), s0p)


def _up_conv_kernel(x_ref, gn_ref, wg_ref, wv_ref, s_ref, cw_ref, cb_ref, act_ref, nc_ref, h_ref, c_ref,
                    *, rows, tiles_per_seq, seqs):
    i, j = pl.program_id(0), pl.program_id(1)
    tm = x_ref.shape[0]
    tl = tm // seqs

    @pl.when(j == 0)
    def _():
        _rms_rows(x_ref, gn_ref, h_ref, rows)

    h = h_ref[...]
    g = _dot(h, wg_ref[...].astype(BF16))
    v = _dot(h, wv_ref[...].astype(BF16))
    tn = g.shape[1]
    g = g.reshape(seqs, tl, tn)
    first = (i % tiles_per_seq) == 0
    ci = j if tiles_per_seq > 1 else 0
    prev = jnp.where(first, s_ref[...], c_ref[ci])
    row = lax.broadcasted_iota(jnp.int32, g.shape, 1)
    g1 = jnp.where(row == 0, prev[:, 1:2], pltpu.roll(g, 1, 1))
    g2 = jnp.where(row == 0, prev[:, 0:1], jnp.where(row == 1, prev[:, 1:2], pltpu.roll(g, 2, 1)))
    last2 = g[:, tl - 2:tl, :]
    c_ref[ci] = last2
    nc_ref[...] = last2
    conv = cb_ref[...] + g2 * cw_ref[0:1, :] + g1 * cw_ref[1:2, :] + g * cw_ref[2:3, :]
    act_ref[...] = (_gelu_tanh(conv).reshape(tm, tn) * v).astype(BF16)


def _up_conv_act(x, gain, w_up, s_conv, conv_w, conv_b, *, seq_len, tm=1024, tn=256):
    M, K = x.shape
    dff = conv_b.shape[0]
    if seq_len >= tm:
        tm = _pick_tile(seq_len, tm, SUBLANE)
        seqs, tps = 1, seq_len // tm
    else:
        assert seq_len % SUBLANE == 0
        seqs, tps = _pick_tile(M // seq_len, tm // seq_len, 1), 1
        tm = seqs * seq_len
    tn = _pick_tile(dff, tn, LANE)
    nj = dff // tn
    rows = _pick_tile(tm, 64, SUBLANE)
    assert conv_w.shape[0] == 3 and s_conv.shape[1] == 2 and tm // seqs >= 2
    act, tails = pl.pallas_call(
        functools.partial(_up_conv_kernel, rows=rows, tiles_per_seq=tps, seqs=seqs),
        grid=(M // tm, nj),
        in_specs=[pl.BlockSpec((tm, K), lambda i, j: (i, 0), pipeline_mode=ROW_RESIDENT),
                  pl.BlockSpec((1, K), lambda i, j: (0, 0)),
                  pl.BlockSpec((K, tn), lambda i, j: (0, j)),
                  pl.BlockSpec((K, tn), lambda i, j: (0, nj + j)),
                  pl.BlockSpec((seqs, 2, tn), lambda i, j: (i // tps, 0, j)),
                  pl.BlockSpec((3, tn), lambda i, j: (0, j)),
                  pl.BlockSpec((1, tn), lambda i, j: (0, j))],
        out_specs=[pl.BlockSpec((tm, tn), lambda i, j: (i, j)),
                   pl.BlockSpec((seqs, 2, tn), lambda i, j: (i, 0, j))],
        out_shape=[jax.ShapeDtypeStruct((M, dff), BF16),
                   jax.ShapeDtypeStruct((M // tm * seqs, 2, dff), F32)],
        scratch_shapes=[pltpu.VMEM((tm, K), BF16), pltpu.VMEM((nj if tps > 1 else 1, seqs, 2, tn), F32)],
        compiler_params=_cparams(("arbitrary", "arbitrary")),
        name="up_conv_act",
    )(x, gain.reshape(1, K), w_up, w_up, s_conv, conv_w, conv_b.reshape(1, dff))
    return act, tails[tps - 1::tps]


def _pad_cols(a, width):
    return a if a.shape[-1] == width else jnp.pad(a, [(0, 0)] * (a.ndim - 1) + [(0, width - a.shape[-1])])


def _pad_rows(a, height):
    return a if a.shape[0] == height else jnp.pad(a, [(0, height - a.shape[0])] + [(0, 0)] * (a.ndim - 1))


def _layout(D, QK, GW, R, RW, WL, AL, GL):
    src, o = {}, 0
    for name, w in (("q", QK), ("k", QK), ("v", GW), ("za", R), ("zg", GW), ("r", RW), ("xw", WL),
                    ("kr", RW), ("vr", RW), ("xa", AL), ("xg", GL), ("ga", D), ("gb", D)):
        src[name] = (o, w)
        o += w
    return src


def _prepare_params(lp):
    (w_in, w_alpha2, b_alpha, gla_norm, w_branch_a,
     mu_shift, w0, w_decay2, a0, w_iclr2, w_gate2, k_k, k_a, r_k, ln_x_w, ln_x_b, w_branch_b,
     w_out, g_pre_mix, g_post_mix, g_pre_ffn, g_post_ffn,
     w_up, conv_w, conv_b, w_down, g_pe, w_pe_gate, w_pe) = lp
    D = w_in.shape[0]
    R, QK = w_alpha2.shape
    GW = w_branch_a.shape[0]
    RW = w_branch_b.shape[0]
    WL, AL, GL = w_decay2.shape[0], w_iclr2.shape[0], w_gate2.shape[0]
    src = _layout(D, QK, GW, R, RW, WL, AL, GL)
    pad = lambda w: -(-w // LANE) * LANE
    prm = dict(
        dims=dict(D=D, R=R, QK=QK, GW=GW, RW=RW, WL=WL, AL=AL, GL=GL, DV=gla_norm.shape[0],
                  HD=r_k.shape[1], DFF=conv_b.shape[0]),
        src=src, off={n: src[n][0] for n in src},
        w_in_t=w_in.T,
        w_alpha2=_pad_rows(w_alpha2, LANE).astype(BF16), b_alpha=b_alpha, gla_norm=gla_norm,
        w_branch_a=w_branch_a, w_branch_b=w_branch_b,
        w0=w0, w_decay2=_pad_rows(w_decay2, LANE).astype(BF16), a0=a0,
        w_iclr2=_pad_rows(w_iclr2, LANE).astype(BF16), w_gate2=w_gate2.astype(BF16),
        k_k=k_k, k_a=k_a, r_k=r_k.reshape(-1), ln_x_w=ln_x_w, ln_x_b=ln_x_b,
        w_out=w_out.astype(BF16), g_pre_mix=g_pre_mix, g_post_mix=g_post_mix,
        g_pre_ffn=g_pre_ffn, g_post_ffn=g_post_ffn,
        w_up=w_up, conv_w=conv_w, conv_b=conv_b, w_down=w_down.astype(BF16),
        g_pe=g_pe, w_pe_gate=w_pe_gate, w_pe=w_pe.astype(BF16),
    )
    rsrc = src["r"][0]
    rnames = ("r", "xw", "kr", "vr", "xa", "xg")
    prm["rnames"] = rnames
    prm["rsl"] = {n: (src[n][0] - rsrc, src[n][1]) for n in rnames}
    prm["mu"] = {n: _pad_cols(mu_shift[None, prm["rsl"][n][0]:prm["rsl"][n][0] + prm["rsl"][n][1]],
                              pad(prm["rsl"][n][1])) for n in rnames}
    return prm


def _layer(x, p, s_gla, s_rwkv, s_shift, s_conv, prm):
    B, L, D = x.shape
    M = B * L
    dm, dst = prm["dims"], prm["off"]
    GW, RW, DV, HD, DFF, QK = dm["GW"], dm["RW"], dm["DV"], dm["HD"], dm["DFF"], dm["QK"]
    heads = GW // DV
    pad = lambda w: -(-w // LANE) * LANE
    x2 = x.reshape(M, D)

    proj = _norm_matmul_nt(x2, prm["g_pre_mix"], prm["w_in_t"])
    proj3 = proj.reshape(B, L, -1)

    o_a, s_gla_new = _gla(proj3, dst, prm["w_alpha2"], prm["b_alpha"], prm["gla_norm"], s_gla,
                          dk=QK // heads, dv=DV, heads=heads)

    shift_parts = {n: _pad_cols(s_shift[:, None, prm["rsl"][n][0]:prm["rsl"][n][0] + prm["rsl"][n][1]],
                                pad(prm["rsl"][n][1])) for n in prm["rnames"]}
    r, lw, kr, vr, a, g = _rwkv_prep(proj3, dst, shift_parts, prm["mu"], prm["w0"], prm["w_decay2"],
                                     prm["a0"], prm["w_iclr2"], prm["w_gate2"], rw=RW, gl=dm["GL"])
    o_b, s_rwkv_new = _rwkv_chunk(r, lw, kr, vr, a, g, prm["k_k"], prm["k_a"], prm["r_k"],
                                  prm["ln_x_w"], prm["ln_x_b"], s_rwkv, hd=HD)
    new_shift = proj3[:, L - 1, dst["r"]:dst["ga"]]

    mixed = _merge(o_a.reshape(M, GW), o_b.reshape(M, RW), prm["w_branch_a"], prm["w_branch_b"],
                   proj, dst["ga"], dst["gb"])
    x2 = _matmul_norm_residual(mixed, prm["w_out"], x2, prm["g_post_mix"], tm=512, tn=512,
                               single_buffer_rows=False)

    act, new_conv = _up_conv_act(x2, prm["g_pre_ffn"], prm["w_up"], s_conv, prm["conv_w"],
                                 prm["conv_b"], seq_len=L)
    x2 = _matmul_norm_residual(act.reshape(M, DFF), prm["w_down"], x2, prm["g_post_ffn"], tm=512, tn=256,
                               single_buffer_rows=True)

    x2 = _pe_layer(x2, prm["g_pe"], prm["w_pe_gate"], p.reshape(M, -1), prm["w_pe"])
    return x2.reshape(B, L, D), s_gla_new, s_rwkv_new, new_shift, new_conv


def kernel(x_prompt, x_sample, state_gla, state_rwkv, state_shift, state_ffn_conv, p_prompt, p_sample, w_in, w_alpha2, b_alpha, gla_norm, w_branch_a, mu_shift, w0, w_decay2, a0, w_iclr2, w_gate2, k_k, k_a, r_k, ln_x_w, ln_x_b, w_branch_b, w_out, g_pre_mix, g_post_mix, g_pre_ffn, g_post_ffn, w_up, conv_w, conv_b, w_down, g_pe, w_pe_gate, w_pe):
    params = (w_in, w_alpha2, b_alpha, gla_norm, w_branch_a,
              mu_shift, w0, w_decay2, a0, w_iclr2, w_gate2, k_k, k_a, r_k, ln_x_w, ln_x_b, w_branch_b,
              w_out, g_pre_mix, g_post_mix, g_pre_ffn, g_post_ffn,
              w_up, conv_w, conv_b, w_down, g_pe, w_pe_gate, w_pe)
    depth = w_in.shape[0]
    nb = x_prompt.shape[0]
    yp, ys = x_prompt, x_sample
    outs_p = [[], [], [], []]
    outs_s = [[], [], [], []]
    for i in range(depth):
        prm = _prepare_params(tuple(t[i] for t in params))
        z_gla = jnp.zeros((nb,) + state_gla.shape[2:], F32)
        z_rwkv = jnp.zeros((nb,) + state_rwkv.shape[2:], F32)
        z_shift = jnp.zeros((nb,) + state_shift.shape[2:], x_prompt.dtype)
        z_conv = jnp.zeros((nb,) + state_ffn_conv.shape[2:], x_prompt.dtype)
        yp, *st = _layer(yp, p_prompt[i], z_gla, z_rwkv, z_shift, z_conv, prm)
        for acc, s in zip(outs_p, st):
            acc.append(s)
        ys, *st = _layer(ys, p_sample[i], state_gla[i], state_rwkv[i], state_shift[i],
                         state_ffn_conv[i], prm)
        for acc, s in zip(outs_s, st):
            acc.append(s)
    return (yp, ys, *(jnp.stack(a) for a in outs_p), *(jnp.stack(a) for a in outs_s))
```
